```python
import math
import jax, jax.numpy as jnp
from jax import lax
import numpy as np

D_MODEL = 1024
BATCH = 8
SEQ = 4096
DEPTH = 1
DEC_BATCH = 4
DEC_SEQ = 4096
PAST_LEN = 128

N_FOURIER_GROUPS = 4
FOURIER_GROUP_DIM = D_MODEL // 8
D_FOURIER = N_FOURIER_GROUPS * FOURIER_GROUP_DIM
N_HYENA_GROUPS = 8
D_HYENA = D_MODEL // 2
HYENA_ORDER = 2
SHORT_CONV = 3
POS_BANDS = 16
POS_EMB = 2 * POS_BANDS + 1
FILTER_HIDDEN = 64
SHORT_DECAY_PCT = 0.3
LONG_DECAY_PCT = 1.5
DECAY_TARGET = 1e-2
N_IN = D_FOURIER + (HYENA_ORDER + 1) * D_HYENA + 2 * D_MODEL
N_GROUPS = 4
EXPERTS_PER_GROUP = 8
N_EXPERTS = N_GROUPS * EXPERTS_PER_GROUP
TOP_K = 2
D_EXPERT = D_MODEL // 2
MOE_BLOCK = 128
EPS = 1e-6

kernel_name = 'cond_fnet_hyena_hmoe_encoder'


def rmsnorm(x):
    xf = x.astype(jnp.float32)
    y = xf * lax.rsqrt(jnp.mean(xf * xf, axis=-1, keepdims=True) + EPS)
    return y.astype(x.dtype)


def fourier_mix(u):
    B, L, _ = u.shape
    ug = u.reshape(B, L, N_FOURIER_GROUPS, FOURIER_GROUP_DIM).astype(jnp.float32)
    f = jnp.fft.fftn(ug, axes=(1, 3), norm='ortho').real
    return f.reshape(B, L, D_FOURIER).astype(u.dtype)


def short_conv_centred(u, w, b):
    L = u.shape[1]
    up = jnp.pad(u, ((0, 0), (1, 1), (0, 0)))
    return up[:, :L] * w[0] + up[:, 1:L + 1] * w[1] + up[:, 2:] * w[2] + b


def hyena_filters(L, w1, b1, w2, b2, w3, b3, freq, wout):
    f32 = jnp.float32
    t = jnp.linspace(0.0, 1.0, L, dtype=f32)[:, None]
    bands = jnp.linspace(1e-4, POS_BANDS - 1, POS_BANDS, dtype=f32)
    ang = (2.0 * math.pi / L) * jnp.arange(L, dtype=f32)[:, None] * bands
    z = jnp.concatenate([t, jnp.cos(ang), -jnp.sin(ang)], axis=-1)
    h = jnp.sin(freq[0].astype(f32) * (z @ w1.astype(f32) + b1.astype(f32)))
    h = jnp.sin(freq[1].astype(f32) * (h @ w2.astype(f32) + b2.astype(f32)))
    h = jnp.sin(freq[2].astype(f32) * (h @ w3.astype(f32) + b3.astype(f32)))
    h = (h @ wout.astype(f32)).reshape(L, HYENA_ORDER, 2, D_HYENA)
    max_decay = math.log(DECAY_TARGET) / SHORT_DECAY_PCT
    min_decay = math.log(DECAY_TARGET) / LONG_DECAY_PCT
    deltas = jnp.abs(jnp.linspace(min_decay, max_decay, D_HYENA, dtype=f32))
    window = jnp.exp(-t * deltas)
    h = h * window[:, None, None, :]
    k = jnp.concatenate([h[:, :, 0],
                         jnp.zeros((1, HYENA_ORDER, D_HYENA), f32),
                         h[:0:-1, :, 1]], axis=0)
    k = k / jnp.sum(jnp.abs(k), axis=0, keepdims=True)
    return jnp.fft.rfft(k, axis=0)


def long_conv(u, k_f):
    L = u.shape[1]
    u_f = jnp.fft.rfft(u.astype(jnp.float32), n=2 * L, axis=1)
    return jnp.fft.irfft(u_f * k_f, n=2 * L, axis=1)[:, :L]


def hyena_mix(u, conv_w, conv_b, skip, k_f):
    u = short_conv_centred(u, conv_w, conv_b)
    v, x1, x2 = jnp.split(u, 3, axis=-1)
    z = v
    for n, gate in enumerate((x1, x2)):
        z = gate * (long_conv(z, k_f[:, n]).astype(z.dtype) + skip[n] * z)
    return z


def hier_moe(h, rw1, rb1, rw2, rb2, wg, wu, wd):
    T, D = h.shape
    f32 = jnp.float32
    hf = h.astype(f32)
    p_grp = jax.nn.softmax(hf @ rw1.astype(f32) + rb1.astype(f32), axis=-1)
    g = jnp.argmax(p_grp, axis=-1)
    p_g = jnp.max(p_grp, axis=-1)
    logits2 = (hf @ rw2.astype(f32) + rb2.astype(f32)).reshape(T, N_GROUPS, EXPERTS_PER_GROUP)
    logits_sel = jnp.take_along_axis(logits2, g[:, None, None], axis=1)[:, 0]
    q = jax.nn.softmax(logits_sel, axis=-1)
    q_top, j_top = lax.top_k(q, TOP_K)
    w_top = p_g[:, None] * q_top / jnp.sum(q_top, axis=-1, keepdims=True)
    e_top = g[:, None] * EXPERTS_PER_GROUP + j_top

    A = T * TOP_K
    flat_e = e_top.reshape(A)
    flat_tok = jnp.repeat(jnp.arange(T, dtype=jnp.int32), TOP_K)
    flat_w = w_top.reshape(A)
    order = jnp.argsort(flat_e)
    se = flat_e[order]
    counts = jnp.bincount(flat_e, length=N_EXPERTS)
    start = jnp.cumsum(counts) - counts
    pcounts = (counts + MOE_BLOCK - 1) // MOE_BLOCK * MOE_BLOCK
    pend = jnp.cumsum(pcounts)
    pstart = pend - pcounts
    dest = pstart[se] + jnp.arange(A, dtype=jnp.int32) - start[se]
    NB = -(-A // MOE_BLOCK) + N_EXPERTS
    rows = NB * MOE_BLOCK
    buf_tok = jnp.full((rows,), T, jnp.int32).at[dest].set(flat_tok[order])
    buf_w = jnp.zeros((rows,), f32).at[dest].set(flat_w[order])
    blk_e = jnp.minimum(jnp.searchsorted(pend, jnp.arange(NB, dtype=jnp.int32) * MOE_BLOCK, side='right'),
                        N_EXPERTS - 1)
    h_pad = jnp.concatenate([h, jnp.zeros((1, D), h.dtype)], axis=0)
    xb = h_pad[buf_tok].reshape(NB, MOE_BLOCK, D)

    def expert_block(args):
        xblk, e = args
        return (jax.nn.silu(xblk @ wg[e]) * (xblk @ wu[e])) @ wd[e]

    yb = lax.map(expert_block, (xb, blk_e)).reshape(rows, D)
    yb = yb * buf_w[:, None].astype(yb.dtype)
    return jnp.zeros((T + 1, D), yb.dtype).at[buf_tok].add(yb)[:T]


def encoder_layer(x, c, p):
    B, L, D = x.shape
    mod = jax.nn.silu(c) @ p['w_ada'] + p['b_ada']
    sh1, sc1, gt1, sh2, sc2, gt2 = jnp.split(mod[:, None, :], 6, axis=-1)

    h = rmsnorm(x) * (1 + sc1) + sh1
    proj = h @ p['w_in'] + p['b_in']
    c1 = D_FOURIER
    c2 = c1 + (HYENA_ORDER + 1) * D_HYENA
    c3 = c2 + D_MODEL
    u_f, u_h, g_a, g_b = jnp.split(proj, [c1, c2, c3], axis=-1)
    y_a = fourier_mix(u_f) @ p['w_four']
    k_f = hyena_filters(L, p['filt_w1'], p['filt_b1'], p['filt_w2'], p['filt_b2'],
                        p['filt_w3'], p['filt_b3'], p['filt_freq'], p['filt_wout'])
    y_b = hyena_mix(u_h, p['conv_w'], p['conv_b'], p['hyena_skip'], k_f) @ p['w_hyena']
    merged = (jax.nn.sigmoid(g_a) * y_a + jax.nn.sigmoid(g_b) * y_b) @ p['w_out']
    x = x + gt1 * merged

    h2 = rmsnorm(x) * (1 + sc2) + sh2
    moe = hier_moe(h2.reshape(B * L, D), p['router_w1'], p['router_b1'], p['router_w2'],
                   p['router_b2'], p['exp_w_gate'], p['exp_w_up'], p['exp_w_down'])
    return x + gt2 * moe.reshape(B, L, D)


def setup_inputs(seed: int = 0) -> dict:
    key = jax.random.key(seed)
    ks = iter(jax.random.split(key, 40))

    def nrm(shape, scale):
        return scale * jax.random.normal(next(ks), shape, jnp.float32)

    Dp = DEPTH
    return {
        'x_prompt': nrm((BATCH, SEQ, D_MODEL), 1.0),
        'x_sample': nrm((DEC_BATCH, DEC_SEQ, D_MODEL), 1.0),
        'c_prompt': nrm((BATCH, D_MODEL), 1.0),
        'c_sample': nrm((DEC_BATCH, D_MODEL), 1.0),
        'w_ada': nrm((Dp, D_MODEL, 6 * D_MODEL), 0.5 * D_MODEL ** -0.5),
        'b_ada': nrm((Dp, 6 * D_MODEL), 0.02),
        'w_in': nrm((Dp, D_MODEL, N_IN), D_MODEL ** -0.5),
        'b_in': nrm((Dp, N_IN), 0.02),
        'conv_w': nrm((Dp, SHORT_CONV, (HYENA_ORDER + 1) * D_HYENA), SHORT_CONV ** -0.5),
        'conv_b': nrm((Dp, (HYENA_ORDER + 1) * D_HYENA), 0.02),
        'filt_w1': nrm((Dp, POS_EMB, FILTER_HIDDEN), POS_EMB ** -0.5),
        'filt_b1': nrm((Dp, FILTER_HIDDEN), 0.1),
        'filt_w2': nrm((Dp, FILTER_HIDDEN, FILTER_HIDDEN), FILTER_HIDDEN ** -0.5),
        'filt_b2': nrm((Dp, FILTER_HIDDEN), 0.1),
        'filt_w3': nrm((Dp, FILTER_HIDDEN, FILTER_HIDDEN), FILTER_HIDDEN ** -0.5),
        'filt_b3': nrm((Dp, FILTER_HIDDEN), 0.1),
        'filt_freq': 1.0 + nrm((Dp, 3, FILTER_HIDDEN), 0.1),
        'filt_wout': nrm((Dp, FILTER_HIDDEN, HYENA_ORDER * 2 * D_HYENA), FILTER_HIDDEN ** -0.5),
        'hyena_skip': nrm((Dp, HYENA_ORDER, D_HYENA), 0.5),
        'w_four': nrm((Dp, D_FOURIER, D_MODEL), D_FOURIER ** -0.5),
        'w_hyena': nrm((Dp, D_HYENA, D_MODEL), D_HYENA ** -0.5),
        'w_out': nrm((Dp, D_MODEL, D_MODEL), D_MODEL ** -0.5),
        'router_w1': nrm((Dp, D_MODEL, N_GROUPS), D_MODEL ** -0.5),
        'router_b1': nrm((Dp, N_GROUPS), 0.01),
        'router_w2': nrm((Dp, D_MODEL, N_EXPERTS), D_MODEL ** -0.5),
        'router_b2': nrm((Dp, N_EXPERTS), 0.01),
        'exp_w_gate': nrm((Dp, N_EXPERTS, D_MODEL, D_EXPERT), D_MODEL ** -0.5),
        'exp_w_up': nrm((Dp, N_EXPERTS, D_MODEL, D_EXPERT), D_MODEL ** -0.5),
        'exp_w_down': nrm((Dp, N_EXPERTS, D_EXPERT, D_MODEL), D_EXPERT ** -0.5),
        'g_final': 1.0 + nrm((D_MODEL,), 0.1),
    }


def reference(x_prompt, x_sample, c_prompt, c_sample, w_ada, b_ada, w_in, b_in, conv_w, conv_b,
              filt_w1, filt_b1, filt_w2, filt_b2, filt_w3, filt_b3, filt_freq, filt_wout,
              hyena_skip, w_four, w_hyena, w_out, router_w1, router_b1, router_w2, router_b2,
              exp_w_gate, exp_w_up, exp_w_down, g_final):
    xp = x_prompt
    xs = x_sample
    for l in range(DEPTH):
        p = dict(w_ada=w_ada[l], b_ada=b_ada[l], w_in=w_in[l], b_in=b_in[l],
                 conv_w=conv_w[l], conv_b=conv_b[l],
                 filt_w1=filt_w1[l], filt_b1=filt_b1[l], filt_w2=filt_w2[l], filt_b2=filt_b2[l],
                 filt_w3=filt_w3[l], filt_b3=filt_b3[l], filt_freq=filt_freq[l],
                 filt_wout=filt_wout[l], hyena_skip=hyena_skip[l],
                 w_four=w_four[l], w_hyena=w_hyena[l], w_out=w_out[l],
                 router_w1=router_w1[l], router_b1=router_b1[l],
                 router_w2=router_w2[l], router_b2=router_b2[l],
                 exp_w_gate=exp_w_gate[l], exp_w_up=exp_w_up[l], exp_w_down=exp_w_down[l])
        xp = encoder_layer(xp, c_prompt, p)
        xs = encoder_layer(xs, c_sample, p)
    y_prompt = rmsnorm(xp) * g_final
    y_sample = rmsnorm(xs) * g_final
    return (y_prompt, y_sample)
```

```python
import functools
import math

import numpy as np
import jax
import jax.numpy as jnp
from jax import lax
from jax.experimental import pallas as pl
from jax.experimental.pallas import tpu as pltpu

F32 = jnp.float32
BF16 = jnp.bfloat16
HI = lax.Precision.HIGHEST

EPS = 1e-6
FFT_MINOR = 64
N_FOURIER_GROUPS = 4
FOURIER_GROUP_DIM = 128
D_HYENA = 512
HYENA_ORDER = 2
POS_BANDS = 16
N_GROUPS = 4
EXPERTS_PER_GROUP = 8
N_EXPERTS = 32
TOP_K = 2
SHORT_DECAY_PCT = 0.3
LONG_DECAY_PCT = 1.5
DECAY_TARGET = 1e-2
LANES = 128
VMEM_LIMIT = 56 * 1024 * 1024

TOKEN_TILE = 256
MOE_BLOCK = 256
HYENA_COLS = 128
FOURIER_COLS = 128
CONV_CHUNK = 256


def _dot(a, b):
    return jnp.dot(a, b, preferred_element_type=F32)


def _dot_hi(a, b):
    return jnp.dot(a, b, preferred_element_type=F32, precision=HI)


def _params(sem=None):
    return pltpu.CompilerParams(dimension_semantics=sem, vmem_limit_bytes=VMEM_LIMIT)


def _const_spec(shape):
    nd = len(shape)
    return pl.BlockSpec(shape, lambda *_: (0,) * nd, pipeline_mode=pl.Buffered(1))


@functools.lru_cache(maxsize=None)
def _stage2_tables():
    i = np.arange(FFT_MINOR)
    ph = 2.0 * np.pi * np.outer(i, i) / FFT_MINOR
    c, s = np.cos(ph), np.sin(ph)
    fwd = np.block([[c, s], [-s, c]])
    inv = np.block([[c, -s], [s, c]])
    return fwd, inv, c, s


@functools.lru_cache(maxsize=None)
def _hyena_tables(L):
    n_fft = 2 * L
    na = n_fft // FFT_MINOR
    ah = na // 2
    b = np.arange(FFT_MINOR)[:, None, None]
    ka = np.arange(na)[None, :, None]

    def theta(a_count):
        a = np.arange(a_count)[None, None, :]
        return 2.0 * np.pi * ((ka * (FFT_MINOR * a + b)) % n_fft) / n_fft

    th = theta(ah)
    fwd_half = np.concatenate([np.cos(th), -np.sin(th)], axis=1)
    inv_half = np.transpose(fwd_half, (0, 2, 1)) / n_fft
    thf = theta(na)
    fwd_full = np.concatenate([np.cos(thf), -np.sin(thf)], axis=1)
    return fwd_half, inv_half, fwd_full


@functools.lru_cache(maxsize=None)
def _fourier_tables(L):
    na = L // FFT_MINOR
    b = np.arange(FFT_MINOR)[:, None, None]
    ka = np.arange(na)[None, :, None]
    a = np.arange(na)[None, None, :]
    th = 2.0 * np.pi * ((ka * (FFT_MINOR * a + b)) % L) / L
    c, s = np.cos(th), np.sin(th)
    st1 = np.concatenate([np.concatenate([c, s], axis=2),
                          np.concatenate([-s, c], axis=2)], axis=1)
    _, _, c2, s2 = _stage2_tables()
    st2 = np.concatenate([c2, s2], axis=1) / math.sqrt(L)
    return st1, st2


@functools.lru_cache(maxsize=None)
def _channel_dft():
    i = np.arange(FOURIER_GROUP_DIM)
    ph = 2.0 * np.pi * np.outer(i, i) / FOURIER_GROUP_DIM
    return np.concatenate([np.cos(ph), -np.sin(ph)], axis=1) / math.sqrt(FOURIER_GROUP_DIM)


def _ada_kernel(c_ref, w_ref, b_ref, o_ref):
    c = c_ref[...]
    o_ref[...] = _dot_hi(c * jax.nn.sigmoid(c), w_ref[...]) + b_ref[...]


def _ada(c, w_ada, b_ada):
    nb, d = c.shape
    n = w_ada.shape[1]
    tn = 1536
    return pl.pallas_call(
        _ada_kernel,
        grid=(n // tn,),
        in_specs=[pl.BlockSpec((nb, d), lambda j: (0, 0)),
                  pl.BlockSpec((d, tn), lambda j: (0, j)),
                  pl.BlockSpec((1, tn), lambda j: (0, j))],
        out_specs=pl.BlockSpec((nb, tn), lambda j: (0, j)),
        out_shape=jax.ShapeDtypeStruct((nb, n), F32),
        compiler_params=_params(("arbitrary",)),
        name="ada",
    )(c, w_ada, b_ada.reshape(1, n))


def _fold_kernel(w_ref, f_ref, re_ref, im_ref):
    z = _dot_hi(w_ref[...], f_ref[...])
    re_ref[...] = z[:, :FOURIER_GROUP_DIM]
    im_ref[...] = z[:, FOURIER_GROUP_DIM:]


def _fold_channel_dft(wb):
    rows = wb.shape[0]
    gd = FOURIER_GROUP_DIM
    fmat = jnp.asarray(_channel_dft(), F32)
    return pl.pallas_call(
        _fold_kernel,
        grid=(N_FOURIER_GROUPS,),
        in_specs=[pl.BlockSpec((rows, gd), lambda g: (0, g)),
                  pl.BlockSpec((gd, 2 * gd), lambda g: (0, 0))],
        out_specs=[pl.BlockSpec((rows, gd), lambda g: (0, g)),
                   pl.BlockSpec((rows, gd), lambda g: (0, g))],
        out_shape=[jax.ShapeDtypeStruct((rows, N_FOURIER_GROUPS * gd), F32)] * 2,
        compiler_params=_params(("arbitrary",)),
        name="fold",
    )(wb, fmat)


def _filt_time_kernel(L, rows, bands_ref, delt_ref, w1_ref, b1_ref, w2_ref, b2_ref, w3_ref, b3_ref,
                      fr_ref, wo_ref, k_ref, sum_ref):
    i = pl.program_id(0)
    n = i * rows + lax.broadcasted_iota(jnp.int32, (rows, 1), 0)
    fwd = n < L
    pos = jnp.where(fwd, n, 2 * L - n).astype(F32)
    t = pos * (1.0 / (L - 1))
    ang = (2.0 * math.pi / L) * pos * bands_ref[...]
    w1 = w1_ref[...]
    pre = (t * w1[0:1, :] + _dot_hi(jnp.cos(ang), w1[1:1 + POS_BANDS, :])
           + _dot_hi(-jnp.sin(ang), w1[1 + POS_BANDS:, :]) + b1_ref[...])
    fr = fr_ref[...]
    h = jnp.sin(fr[0:1, :] * pre)
    h = jnp.sin(fr[1:2, :] * (_dot_hi(h, w2_ref[...]) + b2_ref[...]))
    h = jnp.sin(fr[2:3, :] * (_dot_hi(h, w3_ref[...]) + b3_ref[...]))
    window = jnp.exp(-t * delt_ref[...])
    live = n != L
    parts = []
    for o in range(HYENA_ORDER):
        base = o * 2 * D_HYENA
        hf = _dot_hi(h, wo_ref[:, base:base + D_HYENA])
        hb = _dot_hi(h, wo_ref[:, base + D_HYENA:base + 2 * D_HYENA])
        parts.append(jnp.where(live, jnp.where(fwd, hf, hb) * window, 0.0))
    k = jnp.concatenate(parts, axis=1)
    k_ref[...] = k

    @pl.when(i == 0)
    def _():
        sum_ref[...] = jnp.zeros_like(sum_ref)

    sum_ref[...] += jnp.sum(jnp.abs(k), axis=0, keepdims=True)


def _filt_time(L, w1, b1, w2, b2, w3, b3, freq, wout):
    rows = 512
    n_fft = 2 * L
    cols = HYENA_ORDER * D_HYENA
    bands = jnp.linspace(1e-4, POS_BANDS - 1, POS_BANDS, dtype=F32).reshape(1, POS_BANDS)
    max_decay = math.log(DECAY_TARGET) / SHORT_DECAY_PCT
    min_decay = math.log(DECAY_TARGET) / LONG_DECAY_PCT
    deltas = jnp.abs(jnp.linspace(min_decay, max_decay, D_HYENA, dtype=F32)).reshape(1, D_HYENA)
    args = (bands, deltas, w1, b1.reshape(1, -1), w2, b2.reshape(1, -1), w3, b3.reshape(1, -1), freq, wout)
    return pl.pallas_call(
        functools.partial(_filt_time_kernel, L, rows),
        grid=(n_fft // rows,),
        in_specs=[pl.BlockSpec(a.shape, lambda i: (0, 0)) for a in args],
        out_specs=[pl.BlockSpec((rows, cols), lambda i: (i, 0)),
                   pl.BlockSpec((1, cols), lambda i: (0, 0))],
        out_shape=[jax.ShapeDtypeStruct((n_fft, cols), F32),
                   jax.ShapeDtypeStruct((1, cols), F32)],
        compiler_params=_params(("arbitrary",)),
        name="filt_time",
    )(*args)


def _filt_fft_kernel(na, k_ref, sum_ref, st1_ref, st2_ref, o_ref, s1):
    def stage1(b, carry):
        slab = k_ref[pl.ds(b, na, stride=FFT_MINOR), :]
        s1[pl.ds(pl.multiple_of(b * 2 * na, 2 * na), 2 * na), :] = _dot_hi(st1_ref[b], slab)
        return carry

    lax.fori_loop(0, FFT_MINOR, stage1, 0)
    inv_norm = 1.0 / sum_ref[...]

    def stage2(ka, carry):
        ar = s1[pl.ds(ka, FFT_MINOR, stride=2 * na), :]
        ai = s1[pl.ds(na + ka, FFT_MINOR, stride=2 * na), :]
        o_ref[ka] = _dot_hi(st2_ref[...], jnp.concatenate([ar, ai], axis=0)) * inv_norm
        return carry

    lax.fori_loop(0, na, stage2, 0)


def _filt_fft(L, ktime, ksum):
    n_fft, cols = ktime.shape
    na = n_fft // FFT_MINOR
    ct = LANES
    _, _, fwd_full = _hyena_tables(L)
    st2f, _, _, _ = _stage2_tables()
    return pl.pallas_call(
        functools.partial(_filt_fft_kernel, na),
        grid=(cols // ct,),
        in_specs=[pl.BlockSpec((n_fft, ct), lambda j: (0, j)),
                  pl.BlockSpec((1, ct), lambda j: (0, j)),
                  _const_spec((FFT_MINOR, 2 * na, na)),
                  _const_spec((2 * FFT_MINOR, 2 * FFT_MINOR))],
        out_specs=pl.BlockSpec((na, 2 * FFT_MINOR, ct), lambda j: (0, 0, j)),
        out_shape=jax.ShapeDtypeStruct((na, 2 * FFT_MINOR, cols), F32),
        scratch_shapes=[pltpu.VMEM((FFT_MINOR * 2 * na, ct), F32)],
        compiler_params=_params(("arbitrary",)),
        name="filt_fft",
    )(ktime, ksum, jnp.asarray(fwd_full, F32), jnp.asarray(st2f, F32))


def _rms_mod(x, scale, shift):
    y = x * lax.rsqrt(jnp.mean(x * x, axis=-1, keepdims=True) + EPS)
    return y * (1.0 + scale) + shift


def _inproj_kernel(d_f, d_h, x_ref, sc_ref, sh_ref, w_ref, b_ref, zr_ref, zi_ref, uh_ref, sg_ref):
    h = _rms_mod(x_ref[...], sc_ref[0], sh_ref[0]).astype(BF16)

    def proj(c0, width):
        return _dot(h, w_ref[:, c0:c0 + width]) + b_ref[:, c0:c0 + width]

    zr_ref[...] = proj(0, d_f)
    zi_ref[...] = proj(d_f, d_f)
    chunk = 512
    for c in range(0, d_h, chunk):
        uh_ref[:, c:c + chunk] = proj(2 * d_f + c, chunk)
    d_g = sg_ref.shape[1]
    for c in range(0, d_g, chunk):
        sg_ref[:, c:c + chunk] = jax.nn.sigmoid(proj(2 * d_f + d_h + c, chunk)).astype(BF16)


def _inproj(x2d, sc, sh, w, b, L, d_f, d_h, d_g):
    T, D = x2d.shape
    tm = TOKEN_TILE
    per_b = L // tm
    n = w.shape[1]
    mod_spec = pl.BlockSpec((1, 1, D), lambda i: (i // per_b, 0, 0))
    return pl.pallas_call(
        functools.partial(_inproj_kernel, d_f, d_h),
        grid=(T // tm,),
        in_specs=[pl.BlockSpec((tm, D), lambda i: (i, 0)), mod_spec, mod_spec,
                  _const_spec((D, n)), _const_spec((1, n))],
        out_specs=[pl.BlockSpec((tm, d_f), lambda i: (i, 0)),
                   pl.BlockSpec((tm, d_f), lambda i: (i, 0)),
                   pl.BlockSpec((tm, d_h), lambda i: (i, 0)),
                   pl.BlockSpec((tm, d_g), lambda i: (i, 0))],
        out_shape=[jax.ShapeDtypeStruct((T, d_f), F32), jax.ShapeDtypeStruct((T, d_f), F32),
                   jax.ShapeDtypeStruct((T, d_h), F32), jax.ShapeDtypeStruct((T, d_g), BF16)],
        compiler_params=_params(("parallel",)),
        name="inproj",
    )(x2d, sc, sh, w, b)


def _fourier_kernel(na, zr_ref, zi_ref, st1_ref, st2_ref, o_ref, s1):
    def stage1(b, carry):
        zr = zr_ref[0, pl.ds(b, na, stride=FFT_MINOR), :]
        zi = zi_ref[0, pl.ds(b, na, stride=FFT_MINOR), :]
        z = jnp.concatenate([zr, zi], axis=0).astype(BF16)
        s1[pl.ds(pl.multiple_of(b * 2 * na, 2 * na), 2 * na), :] = _dot(st1_ref[b], z)
        return carry

    lax.fori_loop(0, FFT_MINOR, stage1, 0)

    def stage2(ka, carry):
        ar = s1[pl.ds(ka, FFT_MINOR, stride=2 * na), :]
        ai = s1[pl.ds(na + ka, FFT_MINOR, stride=2 * na), :]
        a = jnp.concatenate([ar, ai], axis=0).astype(BF16)
        o_ref[0, pl.ds(ka, FFT_MINOR, stride=na), :] = _dot(st2_ref[...], a)
        return carry

    lax.fori_loop(0, na, stage2, 0)


def _fourier(zr, zi):
    B, L, C = zr.shape
    na = L // FFT_MINOR
    ct = FOURIER_COLS
    st1, st2 = _fourier_tables(L)
    spec = pl.BlockSpec((1, L, ct), lambda j, b: (b, 0, j))
    return pl.pallas_call(
        functools.partial(_fourier_kernel, na),
        grid=(C // ct, B),
        in_specs=[spec, spec, _const_spec((FFT_MINOR, 2 * na, 2 * na)),
                  _const_spec((FFT_MINOR, 2 * FFT_MINOR))],
        out_specs=spec,
        out_shape=jax.ShapeDtypeStruct((B, L, C), F32),
        scratch_shapes=[pltpu.VMEM((FFT_MINOR * 2 * na, ct), F32)],
        compiler_params=_params(("parallel", "parallel")),
        name="fourier",
    )(zr, zi, jnp.asarray(st1, BF16), jnp.asarray(st2, BF16))


def _short_conv(dst, src_ref, w_ref, b_ref, L):
    R = CONV_CHUNK
    ct = dst.shape[1]
    nchunk = L // R
    w0, w1, w2 = w_ref[0:1, :], w_ref[1:2, :], w_ref[2:3, :]
    bias = b_ref[...]
    row = lax.broadcasted_iota(jnp.int32, (R, ct), 0)

    def body(j, carry):
        r0 = pl.multiple_of(j * R, R)
        cur = src_ref[0, pl.ds(r0, R), :]
        before = src_ref[0, pl.ds(jnp.maximum(r0 - 1, 0), 1), :] * jnp.where(j > 0, 1.0, 0.0)
        after = src_ref[0, pl.ds(jnp.minimum(r0 + R, L - 1), 1), :] * jnp.where(j < nchunk - 1, 1.0, 0.0)
        up = jnp.where(row == 0, before, pltpu.roll(cur, 1, 0))
        dn = jnp.where(row == R - 1, after, pltpu.roll(cur, R - 1, 0))
        dst[pl.ds(r0, R), :] = w0 * up + w1 * cur + w2 * dn + bias
        return carry

    lax.fori_loop(0, nchunk, body, 0)


def _hyena_kernel(conv_z, L, zin_ref, gin_ref, cwz_ref, cbz_ref, cwg_ref, cbg_ref, skip_ref, kf_ref,
                  fwd1_ref, inv1_ref, st2f_ref, st2i_ref, out_ref, zc, gc, s1, g2):
    na = 2 * L // FFT_MINOR
    ah = na // 2
    m = FFT_MINOR
    _short_conv(gc, gin_ref, cwg_ref, cbg_ref, L)
    if conv_z:
        _short_conv(zc, zin_ref, cwz_ref, cbz_ref, L)

    def z_slab(b):
        if conv_z:
            return zc[pl.ds(b, ah, stride=m), :]
        return zin_ref[0, pl.ds(b, ah, stride=m), :]

    def stage1(b, carry):
        s1[pl.ds(pl.multiple_of(b * 2 * na, 2 * na), 2 * na), :] = _dot(fwd1_ref[b], z_slab(b).astype(BF16))
        return carry

    lax.fori_loop(0, m, stage1, 0)

    def stage2(ka, carry):
        ar = s1[pl.ds(ka, m, stride=2 * na), :]
        ai = s1[pl.ds(na + ka, m, stride=2 * na), :]
        x = _dot(st2f_ref[...], jnp.concatenate([ar, ai], axis=0).astype(BF16))
        xr, xi = x[:m], x[m:]
        k = kf_ref[ka]
        kr, ki = k[:m], k[m:]
        y = jnp.concatenate([xr * kr - xi * ki, xr * ki + xi * kr], axis=0).astype(BF16)
        g2[pl.ds(pl.multiple_of(ka * 2 * m, 2 * m), 2 * m), :] = _dot(st2i_ref[...], y)
        return carry

    lax.fori_loop(0, na, stage2, 0)
    skip = skip_ref[...]

    def stage3(b, carry):
        gr = g2[pl.ds(b, na, stride=2 * m), :]
        gi = g2[pl.ds(m + b, na, stride=2 * m), :]
        y = _dot(inv1_ref[b], jnp.concatenate([gr, gi], axis=0).astype(BF16))
        z = z_slab(b)
        gate = gc[pl.ds(b, ah, stride=m), :]
        out_ref[0, pl.ds(b, ah, stride=m), :] = gate * (y + skip * z)
        return carry

    lax.fori_loop(0, m, stage3, 0)


def _hyena_order(order, zin, z_col0, uh, g_col0, conv_w, conv_b, skip, kf, L):
    B = uh.shape[0]
    ct = HYENA_COLS
    ncol = D_HYENA // ct
    na = 2 * L // FFT_MINOR
    ah = na // 2
    conv_z = order == 0
    fwd_half, inv_half, _ = _hyena_tables(L)
    st2f, st2i, _, _ = _stage2_tables()
    z_cols = (lambda j, b: (b, 0, z_col0 + j))
    g_cols = (lambda j, b: (b, 0, g_col0 + j))
    zw_col0 = z_col0 if conv_z else 0
    kernel = functools.partial(_hyena_kernel, conv_z, L)
    return pl.pallas_call(
        kernel,
        grid=(ncol, B),
        in_specs=[pl.BlockSpec((1, L, ct), z_cols),
                  pl.BlockSpec((1, L, ct), g_cols),
                  pl.BlockSpec((3, ct), lambda j, b: (0, zw_col0 + j)),
                  pl.BlockSpec((1, ct), lambda j, b: (0, zw_col0 + j)),
                  pl.BlockSpec((3, ct), lambda j, b: (0, g_col0 + j)),
                  pl.BlockSpec((1, ct), lambda j, b: (0, g_col0 + j)),
                  pl.BlockSpec((1, ct), lambda j, b: (0, j)),
                  pl.BlockSpec((na, 2 * FFT_MINOR, ct), lambda j, b: (0, 0, order * ncol + j),
                               pipeline_mode=pl.Buffered(1)),
                  _const_spec((FFT_MINOR, 2 * na, ah)),
                  _const_spec((FFT_MINOR, ah, 2 * na)),
                  _const_spec((2 * FFT_MINOR, 2 * FFT_MINOR)),
                  _const_spec((2 * FFT_MINOR, 2 * FFT_MINOR))],
        out_specs=pl.BlockSpec((1, L, ct), lambda j, b: (b, 0, j)),
        out_shape=jax.ShapeDtypeStruct((B, L, D_HYENA), F32),
        scratch_shapes=[pltpu.VMEM((L, ct), F32), pltpu.VMEM((L, ct), F32),
                        pltpu.VMEM((FFT_MINOR * 2 * na, ct), F32),
                        pltpu.VMEM((na * 2 * FFT_MINOR, ct), F32)],
        compiler_params=_params(("parallel", "arbitrary")),
        name=f"hyena{order}",
    )(zin, uh, conv_w, conv_b, conv_w, conv_b, skip, kf,
      jnp.asarray(fwd_half, BF16), jnp.asarray(inv_half, BF16),
      jnp.asarray(st2f, BF16), jnp.asarray(st2i, BF16))


def _merge_kernel(f_ref, z_ref, sg_ref, x_ref, gt_ref, sc_ref, sh_ref, wf_ref, wh_ref, wo_ref,
                  rw_ref, rb_ref, tri_ref, x1_ref, h2_ref, route_ref, cnt_ref):
    i = pl.program_id(0)
    D = x_ref.shape[1]
    ya = _dot(f_ref[...].astype(BF16), wf_ref[...])
    yb = _dot(z_ref[...].astype(BF16), wh_ref[...])
    merged = sg_ref[:, :D].astype(F32) * ya + sg_ref[:, D:].astype(F32) * yb
    x1 = x_ref[...] + gt_ref[0] * _dot(merged.astype(BF16), wo_ref[...])
    x1_ref[...] = x1
    h2 = _rms_mod(x1, sc_ref[0], sh_ref[0])
    h2_ref[...] = h2

    logits = _dot_hi(h2, rw_ref[...]) + rb_ref[...]
    tm = logits.shape[0]
    lane = lax.broadcasted_iota(jnp.int32, (tm, LANES), 1).astype(F32)
    neg = -1e30

    def first_max(v):
        mx = jnp.max(v, axis=1, keepdims=True)
        return mx, jnp.min(jnp.where(v == mx, lane, float(LANES)), axis=1, keepdims=True)

    gl = jnp.where(lane < N_GROUPS, logits, neg)
    gmax, g = first_max(gl)
    p_g = 1.0 / jnp.sum(jnp.exp(gl - gmax), axis=1, keepdims=True)
    lo = N_GROUPS + EXPERTS_PER_GROUP * g
    el = jnp.where((lane >= lo) & (lane < lo + EXPERTS_PER_GROUP), logits, neg)
    m1, i1 = first_max(el)
    m2, i2 = first_max(jnp.where(lane == i1, neg, el))
    r = jnp.exp(m2 - m1)
    wt1 = p_g / (1.0 + r)
    wt2 = p_g * r / (1.0 + r)
    e1 = i1 - N_GROUPS
    e2 = i2 - N_GROUPS

    @pl.when(i == 0)
    def _():
        cnt_ref[...] = jnp.zeros_like(cnt_ref)

    onehot = ((lane == e1) | (lane == e2)).astype(BF16)
    before = _dot(tri_ref[...], onehot) + cnt_ref[...]
    r1 = jnp.sum(jnp.where(lane == e1, before, 0.0), axis=1, keepdims=True)
    r2 = jnp.sum(jnp.where(lane == e2, before, 0.0), axis=1, keepdims=True)
    cnt_ref[...] += jnp.sum(onehot.astype(F32), axis=0, keepdims=True)

    vals = (e1, e2, wt1, wt2, r1, r2)
    packed = jnp.zeros((tm, LANES), F32)
    for slot, v in enumerate(vals):
        packed = jnp.where(lane == slot, v, packed)
    route_ref[...] = packed


def _merge(f2d, z2d, sg, x2d, gt1, sc2, sh2, w_four, w_hyena, w_out, rw, rb, L):
    T, D = x2d.shape
    tm = TOKEN_TILE
    per_b = L // tm
    d_f = f2d.shape[1]
    d_h = z2d.shape[1]
    tri = jnp.asarray(np.tril(np.ones((tm, tm)), -1), BF16)
    mod_spec = pl.BlockSpec((1, 1, D), lambda i: (i // per_b, 0, 0))
    row = lambda w: pl.BlockSpec((tm, w), lambda i: (i, 0))
    return pl.pallas_call(
        _merge_kernel,
        grid=(T // tm,),
        in_specs=[row(d_f), row(d_h), row(2 * D), row(D), mod_spec, mod_spec, mod_spec,
                  _const_spec((d_f, D)), _const_spec((d_h, D)), _const_spec((D, D)),
                  _const_spec((D, LANES)), _const_spec((1, LANES)), _const_spec((tm, tm))],
        out_specs=[row(D), row(D), row(LANES), pl.BlockSpec((1, LANES), lambda i: (0, 0))],
        out_shape=[jax.ShapeDtypeStruct((T, D), F32), jax.ShapeDtypeStruct((T, D), F32),
                   jax.ShapeDtypeStruct((T, LANES), F32), jax.ShapeDtypeStruct((1, LANES), F32)],
        compiler_params=_params(("arbitrary",)),
        name="merge",
    )(f2d, z2d, sg, x2d, gt1, sc2, sh2, w_four, w_hyena, w_out, rw, rb, tri)


def _row_gather(idx_ref, n_rows, src_hbm, dst, sem):
    def issue(r, carry):
        pltpu.make_async_copy(src_hbm.at[pl.ds(idx_ref[0, 0, r], 1)], dst.at[pl.ds(r, 1)], sem).start()
        return carry

    lax.fori_loop(0, n_rows, issue, 0, unroll=8)
    pltpu.make_async_copy(src_hbm.at[pl.ds(0, n_rows)], dst, sem).wait()


def _expert_kernel(blk_e_ref, nused_ref, src_ref, h_hbm, wg_ref, wu_ref, wd_ref, y_ref, xbuf, sem):
    i = pl.program_id(0)

    @pl.when(i < nused_ref[0])
    def _():
        _row_gather(src_ref, xbuf.shape[0], h_hbm, xbuf, sem)
        x = xbuf[...].astype(BF16)
        g = _dot(x, wg_ref[0])
        u = _dot(x, wu_ref[0])
        a = (g * jax.nn.sigmoid(g) * u).astype(BF16)
        y_ref[...] = _dot(a, wd_ref[0])

    @pl.when(i >= nused_ref[0])
    def _():
        y_ref[...] = jnp.zeros_like(y_ref)


def _experts(blk_e, nused, src_tok, h2, wg, wu, wd):
    nb = blk_e.shape[0]
    bm = MOE_BLOCK
    T, D = h2.shape
    de = wg.shape[2]
    grid_spec = pltpu.PrefetchScalarGridSpec(
        num_scalar_prefetch=2,
        grid=(nb,),
        in_specs=[pl.BlockSpec((1, 1, bm), lambda i, be, nu: (i, 0, 0), memory_space=pltpu.SMEM),
                  pl.BlockSpec(memory_space=pl.ANY),
                  pl.BlockSpec((1, D, de), lambda i, be, nu: (be[i], 0, 0)),
                  pl.BlockSpec((1, D, de), lambda i, be, nu: (be[i], 0, 0)),
                  pl.BlockSpec((1, de, D), lambda i, be, nu: (be[i], 0, 0))],
        out_specs=pl.BlockSpec((bm, D), lambda i, be, nu: (i, 0)),
        scratch_shapes=[pltpu.VMEM((bm, D), F32), pltpu.SemaphoreType.DMA(())],
    )
    return pl.pallas_call(
        _expert_kernel,
        grid_spec=grid_spec,
        out_shape=jax.ShapeDtypeStruct((nb * bm, D), F32),
        compiler_params=_params(("arbitrary",)),
        name="experts",
    )(blk_e, nused, src_tok.reshape(nb, 1, bm), h2, wg, wu, wd)


def _combine_kernel(d1_ref, d2_ref, y_hbm, x1_ref, route_ref, gt_ref, gf_ref, o_ref, buf1, buf2, sem1, sem2):
    tm = x1_ref.shape[0]
    _row_gather(d1_ref, tm, y_hbm, buf1, sem1)
    _row_gather(d2_ref, tm, y_hbm, buf2, sem2)
    route = route_ref[...]
    moe = route[:, 2:3] * buf1[...] + route[:, 3:4] * buf2[...]
    x = x1_ref[...] + gt_ref[0] * moe
    o_ref[...] = x * lax.rsqrt(jnp.mean(x * x, axis=-1, keepdims=True) + EPS) * gf_ref[...]


def _combine(tile0, n_tok, dest1, dest2, yb, x1, route, gt2, g_final, L):
    T, D = x1.shape
    tm = TOKEN_TILE
    per_b = L // tm
    ntile = T // tm
    smem = lambda: pl.BlockSpec((1, 1, tm), lambda i: (tile0 + i, 0, 0), memory_space=pltpu.SMEM)
    return pl.pallas_call(
        _combine_kernel,
        grid=(n_tok // tm,),
        in_specs=[smem(), smem(), pl.BlockSpec(memory_space=pl.ANY),
                  pl.BlockSpec((tm, D), lambda i: (tile0 + i, 0)),
                  pl.BlockSpec((tm, LANES), lambda i: (tile0 + i, 0)),
                  pl.BlockSpec((1, 1, D), lambda i: ((tile0 + i) // per_b, 0, 0)),
                  pl.BlockSpec((1, D), lambda i: (0, 0))],
        out_specs=pl.BlockSpec((tm, D), lambda i: (i, 0)),
        out_shape=jax.ShapeDtypeStruct((n_tok, D), F32),
        scratch_shapes=[pltpu.VMEM((tm, D), F32), pltpu.VMEM((tm, D), F32),
                        pltpu.SemaphoreType.DMA(()), pltpu.SemaphoreType.DMA(())],
        compiler_params=_params(("arbitrary",)),
        name="combine",
    )(dest1.reshape(ntile, 1, tm), dest2.reshape(ntile, 1, tm), yb, x1, route, gt2, g_final.reshape(1, D))


def _encoder(x, c, p, g_final, n_prompt):
    B, L, D = x.shape
    T = B * L
    d_f = N_FOURIER_GROUPS * FOURIER_GROUP_DIM
    d_h = (HYENA_ORDER + 1) * D_HYENA
    d_g = 2 * D

    mod = _ada(c, p["w_ada"], p["b_ada"])
    sh1, sc1, gt1, sh2, sc2, gt2 = [mod[:, k * D:(k + 1) * D].reshape(B, 1, D) for k in range(6)]

    wb = jnp.concatenate([p["w_in"][:, :d_f], jnp.broadcast_to(p["b_in"][None, :d_f], (8, d_f))], axis=0)
    fre, fim = _fold_channel_dft(wb)
    w_all = jnp.concatenate([fre[:D], fim[:D], p["w_in"][:, d_f:]], axis=1).astype(BF16)
    b_all = jnp.concatenate([fre[D], fim[D], p["b_in"][d_f:]])[None, :]
    zr, zi, uh, sg = _inproj(x.reshape(T, D), sc1, sh1, w_all, b_all, L, d_f, d_h, d_g)

    f = _fourier(zr.reshape(B, L, d_f), zi.reshape(B, L, d_f))

    ktime, ksum = _filt_time(L, p["filt_w1"], p["filt_b1"], p["filt_w2"], p["filt_b2"],
                             p["filt_w3"], p["filt_b3"], p["filt_freq"], p["filt_wout"])
    kf = _filt_fft(L, ktime, ksum)
    uh3 = uh.reshape(B, L, d_h)
    conv_b = p["conv_b"][None, :]
    ncol = D_HYENA // HYENA_COLS
    skip = p["hyena_skip"]
    z = _hyena_order(0, uh3, 0, uh3, ncol, p["conv_w"], conv_b, skip[0:1], kf, L)
    z = _hyena_order(1, z, 0, uh3, 2 * ncol, p["conv_w"], conv_b, skip[1:2], kf, L)

    rw = jnp.zeros((D, LANES), F32).at[:, :N_GROUPS].set(p["router_w1"])
    rw = rw.at[:, N_GROUPS:N_GROUPS + N_EXPERTS].set(p["router_w2"])
    rb = jnp.zeros((1, LANES), F32).at[0, :N_GROUPS].set(p["router_b1"])
    rb = rb.at[0, N_GROUPS:N_GROUPS + N_EXPERTS].set(p["router_b2"])
    x1, h2, route, counts = _merge(f.reshape(T, d_f), z.reshape(T, D_HYENA), sg, x.reshape(T, D),
                                   gt1, sc2, sh2, p["w_four"].astype(BF16), p["w_hyena"].astype(BF16),
                                   p["w_out"].astype(BF16), rw, rb, L)

    bm = MOE_BLOCK
    nb = (T * TOP_K) // bm + N_EXPERTS
    e = route[:, 0:2].astype(jnp.int32)
    rank = route[:, 4:6].astype(jnp.int32)
    cnt = counts[0, :N_EXPERTS].astype(jnp.int32)
    pcnt = (cnt + bm - 1) // bm * bm
    pend = jnp.cumsum(pcnt)
    dest = (pend - pcnt)[e] + rank
    tok = jnp.broadcast_to(jnp.arange(T, dtype=jnp.int32)[:, None], (T, TOP_K))
    src_tok = jnp.zeros((nb * bm,), jnp.int32).at[dest.reshape(-1)].set(tok.reshape(-1))
    blk_e = jnp.minimum(jnp.searchsorted(pend, jnp.arange(nb, dtype=jnp.int32) * bm, side="right"),
                        N_EXPERTS - 1).astype(jnp.int32)
    nused = (pend[-1] // bm).astype(jnp.int32).reshape(1)
    yb = _experts(blk_e, nused, src_tok, h2, p["exp_w_gate"].astype(BF16), p["exp_w_up"].astype(BF16),
                  p["exp_w_down"].astype(BF16))

    tm = TOKEN_TILE
    t_prompt = n_prompt * L
    outs = []
    for tile0, n_tok in ((0, t_prompt), (t_prompt // tm, T - t_prompt)):
        outs.append(_combine(tile0, n_tok, dest[:, 0], dest[:, 1], yb, x1, route, gt2, g_final, L))
    return outs[0].reshape(n_prompt, L, D), outs[1].reshape(B - n_prompt, L, D)


def kernel(x_prompt, x_sample, c_prompt, c_sample, w_ada, b_ada, w_in, b_in, conv_w, conv_b, filt_w1, filt_b1, filt_w2, filt_b2, filt_w3, filt_b3, filt_freq, filt_wout, hyena_skip, w_four, w_hyena, w_out, router_w1, router_b1, router_w2, router_b2, exp_w_gate, exp_w_up, exp_w_down, g_final):
    assert w_ada.shape[0] == 1, "single-layer block"
    assert x_prompt.shape[1:] == x_sample.shape[1:], "both request groups share sequence length and width"
    p = dict(w_ada=w_ada[0], b_ada=b_ada[0], w_in=w_in[0], b_in=b_in[0], conv_w=conv_w[0], conv_b=conv_b[0],
             filt_w1=filt_w1[0], filt_b1=filt_b1[0], filt_w2=filt_w2[0], filt_b2=filt_b2[0],
             filt_w3=filt_w3[0], filt_b3=filt_b3[0], filt_freq=filt_freq[0], filt_wout=filt_wout[0],
             hyena_skip=hyena_skip[0], w_four=w_four[0], w_hyena=w_hyena[0], w_out=w_out[0],
             router_w1=router_w1[0], router_b1=router_b1[0], router_w2=router_w2[0], router_b2=router_b2[0],
             exp_w_gate=exp_w_gate[0], exp_w_up=exp_w_up[0], exp_w_down=exp_w_down[0])
    x = jnp.concatenate([x_prompt, x_sample], axis=0)
    c = jnp.concatenate([c_prompt, c_sample], axis=0)
    return _encoder(x, c, p, g_final, x_prompt.shape[0])
```

```python
import functools
import math

import numpy as np
import jax
import jax.numpy as jnp
from jax import lax
from jax.experimental import pallas as pl
from jax.experimental.pallas import tpu as pltpu

F32 = jnp.float32
BF16 = jnp.bfloat16
HI = lax.Precision.HIGHEST

EPS = 1e-6
FFT_MINOR = 64
N_FOURIER_GROUPS = 4
FOURIER_GROUP_DIM = 128
D_HYENA = 512
HYENA_ORDER = 2
POS_BANDS = 16
N_GROUPS = 4
EXPERTS_PER_GROUP = 8
N_EXPERTS = 32
TOP_K = 2
SHORT_DECAY_PCT = 0.3
LONG_DECAY_PCT = 1.5
DECAY_TARGET = 1e-2
LANES = 128
SUBLANES = 8
VMEM_LIMIT = 56 * 1024 * 1024

TOKEN_TILE = 256
MOE_BLOCK = 256
FFT_COLS = LANES
ROW_CHUNK = 256
FFT_UNROLL = 16


def _dot(a, b):
    return jnp.dot(a, b, preferred_element_type=F32)


def _dot_hi(a, b):
    return jnp.dot(a, b, preferred_element_type=F32, precision=HI)


def _params(sem=None):
    return pltpu.CompilerParams(dimension_semantics=sem, vmem_limit_bytes=VMEM_LIMIT)


def _const_spec(shape):
    nd = len(shape)
    return pl.BlockSpec(shape, lambda *_: (0,) * nd, pipeline_mode=pl.Buffered(1))


def _pitch(rows):
    p = -(-rows // SUBLANES)
    return SUBLANES * (p if p % 2 else p + 1)


@functools.lru_cache(maxsize=None)
def _stage2_tables():
    i = np.arange(FFT_MINOR)
    ph = 2.0 * np.pi * np.outer(i, i) / FFT_MINOR
    c, s = np.cos(ph), np.sin(ph)
    fwd = np.block([[c, s], [-s, c]])
    inv = np.block([[c, -s], [s, c]])
    return fwd, inv, c, s


@functools.lru_cache(maxsize=None)
def _hyena_tables(L):
    n_fft = 2 * L
    na = n_fft // FFT_MINOR
    ah = na // 2
    b = np.arange(FFT_MINOR)[:, None, None]
    ka = np.arange(na)[None, :, None]

    def theta(a_count):
        a = np.arange(a_count)[None, None, :]
        return 2.0 * np.pi * ((ka * (FFT_MINOR * a + b)) % n_fft) / n_fft

    th = theta(ah)
    fwd_half = np.concatenate([np.cos(th), -np.sin(th)], axis=1)
    inv_half = np.transpose(fwd_half, (0, 2, 1)) / n_fft
    thf = theta(na)
    fwd_full = np.concatenate([np.cos(thf), -np.sin(thf)], axis=1)
    return fwd_half, inv_half, fwd_full


@functools.lru_cache(maxsize=None)
def _fourier_tables(L):
    na = L // FFT_MINOR
    b = np.arange(FFT_MINOR)[:, None, None]
    ka = np.arange(na)[None, :, None]
    a = np.arange(na)[None, None, :]
    th = 2.0 * np.pi * ((ka * (FFT_MINOR * a + b)) % L) / L
    c, s = np.cos(th), np.sin(th)
    st1 = np.concatenate([np.concatenate([c, s], axis=2),
                          np.concatenate([-s, c], axis=2)], axis=1)
    _, _, c2, s2 = _stage2_tables()
    st2 = np.concatenate([c2, s2], axis=1) / math.sqrt(L)
    return st1, st2


def _mxu_table(table):
    return jnp.asarray(table, F32).astype(BF16)


@functools.lru_cache(maxsize=None)
def _channel_dft():
    i = np.arange(FOURIER_GROUP_DIM)
    ph = 2.0 * np.pi * np.outer(i, i) / FOURIER_GROUP_DIM
    return np.concatenate([np.cos(ph), -np.sin(ph)], axis=1) / math.sqrt(FOURIER_GROUP_DIM)


def _ada_kernel(c_ref, w_ref, b_ref, o_ref):
    c = c_ref[...]
    o_ref[...] = _dot_hi(c * jax.nn.sigmoid(c), w_ref[...]) + b_ref[...]


def _ada(c, w_ada, b_ada):
    nb, d = c.shape
    n = w_ada.shape[1]
    tn = 1536
    return pl.pallas_call(
        _ada_kernel,
        grid=(n // tn,),
        in_specs=[pl.BlockSpec((nb, d), lambda j: (0, 0)),
                  pl.BlockSpec((d, tn), lambda j: (0, j)),
                  pl.BlockSpec((1, tn), lambda j: (0, j))],
        out_specs=pl.BlockSpec((nb, tn), lambda j: (0, j)),
        out_shape=jax.ShapeDtypeStruct((nb, n), F32),
        compiler_params=_params(("arbitrary",)),
        name="ada",
    )(c, w_ada, b_ada.reshape(1, n))


def _fold_kernel(w_ref, f_ref, re_ref, im_ref):
    z = _dot_hi(w_ref[...], f_ref[...])
    re_ref[...] = z[:, :FOURIER_GROUP_DIM]
    im_ref[...] = z[:, FOURIER_GROUP_DIM:]


def _fold_channel_dft(wb):
    rows = wb.shape[0]
    gd = FOURIER_GROUP_DIM
    fmat = jnp.asarray(_channel_dft(), F32)
    return pl.pallas_call(
        _fold_kernel,
        grid=(N_FOURIER_GROUPS,),
        in_specs=[pl.BlockSpec((rows, gd), lambda g: (0, g)),
                  pl.BlockSpec((gd, 2 * gd), lambda g: (0, 0))],
        out_specs=[pl.BlockSpec((rows, gd), lambda g: (0, g)),
                   pl.BlockSpec((rows, gd), lambda g: (0, g))],
        out_shape=[jax.ShapeDtypeStruct((rows, N_FOURIER_GROUPS * gd), F32)] * 2,
        compiler_params=_params(("arbitrary",)),
        name="fold",
    )(wb, fmat)


def _for_row_chunks(n_rows, body):
    def step(j, carry):
        body(j, pl.multiple_of(j * ROW_CHUNK, ROW_CHUNK))
        return carry

    lax.fori_loop(0, n_rows // ROW_CHUNK, step, 0)


def _store_padded(dst, j, val, run=FFT_MINOR):
    pz = _pitch(run)
    per = ROW_CHUNK // run
    for i in range(per):
        dst[pl.ds(pl.multiple_of((j * per + i) * pz, SUBLANES), run), :] = val[i * run:(i + 1) * run]


def _load_padded(src, j, run=FFT_MINOR):
    pz = _pitch(run)
    per = ROW_CHUNK // run
    return jnp.concatenate([src[pl.ds(pl.multiple_of((j * per + i) * pz, SUBLANES), run), :] for i in range(per)],
                           axis=0)


def _short_conv_chunk(src_ref, w_ref, b_ref, L, j, r0):
    R = ROW_CHUNK
    ct = src_ref.shape[-1]
    row = lax.broadcasted_iota(jnp.int32, (R, ct), 0)
    cur = src_ref[0, pl.ds(r0, R), :]
    before = src_ref[0, pl.ds(jnp.maximum(r0 - 1, 0), 1), :] * jnp.where(j > 0, 1.0, 0.0)
    after = src_ref[0, pl.ds(jnp.minimum(r0 + R, L - 1), 1), :] * jnp.where(j < L // R - 1, 1.0, 0.0)
    up = jnp.where(row == 0, before, pltpu.roll(cur, 1, 0))
    dn = jnp.where(row == R - 1, after, pltpu.roll(cur, R - 1, 0))
    return w_ref[0:1, :] * up + w_ref[1:2, :] * cur + w_ref[2:3, :] * dn + b_ref[...]


def _filt_time_kernel(L, rows, bands_ref, delt_ref, w1_ref, b1_ref, w2_ref, b2_ref, w3_ref, b3_ref,
                      fr_ref, wo_ref, k_ref, sum_ref):
    i = pl.program_id(0)
    n = i * rows + lax.broadcasted_iota(jnp.int32, (rows, 1), 0)
    fwd = n < L
    pos = jnp.where(fwd, n, 2 * L - n).astype(F32)
    t = pos * (1.0 / (L - 1))
    ang = (2.0 * math.pi / L) * pos * bands_ref[...]
    w1 = w1_ref[...]
    pre = (t * w1[0:1, :] + _dot_hi(jnp.cos(ang), w1[1:1 + POS_BANDS, :])
           + _dot_hi(-jnp.sin(ang), w1[1 + POS_BANDS:, :]) + b1_ref[...])
    fr = fr_ref[...]
    h = jnp.sin(fr[0:1, :] * pre)
    h = jnp.sin(fr[1:2, :] * (_dot_hi(h, w2_ref[...]) + b2_ref[...]))
    h = jnp.sin(fr[2:3, :] * (_dot_hi(h, w3_ref[...]) + b3_ref[...]))
    window = jnp.exp(-t * delt_ref[...])
    live = n != L
    parts = []
    for o in range(HYENA_ORDER):
        base = o * 2 * D_HYENA
        hf = _dot_hi(h, wo_ref[:, base:base + D_HYENA])
        hb = _dot_hi(h, wo_ref[:, base + D_HYENA:base + 2 * D_HYENA])
        parts.append(jnp.where(live, jnp.where(fwd, hf, hb) * window, 0.0))
    k = jnp.concatenate(parts, axis=1)
    k_ref[...] = k

    @pl.when(i == 0)
    def _():
        sum_ref[...] = jnp.zeros_like(sum_ref)

    sum_ref[...] += jnp.sum(jnp.abs(k), axis=0, keepdims=True)


def _filt_time(L, w1, b1, w2, b2, w3, b3, freq, wout):
    rows = 512
    n_fft = 2 * L
    cols = HYENA_ORDER * D_HYENA
    bands = jnp.linspace(1e-4, POS_BANDS - 1, POS_BANDS, dtype=F32).reshape(1, POS_BANDS)
    max_decay = math.log(DECAY_TARGET) / SHORT_DECAY_PCT
    min_decay = math.log(DECAY_TARGET) / LONG_DECAY_PCT
    deltas = jnp.abs(jnp.linspace(min_decay, max_decay, D_HYENA, dtype=F32)).reshape(1, D_HYENA)
    args = (bands, deltas, w1, b1.reshape(1, -1), w2, b2.reshape(1, -1), w3, b3.reshape(1, -1), freq, wout)
    return pl.pallas_call(
        functools.partial(_filt_time_kernel, L, rows),
        grid=(n_fft // rows,),
        in_specs=[pl.BlockSpec(a.shape, lambda i: (0, 0)) for a in args],
        out_specs=[pl.BlockSpec((rows, cols), lambda i: (i, 0)),
                   pl.BlockSpec((1, cols), lambda i: (0, 0))],
        out_shape=[jax.ShapeDtypeStruct((n_fft, cols), F32),
                   jax.ShapeDtypeStruct((1, cols), F32)],
        compiler_params=_params(("arbitrary",)),
        name="filt_time",
    )(*args)


def _filt_fft_kernel(na, k_ref, sum_ref, st1_ref, st2_ref, o_ref, kp, s1):
    m = FFT_MINOR
    pz, p1 = _pitch(m), _pitch(2 * na)
    _for_row_chunks(na * m, lambda j, r0: _store_padded(kp, j, k_ref[pl.ds(r0, ROW_CHUNK), :]))

    def stage1(b, carry):
        slab = kp[pl.ds(b, na, stride=pz), :]
        s1[pl.ds(pl.multiple_of(b * p1, SUBLANES), 2 * na), :] = _dot_hi(st1_ref[b], slab)
        return carry

    lax.fori_loop(0, m, stage1, 0, unroll=2)
    inv_norm = 1.0 / sum_ref[...]

    def stage2(ka, carry):
        ar = s1[pl.ds(ka, m, stride=p1), :]
        ai = s1[pl.ds(na + ka, m, stride=p1), :]
        o_ref[ka] = _dot_hi(st2_ref[...], jnp.concatenate([ar, ai], axis=0)) * inv_norm
        return carry

    lax.fori_loop(0, na, stage2, 0, unroll=2)


def _filt_fft(L, ktime, ksum):
    n_fft, cols = ktime.shape
    m = FFT_MINOR
    na = n_fft // m
    ct = FFT_COLS
    _, _, fwd_full = _hyena_tables(L)
    st2f, _, _, _ = _stage2_tables()
    return pl.pallas_call(
        functools.partial(_filt_fft_kernel, na),
        grid=(cols // ct,),
        in_specs=[pl.BlockSpec((n_fft, ct), lambda j: (0, j)),
                  pl.BlockSpec((1, ct), lambda j: (0, j)),
                  _const_spec((m, 2 * na, na)),
                  _const_spec((2 * m, 2 * m))],
        out_specs=pl.BlockSpec((na, 2 * m, ct), lambda j: (0, 0, j)),
        out_shape=jax.ShapeDtypeStruct((na, 2 * m, cols), F32),
        scratch_shapes=[pltpu.VMEM((na * _pitch(m), ct), F32),
                        pltpu.VMEM((m * _pitch(2 * na), ct), F32)],
        compiler_params=_params(("arbitrary",)),
        name="filt_fft",
    )(ktime, ksum, jnp.asarray(fwd_full, F32), jnp.asarray(st2f, F32))


def _rms_mod(x, scale, shift):
    y = x * lax.rsqrt(jnp.mean(x * x, axis=-1, keepdims=True) + EPS)
    return y * (1.0 + scale) + shift


def _inproj_kernel(d_f, d_h, x_ref, sc_ref, sh_ref, w_ref, b_ref, zr_ref, zi_ref, uh_ref, sg_ref):
    h = _rms_mod(x_ref[...], sc_ref[0], sh_ref[0]).astype(BF16)

    def proj(c0, width):
        return _dot(h, w_ref[:, c0:c0 + width]) + b_ref[:, c0:c0 + width]

    zr_ref[...] = proj(0, d_f)
    zi_ref[...] = proj(d_f, d_f)
    chunk = 512
    for c in range(0, d_h, chunk):
        uh_ref[:, c:c + chunk] = proj(2 * d_f + c, chunk)
    d_g = sg_ref.shape[1]
    for c in range(0, d_g, chunk):
        sg_ref[:, c:c + chunk] = jax.nn.sigmoid(proj(2 * d_f + d_h + c, chunk)).astype(BF16)


def _inproj(x2d, sc, sh, w, b, L, d_f, d_h, d_g):
    T, D = x2d.shape
    tm = TOKEN_TILE
    per_b = L // tm
    n = w.shape[1]
    mod_spec = pl.BlockSpec((1, 1, D), lambda i: (i // per_b, 0, 0))
    return pl.pallas_call(
        functools.partial(_inproj_kernel, d_f, d_h),
        grid=(T // tm,),
        in_specs=[pl.BlockSpec((tm, D), lambda i: (i, 0)), mod_spec, mod_spec,
                  _const_spec((D, n)), _const_spec((1, n))],
        out_specs=[pl.BlockSpec((tm, d_f), lambda i: (i, 0)),
                   pl.BlockSpec((tm, d_f), lambda i: (i, 0)),
                   pl.BlockSpec((tm, d_h), lambda i: (i, 0)),
                   pl.BlockSpec((tm, d_g), lambda i: (i, 0))],
        out_shape=[jax.ShapeDtypeStruct((T, d_f), F32), jax.ShapeDtypeStruct((T, d_f), F32),
                   jax.ShapeDtypeStruct((T, d_h), F32), jax.ShapeDtypeStruct((T, d_g), BF16)],
        compiler_params=_params(("parallel",)),
        name="inproj",
    )(x2d, sc, sh, w, b)


def _fourier_kernel(na, zr_ref, zi_ref, st1_ref, st2_ref, o_ref, zpr, zpi, s1, op):
    m = FFT_MINOR
    pz, p1 = _pitch(m), _pitch(2 * na)

    def fill(j, r0):
        _store_padded(zpr, j, zr_ref[0, pl.ds(r0, ROW_CHUNK), :])
        _store_padded(zpi, j, zi_ref[0, pl.ds(r0, ROW_CHUNK), :])

    _for_row_chunks(na * m, fill)

    def stage1(b, carry):
        z = jnp.concatenate([zpr[pl.ds(b, na, stride=pz), :], zpi[pl.ds(b, na, stride=pz), :]], axis=0)
        s1[pl.ds(pl.multiple_of(b * p1, SUBLANES), 2 * na), :] = _dot(st1_ref[b], z.astype(BF16))
        return carry

    lax.fori_loop(0, m, stage1, 0, unroll=FFT_UNROLL)

    po = _pitch(na)

    def stage2(ka, carry):
        a = jnp.concatenate([s1[pl.ds(ka, m, stride=p1), :], s1[pl.ds(na + ka, m, stride=p1), :]], axis=0)
        op[pl.ds(ka, m, stride=po), :] = _dot(st2_ref[...], a.astype(BF16))
        return carry

    lax.fori_loop(0, na, stage2, 0, unroll=FFT_UNROLL)

    def unpad(j, r0):
        o_ref[0, pl.ds(r0, ROW_CHUNK), :] = _load_padded(op, j, na)

    _for_row_chunks(na * m, unpad)


def _fourier(zr, zi):
    B, L, C = zr.shape
    m = FFT_MINOR
    na = L // m
    ct = FFT_COLS
    st1, st2 = _fourier_tables(L)
    spec = pl.BlockSpec((1, L, ct), lambda j, b: (b, 0, j))
    pad = pltpu.VMEM((na * _pitch(m), ct), F32)
    return pl.pallas_call(
        functools.partial(_fourier_kernel, na),
        grid=(C // ct, B),
        in_specs=[spec, spec, _const_spec((m, 2 * na, 2 * na)), _const_spec((m, 2 * m))],
        out_specs=spec,
        out_shape=jax.ShapeDtypeStruct((B, L, C), F32),
        scratch_shapes=[pad, pad, pltpu.VMEM((m * _pitch(2 * na), ct), F32),
                        pltpu.VMEM((m * _pitch(na), ct), F32)],
        compiler_params=_params(("parallel", "parallel")),
        name="fourier",
    )(zr, zi, _mxu_table(st1), _mxu_table(st2))


def _hyena_kernel(conv_z, L, zin_ref, gin_ref, cwz_ref, cbz_ref, cwg_ref, cbg_ref, skip_ref, kf_ref,
                  fwd1_ref, inv1_ref, st2f_ref, st2i_ref, out_ref, zp, yp, s1, g2):
    m = FFT_MINOR
    na = 2 * L // m
    ah = na // 2
    pz, p1, p2 = _pitch(m), _pitch(2 * na), _pitch(2 * m)

    def fill(j, r0):
        if conv_z:
            _store_padded(zp, j, _short_conv_chunk(zin_ref, cwz_ref, cbz_ref, L, j, r0))
        else:
            _store_padded(zp, j, zin_ref[0, pl.ds(r0, ROW_CHUNK), :])

    _for_row_chunks(L, fill)

    def stage1(b, carry):
        slab = zp[pl.ds(b, ah, stride=pz), :].astype(BF16)
        s1[pl.ds(pl.multiple_of(b * p1, SUBLANES), 2 * na), :] = _dot(fwd1_ref[b], slab)
        return carry

    lax.fori_loop(0, m, stage1, 0, unroll=FFT_UNROLL)

    def stage2(ka, carry):
        a = jnp.concatenate([s1[pl.ds(ka, m, stride=p1), :], s1[pl.ds(na + ka, m, stride=p1), :]], axis=0)
        x = _dot(st2f_ref[...], a.astype(BF16))
        xr, xi = x[:m], x[m:]
        k = kf_ref[ka]
        kr, ki = k[:m], k[m:]
        y = jnp.concatenate([xr * kr - xi * ki, xr * ki + xi * kr], axis=0).astype(BF16)
        g2[pl.ds(pl.multiple_of(ka * p2, SUBLANES), 2 * m), :] = _dot(st2i_ref[...], y)
        return carry

    lax.fori_loop(0, na, stage2, 0, unroll=FFT_UNROLL)

    def stage3(b, carry):
        g = jnp.concatenate([g2[pl.ds(b, na, stride=p2), :], g2[pl.ds(m + b, na, stride=p2), :]], axis=0)
        yp[pl.ds(b, ah, stride=pz), :] = _dot(inv1_ref[b], g.astype(BF16))
        return carry

    lax.fori_loop(0, m, stage3, 0, unroll=FFT_UNROLL)
    skip = skip_ref[...]

    def finish(j, r0):
        gate = _short_conv_chunk(gin_ref, cwg_ref, cbg_ref, L, j, r0)
        out_ref[0, pl.ds(r0, ROW_CHUNK), :] = gate * (_load_padded(yp, j) + skip * _load_padded(zp, j))

    _for_row_chunks(L, finish)


def _hyena_order(order, zin, z_col0, uh, g_col0, conv_w, conv_b, skip, kf, L):
    B = uh.shape[0]
    m = FFT_MINOR
    ct = FFT_COLS
    ncol = D_HYENA // ct
    na = 2 * L // m
    ah = na // 2
    conv_z = order == 0
    fwd_half, inv_half, _ = _hyena_tables(L)
    st2f, st2i, _, _ = _stage2_tables()
    z_cols = (lambda j, b: (b, 0, z_col0 + j))
    g_cols = (lambda j, b: (b, 0, g_col0 + j))
    zw_col0 = z_col0 if conv_z else 0
    pad = pltpu.VMEM((ah * _pitch(m), ct), F32)
    return pl.pallas_call(
        functools.partial(_hyena_kernel, conv_z, L),
        grid=(ncol, B),
        in_specs=[pl.BlockSpec((1, L, ct), z_cols),
                  pl.BlockSpec((1, L, ct), g_cols),
                  pl.BlockSpec((3, ct), lambda j, b: (0, zw_col0 + j)),
                  pl.BlockSpec((1, ct), lambda j, b: (0, zw_col0 + j)),
                  pl.BlockSpec((3, ct), lambda j, b: (0, g_col0 + j)),
                  pl.BlockSpec((1, ct), lambda j, b: (0, g_col0 + j)),
                  pl.BlockSpec((1, ct), lambda j, b: (0, j)),
                  pl.BlockSpec((na, 2 * m, ct), lambda j, b: (0, 0, order * ncol + j),
                               pipeline_mode=pl.Buffered(1)),
                  _const_spec((m, 2 * na, ah)),
                  _const_spec((m, ah, 2 * na)),
                  _const_spec((2 * m, 2 * m)),
                  _const_spec((2 * m, 2 * m))],
        out_specs=pl.BlockSpec((1, L, ct), lambda j, b: (b, 0, j)),
        out_shape=jax.ShapeDtypeStruct((B, L, D_HYENA), F32),
        scratch_shapes=[pad, pad,
                        pltpu.VMEM((m * _pitch(2 * na), ct), F32),
                        pltpu.VMEM((na * _pitch(2 * m), ct), F32)],
        compiler_params=_params(("parallel", "arbitrary")),
        name=f"hyena{order}",
    )(zin, uh, conv_w, conv_b, conv_w, conv_b, skip, kf,
      _mxu_table(fwd_half), _mxu_table(inv_half), _mxu_table(st2f), _mxu_table(st2i))


def _merge_kernel(f_ref, z_ref, sg_ref, x_ref, gt_ref, sc_ref, sh_ref, wf_ref, wh_ref, wo_ref,
                  rw_ref, rb_ref, tri_ref, x1_ref, h2_ref, route_ref, cnt_ref):
    i = pl.program_id(0)
    D = x_ref.shape[1]
    ya = _dot(f_ref[...].astype(BF16), wf_ref[...])
    yb = _dot(z_ref[...].astype(BF16), wh_ref[...])
    merged = sg_ref[:, :D].astype(F32) * ya + sg_ref[:, D:].astype(F32) * yb
    x1 = x_ref[...] + gt_ref[0] * _dot(merged.astype(BF16), wo_ref[...])
    x1_ref[...] = x1
    h2 = _rms_mod(x1, sc_ref[0], sh_ref[0])
    h2_ref[...] = h2

    logits = _dot_hi(h2, rw_ref[...]) + rb_ref[...]
    tm = logits.shape[0]
    lane = lax.broadcasted_iota(jnp.int32, (tm, LANES), 1).astype(F32)
    neg = -1e30

    def first_max(v):
        mx = jnp.max(v, axis=1, keepdims=True)
        return mx, jnp.min(jnp.where(v == mx, lane, float(LANES)), axis=1, keepdims=True)

    gl = jnp.where(lane < N_GROUPS, logits, neg)
    gmax, g = first_max(gl)
    p_g = 1.0 / jnp.sum(jnp.exp(gl - gmax), axis=1, keepdims=True)
    lo = N_GROUPS + EXPERTS_PER_GROUP * g
    el = jnp.where((lane >= lo) & (lane < lo + EXPERTS_PER_GROUP), logits, neg)
    m1, i1 = first_max(el)
    m2, i2 = first_max(jnp.where(lane == i1, neg, el))
    r = jnp.exp(m2 - m1)
    wt1 = p_g / (1.0 + r)
    wt2 = p_g * r / (1.0 + r)
    e1 = i1 - N_GROUPS
    e2 = i2 - N_GROUPS

    @pl.when(i == 0)
    def _():
        cnt_ref[...] = jnp.zeros_like(cnt_ref)

    onehot = ((lane == e1) | (lane == e2)).astype(BF16)
    before = _dot(tri_ref[...], onehot) + cnt_ref[...]
    r1 = jnp.sum(jnp.where(lane == e1, before, 0.0), axis=1, keepdims=True)
    r2 = jnp.sum(jnp.where(lane == e2, before, 0.0), axis=1, keepdims=True)
    cnt_ref[...] += jnp.sum(onehot.astype(F32), axis=0, keepdims=True)

    vals = (e1, e2, wt1, wt2, r1, r2)
    packed = jnp.zeros((tm, LANES), F32)
    for slot, v in enumerate(vals):
        packed = jnp.where(lane == slot, v, packed)
    route_ref[...] = packed


def _merge(f2d, z2d, sg, x2d, gt1, sc2, sh2, w_four, w_hyena, w_out, rw, rb, L):
    T, D = x2d.shape
    tm = TOKEN_TILE
    per_b = L // tm
    d_f = f2d.shape[1]
    d_h = z2d.shape[1]
    tri = jnp.asarray(np.tril(np.ones((tm, tm)), -1), BF16)
    mod_spec = pl.BlockSpec((1, 1, D), lambda i: (i // per_b, 0, 0))
    row = lambda w: pl.BlockSpec((tm, w), lambda i: (i, 0))
    return pl.pallas_call(
        _merge_kernel,
        grid=(T // tm,),
        in_specs=[row(d_f), row(d_h), row(2 * D), row(D), mod_spec, mod_spec, mod_spec,
                  _const_spec((d_f, D)), _const_spec((d_h, D)), _const_spec((D, D)),
                  _const_spec((D, LANES)), _const_spec((1, LANES)), _const_spec((tm, tm))],
        out_specs=[row(D), row(D), row(LANES), pl.BlockSpec((1, LANES), lambda i: (0, 0))],
        out_shape=[jax.ShapeDtypeStruct((T, D), F32), jax.ShapeDtypeStruct((T, D), F32),
                   jax.ShapeDtypeStruct((T, LANES), F32), jax.ShapeDtypeStruct((1, LANES), F32)],
        compiler_params=_params(("arbitrary",)),
        name="merge",
    )(f2d, z2d, sg, x2d, gt1, sc2, sh2, w_four, w_hyena, w_out, rw, rb, tri)


def _gather_start(idx_ref, src_hbm, dst, sem):
    def issue(r, carry):
        pltpu.make_async_copy(src_hbm.at[pl.ds(idx_ref[0, 0, r], 1)], dst.at[pl.ds(r, 1)], sem).start()
        return carry

    lax.fori_loop(0, dst.shape[0], issue, 0, unroll=8)


def _gather_wait(src_hbm, dst, sem):
    pltpu.make_async_copy(src_hbm.at[pl.ds(0, dst.shape[0])], dst, sem).wait()


def _expert_kernel(blk_e_ref, nused_ref, cur_ref, nxt_ref, h_hbm, wg_ref, wu_ref, wd_ref, y_ref, xbuf, sem):
    i = pl.program_id(0)
    nused = nused_ref[0]
    slot = i % 2

    @pl.when(i == 0)
    def _():
        _gather_start(cur_ref, h_hbm, xbuf.at[0], sem.at[0])

    @pl.when(i + 1 < nused)
    def _():
        _gather_start(nxt_ref, h_hbm, xbuf.at[1 - slot], sem.at[1 - slot])

    @pl.when(i < nused)
    def _():
        _gather_wait(h_hbm, xbuf.at[slot], sem.at[slot])
        x = xbuf[slot].astype(BF16)
        g = _dot(x, wg_ref[0])
        u = _dot(x, wu_ref[0])
        a = (g * jax.nn.sigmoid(g) * u).astype(BF16)
        y_ref[...] = _dot(a, wd_ref[0])

    @pl.when(i >= nused)
    def _():
        y_ref[...] = jnp.zeros_like(y_ref)


def _experts(blk_e, nused, src_tok, h2, wg, wu, wd):
    nb = blk_e.shape[0]
    bm = MOE_BLOCK
    T, D = h2.shape
    de = wg.shape[2]
    idx = src_tok.reshape(nb, 1, bm)
    grid_spec = pltpu.PrefetchScalarGridSpec(
        num_scalar_prefetch=2,
        grid=(nb,),
        in_specs=[pl.BlockSpec((1, 1, bm), lambda i, be, nu: (i, 0, 0), memory_space=pltpu.SMEM),
                  pl.BlockSpec((1, 1, bm), lambda i, be, nu: (jnp.minimum(i + 1, nb - 1), 0, 0),
                               memory_space=pltpu.SMEM),
                  pl.BlockSpec(memory_space=pl.ANY),
                  pl.BlockSpec((1, D, de), lambda i, be, nu: (be[i], 0, 0)),
                  pl.BlockSpec((1, D, de), lambda i, be, nu: (be[i], 0, 0)),
                  pl.BlockSpec((1, de, D), lambda i, be, nu: (be[i], 0, 0))],
        out_specs=pl.BlockSpec((bm, D), lambda i, be, nu: (i, 0)),
        scratch_shapes=[pltpu.VMEM((2, bm, D), F32), pltpu.SemaphoreType.DMA((2,))],
    )
    return pl.pallas_call(
        _expert_kernel,
        grid_spec=grid_spec,
        out_shape=jax.ShapeDtypeStruct((nb * bm, D), F32),
        compiler_params=_params(("arbitrary",)),
        name="experts",
    )(blk_e, nused, idx, idx, h2, wg, wu, wd)


def _combine_kernel(n_steps, cur_ref, nxt_ref, y_hbm, x1_ref, route_ref, gt_ref, gf_ref, o_ref, buf, sem):
    i = pl.program_id(0)
    tm = x1_ref.shape[0]
    slot = i % 2

    @pl.when(i == 0)
    def _():
        _gather_start(cur_ref, y_hbm, buf.at[0], sem.at[0])

    @pl.when(i + 1 < n_steps)
    def _():
        _gather_start(nxt_ref, y_hbm, buf.at[1 - slot], sem.at[1 - slot])

    _gather_wait(y_hbm, buf.at[slot], sem.at[slot])
    route = route_ref[...]
    moe = route[:, 2:3] * buf[slot, 0:tm, :] + route[:, 3:4] * buf[slot, tm:2 * tm, :]
    x = x1_ref[...] + gt_ref[0] * moe
    o_ref[...] = x * lax.rsqrt(jnp.mean(x * x, axis=-1, keepdims=True) + EPS) * gf_ref[...]


def _combine(tile0, n_tok, dest_tiles, yb, x1, route, gt2, g_final, L):
    T, D = x1.shape
    tm = TOKEN_TILE
    per_b = L // tm
    n = n_tok // tm
    last = tile0 + n - 1
    smem = lambda f: pl.BlockSpec((1, 1, 2 * tm), lambda i: (f(i), 0, 0), memory_space=pltpu.SMEM)
    return pl.pallas_call(
        functools.partial(_combine_kernel, n),
        grid=(n,),
        in_specs=[smem(lambda i: tile0 + i), smem(lambda i: jnp.minimum(tile0 + i + 1, last)),
                  pl.BlockSpec(memory_space=pl.ANY),
                  pl.BlockSpec((tm, D), lambda i: (tile0 + i, 0)),
                  pl.BlockSpec((tm, LANES), lambda i: (tile0 + i, 0)),
                  pl.BlockSpec((1, 1, D), lambda i: ((tile0 + i) // per_b, 0, 0)),
                  pl.BlockSpec((1, D), lambda i: (0, 0))],
        out_specs=pl.BlockSpec((tm, D), lambda i: (i, 0)),
        out_shape=jax.ShapeDtypeStruct((n_tok, D), F32),
        scratch_shapes=[pltpu.VMEM((2, 2 * tm, D), F32), pltpu.SemaphoreType.DMA((2,))],
        compiler_params=_params(("arbitrary",)),
        name="combine",
    )(dest_tiles, dest_tiles, yb, x1, route, gt2, g_final.reshape(1, D))


def _encoder(x, c, p, g_final, n_prompt):
    B, L, D = x.shape
    T = B * L
    d_f = N_FOURIER_GROUPS * FOURIER_GROUP_DIM
    d_h = (HYENA_ORDER + 1) * D_HYENA
    d_g = 2 * D

    mod = _ada(c, p["w_ada"], p["b_ada"])
    sh1, sc1, gt1, sh2, sc2, gt2 = [mod[:, k * D:(k + 1) * D].reshape(B, 1, D) for k in range(6)]

    wb = jnp.concatenate([p["w_in"][:, :d_f], jnp.broadcast_to(p["b_in"][None, :d_f], (8, d_f))], axis=0)
    fre, fim = _fold_channel_dft(wb)
    w_all = jnp.concatenate([fre[:D], fim[:D], p["w_in"][:, d_f:]], axis=1).astype(BF16)
    b_all = jnp.concatenate([fre[D], fim[D], p["b_in"][d_f:]])[None, :]
    zr, zi, uh, sg = _inproj(x.reshape(T, D), sc1, sh1, w_all, b_all, L, d_f, d_h, d_g)

    f = _fourier(zr.reshape(B, L, d_f), zi.reshape(B, L, d_f))

    ktime, ksum = _filt_time(L, p["filt_w1"], p["filt_b1"], p["filt_w2"], p["filt_b2"],
                             p["filt_w3"], p["filt_b3"], p["filt_freq"], p["filt_wout"])
    kf = _filt_fft(L, ktime, ksum)
    uh3 = uh.reshape(B, L, d_h)
    conv_b = p["conv_b"][None, :]
    ncol = D_HYENA // FFT_COLS
    skip = p["hyena_skip"]
    z = _hyena_order(0, uh3, 0, uh3, ncol, p["conv_w"], conv_b, skip[0:1], kf, L)
    z = _hyena_order(1, z, 0, uh3, 2 * ncol, p["conv_w"], conv_b, skip[1:2], kf, L)

    rw = jnp.zeros((D, LANES), F32).at[:, :N_GROUPS].set(p["router_w1"])
    rw = rw.at[:, N_GROUPS:N_GROUPS + N_EXPERTS].set(p["router_w2"])
    rb = jnp.zeros((1, LANES), F32).at[0, :N_GROUPS].set(p["router_b1"])
    rb = rb.at[0, N_GROUPS:N_GROUPS + N_EXPERTS].set(p["router_b2"])
    x1, h2, route, counts = _merge(f.reshape(T, d_f), z.reshape(T, D_HYENA), sg, x.reshape(T, D),
                                   gt1, sc2, sh2, p["w_four"].astype(BF16), p["w_hyena"].astype(BF16),
                                   p["w_out"].astype(BF16), rw, rb, L)

    bm = MOE_BLOCK
    tm = TOKEN_TILE
    nb = (T * TOP_K) // bm + N_EXPERTS
    e = route[:, 0:2].astype(jnp.int32)
    rank = route[:, 4:6].astype(jnp.int32)
    cnt = counts[0, :N_EXPERTS].astype(jnp.int32)
    pcnt = (cnt + bm - 1) // bm * bm
    pend = jnp.cumsum(pcnt)
    dest = (pend - pcnt)[e] + rank
    tok = jnp.broadcast_to(jnp.arange(T, dtype=jnp.int32)[:, None], (T, TOP_K))
    src_tok = jnp.zeros((nb * bm,), jnp.int32).at[dest.reshape(-1)].set(tok.reshape(-1))
    blk_row0 = jnp.arange(nb, dtype=jnp.int32)[:, None] * bm
    blk_e = jnp.minimum(jnp.sum((pend[None, :] <= blk_row0).astype(jnp.int32), axis=1), N_EXPERTS - 1)
    nused = (pend[-1] // bm).astype(jnp.int32).reshape(1)
    yb = _experts(blk_e, nused, src_tok, h2, p["exp_w_gate"].astype(BF16), p["exp_w_up"].astype(BF16),
                  p["exp_w_down"].astype(BF16))

    dest_tiles = dest.reshape(T // tm, tm, TOP_K).transpose(0, 2, 1).reshape(T // tm, 1, TOP_K * tm)
    t_prompt = n_prompt * L
    outs = []
    for tile0, n_tok in ((0, t_prompt), (t_prompt // tm, T - t_prompt)):
        outs.append(_combine(tile0, n_tok, dest_tiles, yb, x1, route, gt2, g_final, L))
    return outs[0].reshape(n_prompt, L, D), outs[1].reshape(B - n_prompt, L, D)


def kernel(x_prompt, x_sample, c_prompt, c_sample, w_ada, b_ada, w_in, b_in, conv_w, conv_b, filt_w1, filt_b1, filt_w2, filt_b2, filt_w3, filt_b3, filt_freq, filt_wout, hyena_skip, w_four, w_hyena, w_out, router_w1, router_b1, router_w2, router_b2, exp_w_gate, exp_w_up, exp_w_down, g_final):
    assert w_ada.shape[0] == 1, "single-layer block"
    assert x_prompt.shape[1:] == x_sample.shape[1:], "both request groups share sequence length and width"
    p = dict(w_ada=w_ada[0], b_ada=b_ada[0], w_in=w_in[0], b_in=b_in[0], conv_w=conv_w[0], conv_b=conv_b[0],
             filt_w1=filt_w1[0], filt_b1=filt_b1[0], filt_w2=filt_w2[0], filt_b2=filt_b2[0],
             filt_w3=filt_w3[0], filt_b3=filt_b3[0], filt_freq=filt_freq[0], filt_wout=filt_wout[0],
             hyena_skip=hyena_skip[0], w_four=w_four[0], w_hyena=w_hyena[0], w_out=w_out[0],
             router_w1=router_w1[0], router_b1=router_b1[0], router_w2=router_w2[0], router_b2=router_b2[0],
             exp_w_gate=exp_w_gate[0], exp_w_up=exp_w_up[0], exp_w_down=exp_w_down[0])
    x = jnp.concatenate([x_prompt, x_sample], axis=0)
    c = jnp.concatenate([c_prompt, c_sample], axis=0)
    return _encoder(x, c, p, g_final, x_prompt.shape[0])
```

```python
import functools
import math

import numpy as np
import jax
import jax.numpy as jnp
from jax import lax
from jax.experimental import pallas as pl
from jax.experimental.pallas import tpu as pltpu

F32 = jnp.float32
BF16 = jnp.bfloat16
HI = lax.Precision.HIGHEST

EPS = 1e-6
FFT_MINOR = 64
N_FOURIER_GROUPS = 4
FOURIER_GROUP_DIM = 128
D_HYENA = 512
HYENA_ORDER = 2
POS_BANDS = 16
N_GROUPS = 4
EXPERTS_PER_GROUP = 8
N_EXPERTS = 32
TOP_K = 2
SHORT_DECAY_PCT = 0.3
LONG_DECAY_PCT = 1.5
DECAY_TARGET = 1e-2
LANES = 128
SUBLANES = 8
VMEM_LIMIT = 56 * 1024 * 1024

TOKEN_TILE = 256
MOE_BLOCK = 256
FFT_COLS = LANES
ROW_CHUNK = 256
FFT_UNROLL = 16


def _dot(a, b):
    return jnp.dot(a, b, preferred_element_type=F32)


def _dot_hi(a, b):
    return jnp.dot(a, b, preferred_element_type=F32, precision=HI)


def _params(sem=None):
    return pltpu.CompilerParams(dimension_semantics=sem, vmem_limit_bytes=VMEM_LIMIT)


def _const_spec(shape):
    nd = len(shape)
    return pl.BlockSpec(shape, lambda *_: (0,) * nd, pipeline_mode=pl.Buffered(1))


def _pitch(rows):
    p = -(-rows // SUBLANES)
    return SUBLANES * (p if p % 2 else p + 1)


@functools.lru_cache(maxsize=None)
def _stage2_tables():
    i = np.arange(FFT_MINOR)
    ph = 2.0 * np.pi * np.outer(i, i) / FFT_MINOR
    c, s = np.cos(ph), np.sin(ph)
    fwd = np.block([[c, s], [-s, c]])
    inv = np.block([[c, -s], [s, c]])
    return fwd, inv, c, s


@functools.lru_cache(maxsize=None)
def _hyena_tables(L):
    n_fft = 2 * L
    na = n_fft // FFT_MINOR
    ah = na // 2
    b = np.arange(FFT_MINOR)[:, None, None]
    ka = np.arange(na)[None, :, None]

    def theta(a_count):
        a = np.arange(a_count)[None, None, :]
        return 2.0 * np.pi * ((ka * (FFT_MINOR * a + b)) % n_fft) / n_fft

    th = theta(ah)
    fwd_half = np.concatenate([np.cos(th), -np.sin(th)], axis=1)
    inv_half = np.transpose(fwd_half, (0, 2, 1)) / n_fft
    thf = theta(na)
    fwd_full = np.concatenate([np.cos(thf), -np.sin(thf)], axis=1)
    return fwd_half, inv_half, fwd_full


@functools.lru_cache(maxsize=None)
def _fourier_tables(L):
    na = L // FFT_MINOR
    b = np.arange(FFT_MINOR)[:, None, None]
    ka = np.arange(na)[None, :, None]
    a = np.arange(na)[None, None, :]
    th = 2.0 * np.pi * ((ka * (FFT_MINOR * a + b)) % L) / L
    c, s = np.cos(th), np.sin(th)
    st1 = np.concatenate([np.concatenate([c, s], axis=2),
                          np.concatenate([-s, c], axis=2)], axis=1)
    _, _, c2, s2 = _stage2_tables()
    st2 = np.concatenate([c2, s2], axis=1) / math.sqrt(L)
    return st1, st2


def _mxu_table(table):
    return jnp.asarray(table, F32).astype(BF16)


@functools.lru_cache(maxsize=None)
def _channel_dft():
    i = np.arange(FOURIER_GROUP_DIM)
    ph = 2.0 * np.pi * np.outer(i, i) / FOURIER_GROUP_DIM
    return np.concatenate([np.cos(ph), -np.sin(ph)], axis=1) / math.sqrt(FOURIER_GROUP_DIM)


def _ada_kernel(c_ref, w_ref, b_ref, o_ref):
    c = c_ref[...]
    o_ref[...] = _dot_hi(c * jax.nn.sigmoid(c), w_ref[...]) + b_ref[...]


def _ada(c, w_ada, b_ada):
    nb, d = c.shape
    n = w_ada.shape[1]
    tn = 1536
    return pl.pallas_call(
        _ada_kernel,
        grid=(n // tn,),
        in_specs=[pl.BlockSpec((nb, d), lambda j: (0, 0)),
                  pl.BlockSpec((d, tn), lambda j: (0, j)),
                  pl.BlockSpec((1, tn), lambda j: (0, j))],
        out_specs=pl.BlockSpec((nb, tn), lambda j: (0, j)),
        out_shape=jax.ShapeDtypeStruct((nb, n), F32),
        compiler_params=_params(("arbitrary",)),
        name="ada",
    )(c, w_ada, b_ada.reshape(1, n))


def _fold_kernel(w_ref, f_ref, re_ref, im_ref):
    z = _dot_hi(w_ref[...], f_ref[...])
    re_ref[...] = z[:, :FOURIER_GROUP_DIM]
    im_ref[...] = z[:, FOURIER_GROUP_DIM:]


def _fold_channel_dft(wb):
    rows = wb.shape[0]
    gd = FOURIER_GROUP_DIM
    fmat = jnp.asarray(_channel_dft(), F32)
    return pl.pallas_call(
        _fold_kernel,
        grid=(N_FOURIER_GROUPS,),
        in_specs=[pl.BlockSpec((rows, gd), lambda g: (0, g)),
                  pl.BlockSpec((gd, 2 * gd), lambda g: (0, 0))],
        out_specs=[pl.BlockSpec((rows, gd), lambda g: (0, g)),
                   pl.BlockSpec((rows, gd), lambda g: (0, g))],
        out_shape=[jax.ShapeDtypeStruct((rows, N_FOURIER_GROUPS * gd), F32)] * 2,
        compiler_params=_params(("arbitrary",)),
        name="fold",
    )(wb, fmat)


def _for_row_chunks(n_rows, body):
    def step(j, carry):
        body(j, pl.multiple_of(j * ROW_CHUNK, ROW_CHUNK))
        return carry

    lax.fori_loop(0, n_rows // ROW_CHUNK, step, 0)


def _store_padded(dst, j, val, run=FFT_MINOR):
    pz = _pitch(run)
    per = ROW_CHUNK // run
    for i in range(per):
        dst[pl.ds(pl.multiple_of((j * per + i) * pz, SUBLANES), run), :] = val[i * run:(i + 1) * run]


def _load_padded(src, j, run=FFT_MINOR):
    pz = _pitch(run)
    per = ROW_CHUNK // run
    return jnp.concatenate([src[pl.ds(pl.multiple_of((j * per + i) * pz, SUBLANES), run), :] for i in range(per)],
                           axis=0)


def _short_conv_chunk(src_ref, w_ref, b_ref, L, j, r0):
    R = ROW_CHUNK
    ct = src_ref.shape[-1]
    row = lax.broadcasted_iota(jnp.int32, (R, ct), 0)
    cur = src_ref[0, pl.ds(r0, R), :]
    before = src_ref[0, pl.ds(jnp.maximum(r0 - 1, 0), 1), :] * jnp.where(j > 0, 1.0, 0.0)
    after = src_ref[0, pl.ds(jnp.minimum(r0 + R, L - 1), 1), :] * jnp.where(j < L // R - 1, 1.0, 0.0)
    up = jnp.where(row == 0, before, pltpu.roll(cur, 1, 0))
    dn = jnp.where(row == R - 1, after, pltpu.roll(cur, R - 1, 0))
    return w_ref[0:1, :] * up + w_ref[1:2, :] * cur + w_ref[2:3, :] * dn + b_ref[...]


def _filt_time_kernel(L, rows, bands_ref, delt_ref, w1_ref, b1_ref, w2_ref, b2_ref, w3_ref, b3_ref,
                      fr_ref, wo_ref, k_ref, sum_ref):
    i = pl.program_id(0)
    n = i * rows + lax.broadcasted_iota(jnp.int32, (rows, 1), 0)
    fwd = n < L
    pos = jnp.where(fwd, n, 2 * L - n).astype(F32)
    t = pos * (1.0 / (L - 1))
    ang = (2.0 * math.pi / L) * pos * bands_ref[...]
    w1 = w1_ref[...]
    pre = (t * w1[0:1, :] + _dot_hi(jnp.cos(ang), w1[1:1 + POS_BANDS, :])
           + _dot_hi(-jnp.sin(ang), w1[1 + POS_BANDS:, :]) + b1_ref[...])
    fr = fr_ref[...]
    h = jnp.sin(fr[0:1, :] * pre)
    h = jnp.sin(fr[1:2, :] * (_dot_hi(h, w2_ref[...]) + b2_ref[...]))
    h = jnp.sin(fr[2:3, :] * (_dot_hi(h, w3_ref[...]) + b3_ref[...]))
    window = jnp.exp(-t * delt_ref[...])
    live = n != L
    parts = []
    for o in range(HYENA_ORDER):
        base = o * 2 * D_HYENA
        hf = _dot_hi(h, wo_ref[:, base:base + D_HYENA])
        hb = _dot_hi(h, wo_ref[:, base + D_HYENA:base + 2 * D_HYENA])
        parts.append(jnp.where(live, jnp.where(fwd, hf, hb) * window, 0.0))
    k = jnp.concatenate(parts, axis=1)
    k_ref[...] = k

    @pl.when(i == 0)
    def _():
        sum_ref[...] = jnp.zeros_like(sum_ref)

    sum_ref[...] += jnp.sum(jnp.abs(k), axis=0, keepdims=True)


def _filt_time(L, w1, b1, w2, b2, w3, b3, freq, wout):
    rows = 512
    n_fft = 2 * L
    cols = HYENA_ORDER * D_HYENA
    bands = jnp.linspace(1e-4, POS_BANDS - 1, POS_BANDS, dtype=F32).reshape(1, POS_BANDS)
    max_decay = math.log(DECAY_TARGET) / SHORT_DECAY_PCT
    min_decay = math.log(DECAY_TARGET) / LONG_DECAY_PCT
    deltas = jnp.abs(jnp.linspace(min_decay, max_decay, D_HYENA, dtype=F32)).reshape(1, D_HYENA)
    args = (bands, deltas, w1, b1.reshape(1, -1), w2, b2.reshape(1, -1), w3, b3.reshape(1, -1), freq, wout)
    return pl.pallas_call(
        functools.partial(_filt_time_kernel, L, rows),
        grid=(n_fft // rows,),
        in_specs=[pl.BlockSpec(a.shape, lambda i: (0, 0)) for a in args],
        out_specs=[pl.BlockSpec((rows, cols), lambda i: (i, 0)),
                   pl.BlockSpec((1, cols), lambda i: (0, 0))],
        out_shape=[jax.ShapeDtypeStruct((n_fft, cols), F32),
                   jax.ShapeDtypeStruct((1, cols), F32)],
        compiler_params=_params(("arbitrary",)),
        name="filt_time",
    )(*args)


def _filt_fft_kernel(na, k_ref, sum_ref, st1_ref, st2_ref, o_ref, kp, s1):
    m = FFT_MINOR
    pz, p1 = _pitch(m), _pitch(2 * na)
    _for_row_chunks(na * m, lambda j, r0: _store_padded(kp, j, k_ref[pl.ds(r0, ROW_CHUNK), :]))

    def stage1(b, carry):
        slab = kp[pl.ds(b, na, stride=pz), :]
        s1[pl.ds(pl.multiple_of(b * p1, SUBLANES), 2 * na), :] = _dot_hi(st1_ref[b], slab)
        return carry

    lax.fori_loop(0, m, stage1, 0, unroll=2)
    inv_norm = 1.0 / sum_ref[...]

    def stage2(ka, carry):
        ar = s1[pl.ds(ka, m, stride=p1), :]
        ai = s1[pl.ds(na + ka, m, stride=p1), :]
        o_ref[ka] = _dot_hi(st2_ref[...], jnp.concatenate([ar, ai], axis=0)) * inv_norm
        return carry

    lax.fori_loop(0, na, stage2, 0, unroll=2)


def _filt_fft(L, ktime, ksum):
    n_fft, cols = ktime.shape
    m = FFT_MINOR
    na = n_fft // m
    ct = FFT_COLS
    _, _, fwd_full = _hyena_tables(L)
    st2f, _, _, _ = _stage2_tables()
    return pl.pallas_call(
        functools.partial(_filt_fft_kernel, na),
        grid=(cols // ct,),
        in_specs=[pl.BlockSpec((n_fft, ct), lambda j: (0, j)),
                  pl.BlockSpec((1, ct), lambda j: (0, j)),
                  _const_spec((m, 2 * na, na)),
                  _const_spec((2 * m, 2 * m))],
        out_specs=pl.BlockSpec((na, 2 * m, ct), lambda j: (0, 0, j)),
        out_shape=jax.ShapeDtypeStruct((na, 2 * m, cols), F32),
        scratch_shapes=[pltpu.VMEM((na * _pitch(m), ct), F32),
                        pltpu.VMEM((m * _pitch(2 * na), ct), F32)],
        compiler_params=_params(("arbitrary",)),
        name="filt_fft",
    )(ktime, ksum, jnp.asarray(fwd_full, F32), jnp.asarray(st2f, F32))


def _rms_mod(x, scale, shift):
    y = x * lax.rsqrt(jnp.mean(x * x, axis=-1, keepdims=True) + EPS)
    return y * (1.0 + scale) + shift


def _group_tile(n_first, a_ref, b_ref):
    return jnp.where(pl.program_id(0) < n_first, a_ref[...], b_ref[...])


def _group_specs(n_first, tm, D):
    return [pl.BlockSpec((tm, D), lambda i: (jnp.minimum(i, n_first - 1), 0)),
            pl.BlockSpec((tm, D), lambda i: (jnp.maximum(i - n_first, 0), 0))]


def _inproj_kernel(d_f, d_h, n_first, xa_ref, xb_ref, sc_ref, sh_ref, w_ref, b_ref, zr_ref, zi_ref, uh_ref, sg_ref):
    h = _rms_mod(_group_tile(n_first, xa_ref, xb_ref), sc_ref[0], sh_ref[0]).astype(BF16)

    def proj(c0, width):
        return _dot(h, w_ref[:, c0:c0 + width]) + b_ref[:, c0:c0 + width]

    zr_ref[...] = proj(0, d_f)
    zi_ref[...] = proj(d_f, d_f)
    chunk = 512
    for c in range(0, d_h, chunk):
        uh_ref[:, c:c + chunk] = proj(2 * d_f + c, chunk)
    d_g = sg_ref.shape[1]
    for c in range(0, d_g, chunk):
        sg_ref[:, c:c + chunk] = jax.nn.sigmoid(proj(2 * d_f + d_h + c, chunk)).astype(BF16)


def _inproj(xa, xb, sc, sh, w, b, L, d_f, d_h, d_g):
    D = xa.shape[1]
    T = xa.shape[0] + xb.shape[0]
    tm = TOKEN_TILE
    per_b = L // tm
    n_first = xa.shape[0] // tm
    n = w.shape[1]
    mod_spec = pl.BlockSpec((1, 1, D), lambda i: (i // per_b, 0, 0))
    return pl.pallas_call(
        functools.partial(_inproj_kernel, d_f, d_h, n_first),
        grid=(T // tm,),
        in_specs=_group_specs(n_first, tm, D) + [mod_spec, mod_spec, _const_spec((D, n)), _const_spec((1, n))],
        out_specs=[pl.BlockSpec((tm, d_f), lambda i: (i, 0)),
                   pl.BlockSpec((tm, d_f), lambda i: (i, 0)),
                   pl.BlockSpec((tm, d_h), lambda i: (i, 0)),
                   pl.BlockSpec((tm, d_g), lambda i: (i, 0))],
        out_shape=[jax.ShapeDtypeStruct((T, d_f), F32), jax.ShapeDtypeStruct((T, d_f), F32),
                   jax.ShapeDtypeStruct((T, d_h), F32), jax.ShapeDtypeStruct((T, d_g), BF16)],
        compiler_params=_params(("parallel",)),
        name="inproj",
    )(xa, xb, sc, sh, w, b)


def _fourier_kernel(na, zr_ref, zi_ref, st1_ref, st2_ref, o_ref, zpr, zpi, s1, op):
    m = FFT_MINOR
    pz, p1 = _pitch(m), _pitch(2 * na)

    def fill(j, r0):
        _store_padded(zpr, j, zr_ref[0, pl.ds(r0, ROW_CHUNK), :])
        _store_padded(zpi, j, zi_ref[0, pl.ds(r0, ROW_CHUNK), :])

    _for_row_chunks(na * m, fill)

    def stage1(b, carry):
        z = jnp.concatenate([zpr[pl.ds(b, na, stride=pz), :], zpi[pl.ds(b, na, stride=pz), :]], axis=0)
        s1[pl.ds(pl.multiple_of(b * p1, SUBLANES), 2 * na), :] = _dot(st1_ref[b], z.astype(BF16))
        return carry

    lax.fori_loop(0, m, stage1, 0, unroll=FFT_UNROLL)

    po = _pitch(na)

    def stage2(ka, carry):
        a = jnp.concatenate([s1[pl.ds(ka, m, stride=p1), :], s1[pl.ds(na + ka, m, stride=p1), :]], axis=0)
        op[pl.ds(ka, m, stride=po), :] = _dot(st2_ref[...], a.astype(BF16))
        return carry

    lax.fori_loop(0, na, stage2, 0, unroll=FFT_UNROLL)

    def unpad(j, r0):
        o_ref[0, pl.ds(r0, ROW_CHUNK), :] = _load_padded(op, j, na)

    _for_row_chunks(na * m, unpad)


def _fourier(zr, zi):
    B, L, C = zr.shape
    m = FFT_MINOR
    na = L // m
    ct = FFT_COLS
    st1, st2 = _fourier_tables(L)
    spec = pl.BlockSpec((1, L, ct), lambda j, b: (b, 0, j))
    pad = pltpu.VMEM((na * _pitch(m), ct), F32)
    return pl.pallas_call(
        functools.partial(_fourier_kernel, na),
        grid=(C // ct, B),
        in_specs=[spec, spec, _const_spec((m, 2 * na, 2 * na)), _const_spec((m, 2 * m))],
        out_specs=spec,
        out_shape=jax.ShapeDtypeStruct((B, L, C), F32),
        scratch_shapes=[pad, pad, pltpu.VMEM((m * _pitch(2 * na), ct), F32),
                        pltpu.VMEM((m * _pitch(na), ct), F32)],
        compiler_params=_params(("parallel", "parallel")),
        name="fourier",
    )(zr, zi, _mxu_table(st1), _mxu_table(st2))


def _hyena_kernel(conv_z, L, zin_ref, gin_ref, cwz_ref, cbz_ref, cwg_ref, cbg_ref, skip_ref, kf_ref,
                  fwd1_ref, inv1_ref, st2f_ref, st2i_ref, out_ref, zp, yp, s1, g2):
    m = FFT_MINOR
    na = 2 * L // m
    ah = na // 2
    pz, p1, p2 = _pitch(m), _pitch(2 * na), _pitch(2 * m)

    def fill(j, r0):
        if conv_z:
            _store_padded(zp, j, _short_conv_chunk(zin_ref, cwz_ref, cbz_ref, L, j, r0))
        else:
            _store_padded(zp, j, zin_ref[0, pl.ds(r0, ROW_CHUNK), :])

    _for_row_chunks(L, fill)

    def stage1(b, carry):
        slab = zp[pl.ds(b, ah, stride=pz), :].astype(BF16)
        s1[pl.ds(pl.multiple_of(b * p1, SUBLANES), 2 * na), :] = _dot(fwd1_ref[b], slab)
        return carry

    lax.fori_loop(0, m, stage1, 0, unroll=FFT_UNROLL)

    def stage2(ka, carry):
        a = jnp.concatenate([s1[pl.ds(ka, m, stride=p1), :], s1[pl.ds(na + ka, m, stride=p1), :]], axis=0)
        x = _dot(st2f_ref[...], a.astype(BF16))
        xr, xi = x[:m], x[m:]
        k = kf_ref[ka]
        kr, ki = k[:m], k[m:]
        y = jnp.concatenate([xr * kr - xi * ki, xr * ki + xi * kr], axis=0).astype(BF16)
        g2[pl.ds(pl.multiple_of(ka * p2, SUBLANES), 2 * m), :] = _dot(st2i_ref[...], y)
        return carry

    lax.fori_loop(0, na, stage2, 0, unroll=FFT_UNROLL)

    def stage3(b, carry):
        g = jnp.concatenate([g2[pl.ds(b, na, stride=p2), :], g2[pl.ds(m + b, na, stride=p2), :]], axis=0)
        yp[pl.ds(b, ah, stride=pz), :] = _dot(inv1_ref[b], g.astype(BF16))
        return carry

    lax.fori_loop(0, m, stage3, 0, unroll=FFT_UNROLL)
    skip = skip_ref[...]

    def finish(j, r0):
        gate = _short_conv_chunk(gin_ref, cwg_ref, cbg_ref, L, j, r0)
        out_ref[0, pl.ds(r0, ROW_CHUNK), :] = gate * (_load_padded(yp, j) + skip * _load_padded(zp, j))

    _for_row_chunks(L, finish)


def _hyena_order(order, zin, z_col0, uh, g_col0, conv_w, conv_b, skip, kf, L):
    B = uh.shape[0]
    m = FFT_MINOR
    ct = FFT_COLS
    ncol = D_HYENA // ct
    na = 2 * L // m
    ah = na // 2
    conv_z = order == 0
    fwd_half, inv_half, _ = _hyena_tables(L)
    st2f, st2i, _, _ = _stage2_tables()
    z_cols = (lambda j, b: (b, 0, z_col0 + j))
    g_cols = (lambda j, b: (b, 0, g_col0 + j))
    zw_col0 = z_col0 if conv_z else 0
    pad = pltpu.VMEM((ah * _pitch(m), ct), F32)
    return pl.pallas_call(
        functools.partial(_hyena_kernel, conv_z, L),
        grid=(ncol, B),
        in_specs=[pl.BlockSpec((1, L, ct), z_cols),
                  pl.BlockSpec((1, L, ct), g_cols),
                  pl.BlockSpec((3, ct), lambda j, b: (0, zw_col0 + j)),
                  pl.BlockSpec((1, ct), lambda j, b: (0, zw_col0 + j)),
                  pl.BlockSpec((3, ct), lambda j, b: (0, g_col0 + j)),
                  pl.BlockSpec((1, ct), lambda j, b: (0, g_col0 + j)),
                  pl.BlockSpec((1, ct), lambda j, b: (0, j)),
                  pl.BlockSpec((na, 2 * m, ct), lambda j, b: (0, 0, order * ncol + j),
                               pipeline_mode=pl.Buffered(1)),
                  _const_spec((m, 2 * na, ah)),
                  _const_spec((m, ah, 2 * na)),
                  _const_spec((2 * m, 2 * m)),
                  _const_spec((2 * m, 2 * m))],
        out_specs=pl.BlockSpec((1, L, ct), lambda j, b: (b, 0, j)),
        out_shape=jax.ShapeDtypeStruct((B, L, D_HYENA), F32),
        scratch_shapes=[pad, pad,
                        pltpu.VMEM((m * _pitch(2 * na), ct), F32),
                        pltpu.VMEM((na * _pitch(2 * m), ct), F32)],
        compiler_params=_params(("parallel", "arbitrary")),
        name=f"hyena{order}",
    )(zin, uh, conv_w, conv_b, conv_w, conv_b, skip, kf,
      _mxu_table(fwd_half), _mxu_table(inv_half), _mxu_table(st2f), _mxu_table(st2i))


def _merge_kernel(n_first, f_ref, z_ref, sg_ref, xa_ref, xb_ref, gt_ref, sc_ref, sh_ref, wf_ref, wh_ref, wo_ref,
                  rw_ref, rb_ref, tri_ref, x1_ref, h2_ref, route_ref, rt_ref, cnt_ref):
    i = pl.program_id(0)
    D = xa_ref.shape[1]
    ya = _dot(f_ref[...].astype(BF16), wf_ref[...])
    yb = _dot(z_ref[...].astype(BF16), wh_ref[...])
    merged = sg_ref[:, :D].astype(F32) * ya + sg_ref[:, D:].astype(F32) * yb
    x1 = _group_tile(n_first, xa_ref, xb_ref) + gt_ref[0] * _dot(merged.astype(BF16), wo_ref[...])
    x1_ref[...] = x1
    h2 = _rms_mod(x1, sc_ref[0], sh_ref[0])
    _to_token_tiles(h2_ref, h2)

    logits = _dot_hi(h2, rw_ref[...]) + rb_ref[...]
    tm = logits.shape[0]
    lane = lax.broadcasted_iota(jnp.int32, (tm, LANES), 1).astype(F32)
    neg = -1e30

    def first_max(v):
        mx = jnp.max(v, axis=1, keepdims=True)
        return mx, jnp.min(jnp.where(v == mx, lane, float(LANES)), axis=1, keepdims=True)

    gl = jnp.where(lane < N_GROUPS, logits, neg)
    gmax, g = first_max(gl)
    p_g = 1.0 / jnp.sum(jnp.exp(gl - gmax), axis=1, keepdims=True)
    lo = N_GROUPS + EXPERTS_PER_GROUP * g
    el = jnp.where((lane >= lo) & (lane < lo + EXPERTS_PER_GROUP), logits, neg)
    m1, i1 = first_max(el)
    m2, i2 = first_max(jnp.where(lane == i1, neg, el))
    r = jnp.exp(m2 - m1)
    wt1 = p_g / (1.0 + r)
    wt2 = p_g * r / (1.0 + r)
    e1 = i1 - N_GROUPS
    e2 = i2 - N_GROUPS

    @pl.when(i == 0)
    def _():
        cnt_ref[...] = jnp.zeros_like(cnt_ref)

    onehot = ((lane == e1) | (lane == e2)).astype(BF16)
    before = _dot(tri_ref[...], onehot) + cnt_ref[...]
    r1 = jnp.sum(jnp.where(lane == e1, before, 0.0), axis=1, keepdims=True)
    r2 = jnp.sum(jnp.where(lane == e2, before, 0.0), axis=1, keepdims=True)
    cnt_ref[...] += jnp.sum(onehot.astype(F32), axis=0, keepdims=True)

    vals = (e1, e2, wt1, wt2, r1, r2)
    packed = jnp.zeros((tm, LANES), F32)
    for slot, v in enumerate(vals):
        packed = jnp.where(lane == slot, v, packed)
    route_ref[...] = packed
    rt_ref[...] = packed.T[0:SUBLANES, :]


def _merge(f2d, z2d, sg, xa, xb, gt1, sc2, sh2, w_four, w_hyena, w_out, rw, rb, L):
    D = xa.shape[1]
    T = xa.shape[0] + xb.shape[0]
    assert D == SUBLANES * LANES
    tm = TOKEN_TILE
    per_b = L // tm
    n_first = xa.shape[0] // tm
    d_f = f2d.shape[1]
    d_h = z2d.shape[1]
    tri = jnp.asarray(np.tril(np.ones((tm, tm)), -1), BF16)
    mod_spec = pl.BlockSpec((1, 1, D), lambda i: (i // per_b, 0, 0))
    row = lambda w: pl.BlockSpec((tm, w), lambda i: (i, 0))
    return pl.pallas_call(
        functools.partial(_merge_kernel, n_first),
        grid=(T // tm,),
        in_specs=[row(d_f), row(d_h), row(2 * D)] + _group_specs(n_first, tm, D) + [
            mod_spec, mod_spec, mod_spec,
            _const_spec((d_f, D)), _const_spec((d_h, D)), _const_spec((D, D)),
            _const_spec((D, LANES)), _const_spec((1, LANES)), _const_spec((tm, tm))],
        out_specs=[row(D), pl.BlockSpec((tm * SUBLANES, LANES), lambda i: (i, 0)), row(LANES),
                   pl.BlockSpec((SUBLANES, tm), lambda i: (0, i)), pl.BlockSpec((1, LANES), lambda i: (0, 0))],
        out_shape=[jax.ShapeDtypeStruct((T, D), F32), jax.ShapeDtypeStruct((T * SUBLANES, LANES), F32),
                   jax.ShapeDtypeStruct((T, LANES), F32), jax.ShapeDtypeStruct((SUBLANES, T), F32),
                   jax.ShapeDtypeStruct((1, LANES), F32)],
        compiler_params=_params(("arbitrary",)),
        name="merge",
    )(f2d, z2d, sg, xa, xb, gt1, sc2, sh2, w_four, w_hyena, w_out, rw, rb, tri)


def _to_token_tiles(ref, val):
    n = val.shape[0]
    for s in range(SUBLANES):
        ref[pl.ds(s, n, stride=SUBLANES), :] = val[:, s * LANES:(s + 1) * LANES]


def _from_token_tiles(ref, row0, n):
    return jnp.concatenate([ref[pl.ds(row0 + s, n, stride=SUBLANES), :] for s in range(SUBLANES)], axis=1)


def _gather_start(idx_ref, src_hbm, dst, row0, n_tok, sem):
    def issue(r, carry):
        src = pl.multiple_of(idx_ref[0, 0, r] * SUBLANES, SUBLANES)
        row = pl.multiple_of(row0 + r * SUBLANES, SUBLANES)
        pltpu.make_async_copy(src_hbm.at[pl.ds(src, SUBLANES)], dst.at[pl.ds(row, SUBLANES)], sem).start()
        return carry

    lax.fori_loop(0, n_tok, issue, 0, unroll=8)


def _gather_wait(src_hbm, dst, row0, n_tok, sem):
    rows = n_tok * SUBLANES
    pltpu.make_async_copy(src_hbm.at[pl.ds(0, rows)], dst.at[pl.ds(row0, rows)], sem).wait()


def _expert_kernel(blk_e_ref, nused_ref, cur_ref, nxt_ref, h_hbm, wg_ref, wu_ref, wd_ref, y_ref, xbuf, sem):
    i = pl.program_id(0)
    nused = nused_ref[0]
    slot = i % 2
    bm = MOE_BLOCK
    rows = bm * SUBLANES

    @pl.when(i == 0)
    def _():
        _gather_start(cur_ref, h_hbm, xbuf, 0, bm, sem.at[0])

    @pl.when(i + 1 < nused)
    def _():
        _gather_start(nxt_ref, h_hbm, xbuf, (1 - slot) * rows, bm, sem.at[1 - slot])

    @pl.when(i < nused)
    def _():
        _gather_wait(h_hbm, xbuf, slot * rows, bm, sem.at[slot])
        x = _from_token_tiles(xbuf, slot * rows, bm).astype(BF16)
        g = _dot(x, wg_ref[0])
        u = _dot(x, wu_ref[0])
        a = (g * jax.nn.sigmoid(g) * u).astype(BF16)
        _to_token_tiles(y_ref, _dot(a, wd_ref[0]))

    @pl.when(i >= nused)
    def _():
        y_ref[...] = jnp.zeros_like(y_ref)


def _experts(blk_e, nused, src_tok, h2, wg, wu, wd):
    nb = blk_e.shape[0]
    bm = MOE_BLOCK
    D, de = wg.shape[1:]
    assert D == SUBLANES * LANES and h2.shape[1] == LANES
    idx = src_tok.reshape(nb, 1, bm)
    grid_spec = pltpu.PrefetchScalarGridSpec(
        num_scalar_prefetch=2,
        grid=(nb,),
        in_specs=[pl.BlockSpec((1, 1, bm), lambda i, be, nu: (i, 0, 0), memory_space=pltpu.SMEM),
                  pl.BlockSpec((1, 1, bm), lambda i, be, nu: (jnp.minimum(i + 1, nb - 1), 0, 0),
                               memory_space=pltpu.SMEM),
                  pl.BlockSpec(memory_space=pl.ANY),
                  pl.BlockSpec((1, D, de), lambda i, be, nu: (be[i], 0, 0)),
                  pl.BlockSpec((1, D, de), lambda i, be, nu: (be[i], 0, 0)),
                  pl.BlockSpec((1, de, D), lambda i, be, nu: (be[i], 0, 0))],
        out_specs=pl.BlockSpec((bm * SUBLANES, LANES), lambda i, be, nu: (i, 0)),
        scratch_shapes=[pltpu.VMEM((2 * bm * SUBLANES, LANES), F32), pltpu.SemaphoreType.DMA((2,))],
    )
    return pl.pallas_call(
        _expert_kernel,
        grid_spec=grid_spec,
        out_shape=jax.ShapeDtypeStruct((nb * bm * SUBLANES, LANES), F32),
        compiler_params=_params(("arbitrary",)),
        name="experts",
    )(blk_e, nused, idx, idx, h2, wg, wu, wd)


def _combine_kernel(n_steps, cur_ref, nxt_ref, y_hbm, x1_ref, route_ref, gt_ref, gf_ref, o_ref, buf, sem):
    i = pl.program_id(0)
    tm = x1_ref.shape[0]
    slot = i % 2
    rows = 2 * tm * SUBLANES

    @pl.when(i == 0)
    def _():
        _gather_start(cur_ref, y_hbm, buf, 0, 2 * tm, sem.at[0])

    @pl.when(i + 1 < n_steps)
    def _():
        _gather_start(nxt_ref, y_hbm, buf, (1 - slot) * rows, 2 * tm, sem.at[1 - slot])

    _gather_wait(y_hbm, buf, slot * rows, 2 * tm, sem.at[slot])
    route = route_ref[...]
    y1 = _from_token_tiles(buf, slot * rows, tm)
    y2 = _from_token_tiles(buf, slot * rows + tm * SUBLANES, tm)
    x = x1_ref[...] + gt_ref[0] * (route[:, 2:3] * y1 + route[:, 3:4] * y2)
    o_ref[...] = x * lax.rsqrt(jnp.mean(x * x, axis=-1, keepdims=True) + EPS) * gf_ref[...]


def _combine(tile0, n_tok, dest_tiles, yb, x1, route, gt2, g_final, L):
    T, D = x1.shape
    tm = TOKEN_TILE
    per_b = L // tm
    n = n_tok // tm
    last = tile0 + n - 1
    smem = lambda f: pl.BlockSpec((1, 1, 2 * tm), lambda i: (f(i), 0, 0), memory_space=pltpu.SMEM)
    return pl.pallas_call(
        functools.partial(_combine_kernel, n),
        grid=(n,),
        in_specs=[smem(lambda i: tile0 + i), smem(lambda i: jnp.minimum(tile0 + i + 1, last)),
                  pl.BlockSpec(memory_space=pl.ANY),
                  pl.BlockSpec((tm, D), lambda i: (tile0 + i, 0)),
                  pl.BlockSpec((tm, LANES), lambda i: (tile0 + i, 0)),
                  pl.BlockSpec((1, 1, D), lambda i: ((tile0 + i) // per_b, 0, 0)),
                  pl.BlockSpec((1, D), lambda i: (0, 0))],
        out_specs=pl.BlockSpec((tm, D), lambda i: (i, 0)),
        out_shape=jax.ShapeDtypeStruct((n_tok, D), F32),
        scratch_shapes=[pltpu.VMEM((2 * 2 * tm * SUBLANES, LANES), F32), pltpu.SemaphoreType.DMA((2,))],
        compiler_params=_params(("arbitrary",)),
        name="combine",
    )(dest_tiles, dest_tiles, yb, x1, route, gt2, g_final.reshape(1, D))


def _encoder(xp, xs, c, p, g_final):
    n_prompt, L, D = xp.shape
    B = n_prompt + xs.shape[0]
    T = B * L
    xa, xb = xp.reshape(-1, D), xs.reshape(-1, D)
    d_f = N_FOURIER_GROUPS * FOURIER_GROUP_DIM
    d_h = (HYENA_ORDER + 1) * D_HYENA
    d_g = 2 * D

    mod = _ada(c, p["w_ada"], p["b_ada"])
    sh1, sc1, gt1, sh2, sc2, gt2 = [mod[:, k * D:(k + 1) * D].reshape(B, 1, D) for k in range(6)]

    wb = jnp.concatenate([p["w_in"][:, :d_f], jnp.broadcast_to(p["b_in"][None, :d_f], (8, d_f))], axis=0)
    fre, fim = _fold_channel_dft(wb)
    w_all = jnp.concatenate([fre[:D], fim[:D], p["w_in"][:, d_f:]], axis=1).astype(BF16)
    b_all = jnp.concatenate([fre[D], fim[D], p["b_in"][d_f:]])[None, :]
    zr, zi, uh, sg = _inproj(xa, xb, sc1, sh1, w_all, b_all, L, d_f, d_h, d_g)

    f = _fourier(zr.reshape(B, L, d_f), zi.reshape(B, L, d_f))

    ktime, ksum = _filt_time(L, p["filt_w1"], p["filt_b1"], p["filt_w2"], p["filt_b2"],
                             p["filt_w3"], p["filt_b3"], p["filt_freq"], p["filt_wout"])
    kf = _filt_fft(L, ktime, ksum)
    uh3 = uh.reshape(B, L, d_h)
    conv_b = p["conv_b"][None, :]
    ncol = D_HYENA // FFT_COLS
    skip = p["hyena_skip"]
    z = _hyena_order(0, uh3, 0, uh3, ncol, p["conv_w"], conv_b, skip[0:1], kf, L)
    z = _hyena_order(1, z, 0, uh3, 2 * ncol, p["conv_w"], conv_b, skip[1:2], kf, L)

    rw = jnp.zeros((D, LANES), F32).at[:, :N_GROUPS].set(p["router_w1"])
    rw = rw.at[:, N_GROUPS:N_GROUPS + N_EXPERTS].set(p["router_w2"])
    rb = jnp.zeros((1, LANES), F32).at[0, :N_GROUPS].set(p["router_b1"])
    rb = rb.at[0, N_GROUPS:N_GROUPS + N_EXPERTS].set(p["router_b2"])
    x1, h2, route, route_t, counts = _merge(f.reshape(T, d_f), z.reshape(T, D_HYENA), sg, xa, xb,
                                            gt1, sc2, sh2, p["w_four"].astype(BF16), p["w_hyena"].astype(BF16),
                                            p["w_out"].astype(BF16), rw, rb, L)

    bm = MOE_BLOCK
    tm = TOKEN_TILE
    nb = (T * TOP_K) // bm + N_EXPERTS
    e = route_t[0:2].astype(jnp.int32)
    rank = route_t[4:6].astype(jnp.int32)
    cnt = counts[0, :N_EXPERTS].astype(jnp.int32)
    pcnt = (cnt + bm - 1) // bm * bm
    pend = jnp.cumsum(pcnt)
    dest = (pend - pcnt)[e] + rank
    tok = jnp.broadcast_to(jnp.arange(T, dtype=jnp.int32)[None, :], (TOP_K, T))
    src_tok = jnp.zeros((nb * bm,), jnp.int32).at[dest.reshape(-1)].set(tok.reshape(-1))
    blk_row0 = jnp.arange(nb, dtype=jnp.int32)[:, None] * bm
    blk_e = jnp.minimum(jnp.sum((pend[None, :] <= blk_row0).astype(jnp.int32), axis=1), N_EXPERTS - 1)
    nused = (pend[-1] // bm).astype(jnp.int32).reshape(1)
    yb = _experts(blk_e, nused, src_tok, h2, p["exp_w_gate"].astype(BF16), p["exp_w_up"].astype(BF16),
                  p["exp_w_down"].astype(BF16))

    dest_tiles = dest.reshape(TOP_K, T // tm, tm).transpose(1, 0, 2).reshape(T // tm, 1, TOP_K * tm)
    t_prompt = n_prompt * L
    outs = []
    for tile0, n_tok in ((0, t_prompt), (t_prompt // tm, T - t_prompt)):
        outs.append(_combine(tile0, n_tok, dest_tiles, yb, x1, route, gt2, g_final, L))
    return outs[0].reshape(xp.shape), outs[1].reshape(xs.shape)


def kernel(x_prompt, x_sample, c_prompt, c_sample, w_ada, b_ada, w_in, b_in, conv_w, conv_b, filt_w1, filt_b1, filt_w2, filt_b2, filt_w3, filt_b3, filt_freq, filt_wout, hyena_skip, w_four, w_hyena, w_out, router_w1, router_b1, router_w2, router_b2, exp_w_gate, exp_w_up, exp_w_down, g_final):
    assert w_ada.shape[0] == 1, "single-layer block"
    assert x_prompt.shape[1:] == x_sample.shape[1:], "both request groups share sequence length and width"
    p = dict(w_ada=w_ada[0], b_ada=b_ada[0], w_in=w_in[0], b_in=b_in[0], conv_w=conv_w[0], conv_b=conv_b[0],
             filt_w1=filt_w1[0], filt_b1=filt_b1[0], filt_w2=filt_w2[0], filt_b2=filt_b2[0],
             filt_w3=filt_w3[0], filt_b3=filt_b3[0], filt_freq=filt_freq[0], filt_wout=filt_wout[0],
             hyena_skip=hyena_skip[0], w_four=w_four[0], w_hyena=w_hyena[0], w_out=w_out[0],
             router_w1=router_w1[0], router_b1=router_b1[0], router_w2=router_w2[0], router_b2=router_b2[0],
             exp_w_gate=exp_w_gate[0], exp_w_up=exp_w_up[0], exp_w_down=exp_w_down[0])
    c = jnp.concatenate([c_prompt, c_sample], axis=0)
    return _encoder(x_prompt, x_sample, c, p, g_final)
```

```python
import functools
import math

import numpy as np
import jax
import jax.numpy as jnp
from jax import lax
from jax.experimental import pallas as pl
from jax.experimental.pallas import tpu as pltpu

F32 = jnp.float32
BF16 = jnp.bfloat16
HI = lax.Precision.HIGHEST

EPS = 1e-6
FFT_MINOR = 64
N_FOURIER_GROUPS = 4
FOURIER_GROUP_DIM = 128
D_HYENA = 512
HYENA_ORDER = 2
POS_BANDS = 16
N_GROUPS = 4
EXPERTS_PER_GROUP = 8
N_EXPERTS = 32
TOP_K = 2
SHORT_DECAY_PCT = 0.3
LONG_DECAY_PCT = 1.5
DECAY_TARGET = 1e-2
LANES = 128
SUBLANES = 8
VMEM_LIMIT = 56 * 1024 * 1024

TOKEN_TILE = 512
MERGE_SUBTILE = 128
MOE_BLOCK = 256
FFT_COLS = LANES
ROW_CHUNK = 256
FFT_UNROLL = 16


def _dot(a, b):
    return jnp.dot(a, b, preferred_element_type=F32)


def _dot_hi(a, b):
    return jnp.dot(a, b, preferred_element_type=F32, precision=HI)


def _params(sem=None):
    return pltpu.CompilerParams(dimension_semantics=sem, vmem_limit_bytes=VMEM_LIMIT)


def _const_spec(shape):
    nd = len(shape)
    return pl.BlockSpec(shape, lambda *_: (0,) * nd, pipeline_mode=pl.Buffered(1))


def _pitch(rows):
    p = -(-rows // SUBLANES)
    return SUBLANES * (p if p % 2 else p + 1)


@functools.lru_cache(maxsize=None)
def _stage2_tables():
    i = np.arange(FFT_MINOR)
    ph = 2.0 * np.pi * np.outer(i, i) / FFT_MINOR
    c, s = np.cos(ph), np.sin(ph)
    fwd = np.block([[c, s], [-s, c]])
    inv = np.block([[c, -s], [s, c]])
    return fwd, inv, c, s


@functools.lru_cache(maxsize=None)
def _hyena_tables(L):
    n_fft = 2 * L
    na = n_fft // FFT_MINOR
    ah = na // 2
    b = np.arange(FFT_MINOR)[:, None, None]
    ka = np.arange(na)[None, :, None]

    def theta(a_count):
        a = np.arange(a_count)[None, None, :]
        return 2.0 * np.pi * ((ka * (FFT_MINOR * a + b)) % n_fft) / n_fft

    th = theta(ah)
    fwd_half = np.concatenate([np.cos(th), -np.sin(th)], axis=1)
    inv_half = np.transpose(fwd_half, (0, 2, 1)) / n_fft
    thf = theta(na)
    fwd_full = np.concatenate([np.cos(thf), -np.sin(thf)], axis=1)
    return fwd_half, inv_half, fwd_full


@functools.lru_cache(maxsize=None)
def _fourier_tables(L):
    na = L // FFT_MINOR
    b = np.arange(FFT_MINOR)[:, None, None]
    ka = np.arange(na)[None, :, None]
    a = np.arange(na)[None, None, :]
    th = 2.0 * np.pi * ((ka * (FFT_MINOR * a + b)) % L) / L
    c, s = np.cos(th), np.sin(th)
    st1 = np.concatenate([np.concatenate([c, s], axis=2),
                          np.concatenate([-s, c], axis=2)], axis=1)
    _, _, c2, s2 = _stage2_tables()
    st2 = np.concatenate([c2, s2], axis=1) / math.sqrt(L)
    return st1, st2


def _mxu_table(table):
    return jnp.asarray(table, F32).astype(BF16)


@functools.lru_cache(maxsize=None)
def _channel_dft():
    i = np.arange(FOURIER_GROUP_DIM)
    ph = 2.0 * np.pi * np.outer(i, i) / FOURIER_GROUP_DIM
    return np.concatenate([np.cos(ph), -np.sin(ph)], axis=1) / math.sqrt(FOURIER_GROUP_DIM)


def _ada_kernel(c_ref, w_ref, b_ref, o_ref):
    c = c_ref[...]
    o_ref[...] = _dot_hi(c * jax.nn.sigmoid(c), w_ref[...]) + b_ref[...]


def _ada(c, w_ada, b_ada):
    nb, d = c.shape
    n = w_ada.shape[1]
    tn = 1536
    return pl.pallas_call(
        _ada_kernel,
        grid=(n // tn,),
        in_specs=[pl.BlockSpec((nb, d), lambda j: (0, 0)),
                  pl.BlockSpec((d, tn), lambda j: (0, j)),
                  pl.BlockSpec((1, tn), lambda j: (0, j))],
        out_specs=pl.BlockSpec((nb, tn), lambda j: (0, j)),
        out_shape=jax.ShapeDtypeStruct((nb, n), F32),
        compiler_params=_params(("arbitrary",)),
        name="ada",
    )(c, w_ada, b_ada.reshape(1, n))


def _fold_kernel(w_ref, f_ref, re_ref, im_ref):
    z = _dot_hi(w_ref[...], f_ref[...])
    re_ref[...] = z[:, :FOURIER_GROUP_DIM]
    im_ref[...] = z[:, FOURIER_GROUP_DIM:]


def _fold_channel_dft(wb):
    rows = wb.shape[0]
    gd = FOURIER_GROUP_DIM
    fmat = jnp.asarray(_channel_dft(), F32)
    return pl.pallas_call(
        _fold_kernel,
        grid=(N_FOURIER_GROUPS,),
        in_specs=[pl.BlockSpec((rows, gd), lambda g: (0, g)),
                  pl.BlockSpec((gd, 2 * gd), lambda g: (0, 0))],
        out_specs=[pl.BlockSpec((rows, gd), lambda g: (0, g)),
                   pl.BlockSpec((rows, gd), lambda g: (0, g))],
        out_shape=[jax.ShapeDtypeStruct((rows, N_FOURIER_GROUPS * gd), F32)] * 2,
        compiler_params=_params(("arbitrary",)),
        name="fold",
    )(wb, fmat)


def _for_row_chunks(n_rows, body):
    def step(j, carry):
        body(j, pl.multiple_of(j * ROW_CHUNK, ROW_CHUNK))
        return carry

    lax.fori_loop(0, n_rows // ROW_CHUNK, step, 0)


def _store_padded(dst, j, val, run=FFT_MINOR):
    pz = _pitch(run)
    per = ROW_CHUNK // run
    for i in range(per):
        dst[pl.ds(pl.multiple_of((j * per + i) * pz, SUBLANES), run), :] = val[i * run:(i + 1) * run]


def _load_padded(src, j, run=FFT_MINOR):
    pz = _pitch(run)
    per = ROW_CHUNK // run
    return jnp.concatenate([src[pl.ds(pl.multiple_of((j * per + i) * pz, SUBLANES), run), :] for i in range(per)],
                           axis=0)


def _short_conv_chunk(src_ref, w_ref, b_ref, L, j, r0):
    R = ROW_CHUNK
    ct = src_ref.shape[-1]
    row = lax.broadcasted_iota(jnp.int32, (R, ct), 0)
    cur = src_ref[0, pl.ds(r0, R), :]
    before = src_ref[0, pl.ds(jnp.maximum(r0 - 1, 0), 1), :] * jnp.where(j > 0, 1.0, 0.0)
    after = src_ref[0, pl.ds(jnp.minimum(r0 + R, L - 1), 1), :] * jnp.where(j < L // R - 1, 1.0, 0.0)
    up = jnp.where(row == 0, before, pltpu.roll(cur, 1, 0))
    dn = jnp.where(row == R - 1, after, pltpu.roll(cur, R - 1, 0))
    return w_ref[0:1, :] * up + w_ref[1:2, :] * cur + w_ref[2:3, :] * dn + b_ref[...]


def _filt_time_kernel(L, rows, bands_ref, delt_ref, w1_ref, b1_ref, w2_ref, b2_ref, w3_ref, b3_ref,
                      fr_ref, wo_ref, k_ref, sum_ref):
    i = pl.program_id(0)
    n = i * rows + lax.broadcasted_iota(jnp.int32, (rows, 1), 0)
    fwd = n < L
    pos = jnp.where(fwd, n, 2 * L - n).astype(F32)
    t = pos * (1.0 / (L - 1))
    ang = (2.0 * math.pi / L) * pos * bands_ref[...]
    w1 = w1_ref[...]
    pre = (t * w1[0:1, :] + _dot_hi(jnp.cos(ang), w1[1:1 + POS_BANDS, :])
           + _dot_hi(-jnp.sin(ang), w1[1 + POS_BANDS:, :]) + b1_ref[...])
    fr = fr_ref[...]
    h = jnp.sin(fr[0:1, :] * pre)
    h = jnp.sin(fr[1:2, :] * (_dot_hi(h, w2_ref[...]) + b2_ref[...]))
    h = jnp.sin(fr[2:3, :] * (_dot_hi(h, w3_ref[...]) + b3_ref[...]))
    window = jnp.exp(-t * delt_ref[...])
    live = n != L
    parts = []
    for o in range(HYENA_ORDER):
        base = o * 2 * D_HYENA
        hf = _dot_hi(h, wo_ref[:, base:base + D_HYENA])
        hb = _dot_hi(h, wo_ref[:, base + D_HYENA:base + 2 * D_HYENA])
        parts.append(jnp.where(live, jnp.where(fwd, hf, hb) * window, 0.0))
    k = jnp.concatenate(parts, axis=1)
    k_ref[...] = k

    @pl.when(i == 0)
    def _():
        sum_ref[...] = jnp.zeros_like(sum_ref)

    sum_ref[...] += jnp.sum(jnp.abs(k), axis=0, keepdims=True)


def _filt_time(L, w1, b1, w2, b2, w3, b3, freq, wout):
    rows = 512
    n_fft = 2 * L
    cols = HYENA_ORDER * D_HYENA
    bands = jnp.linspace(1e-4, POS_BANDS - 1, POS_BANDS, dtype=F32).reshape(1, POS_BANDS)
    max_decay = math.log(DECAY_TARGET) / SHORT_DECAY_PCT
    min_decay = math.log(DECAY_TARGET) / LONG_DECAY_PCT
    deltas = jnp.abs(jnp.linspace(min_decay, max_decay, D_HYENA, dtype=F32)).reshape(1, D_HYENA)
    args = (bands, deltas, w1, b1.reshape(1, -1), w2, b2.reshape(1, -1), w3, b3.reshape(1, -1), freq, wout)
    return pl.pallas_call(
        functools.partial(_filt_time_kernel, L, rows),
        grid=(n_fft // rows,),
        in_specs=[pl.BlockSpec(a.shape, lambda i: (0, 0)) for a in args],
        out_specs=[pl.BlockSpec((rows, cols), lambda i: (i, 0)),
                   pl.BlockSpec((1, cols), lambda i: (0, 0))],
        out_shape=[jax.ShapeDtypeStruct((n_fft, cols), F32),
                   jax.ShapeDtypeStruct((1, cols), F32)],
        compiler_params=_params(("arbitrary",)),
        name="filt_time",
    )(*args)


def _filt_fft_kernel(na, k_ref, sum_ref, st1_ref, st2_ref, o_ref, kp, s1):
    m = FFT_MINOR
    pz, p1 = _pitch(m), _pitch(2 * na)
    _for_row_chunks(na * m, lambda j, r0: _store_padded(kp, j, k_ref[pl.ds(r0, ROW_CHUNK), :]))

    def stage1(b, carry):
        slab = kp[pl.ds(b, na, stride=pz), :]
        s1[pl.ds(pl.multiple_of(b * p1, SUBLANES), 2 * na), :] = _dot_hi(st1_ref[b], slab)
        return carry

    lax.fori_loop(0, m, stage1, 0, unroll=2)
    inv_norm = 1.0 / sum_ref[...]

    def stage2(ka, carry):
        ar = s1[pl.ds(ka, m, stride=p1), :]
        ai = s1[pl.ds(na + ka, m, stride=p1), :]
        o_ref[ka] = _dot_hi(st2_ref[...], jnp.concatenate([ar, ai], axis=0)) * inv_norm
        return carry

    lax.fori_loop(0, na, stage2, 0, unroll=2)


def _filt_fft(L, ktime, ksum):
    n_fft, cols = ktime.shape
    m = FFT_MINOR
    na = n_fft // m
    ct = FFT_COLS
    _, _, fwd_full = _hyena_tables(L)
    st2f, _, _, _ = _stage2_tables()
    return pl.pallas_call(
        functools.partial(_filt_fft_kernel, na),
        grid=(cols // ct,),
        in_specs=[pl.BlockSpec((n_fft, ct), lambda j: (0, j)),
                  pl.BlockSpec((1, ct), lambda j: (0, j)),
                  _const_spec((m, 2 * na, na)),
                  _const_spec((2 * m, 2 * m))],
        out_specs=pl.BlockSpec((na, 2 * m, ct), lambda j: (0, 0, j)),
        out_shape=jax.ShapeDtypeStruct((na, 2 * m, cols), F32),
        scratch_shapes=[pltpu.VMEM((na * _pitch(m), ct), F32),
                        pltpu.VMEM((m * _pitch(2 * na), ct), F32)],
        compiler_params=_params(("arbitrary",)),
        name="filt_fft",
    )(ktime, ksum, jnp.asarray(fwd_full, F32), jnp.asarray(st2f, F32))


def _rms_mod(x, scale, shift):
    y = x * lax.rsqrt(jnp.mean(x * x, axis=-1, keepdims=True) + EPS)
    return y * (1.0 + scale) + shift


def _group_tile(n_first, a_ref, b_ref):
    return jnp.where(pl.program_id(0) < n_first, a_ref[...], b_ref[...])


def _group_specs(n_first, tm, D):
    return [pl.BlockSpec((tm, D), lambda i: (jnp.minimum(i, n_first - 1), 0)),
            pl.BlockSpec((tm, D), lambda i: (jnp.maximum(i - n_first, 0), 0))]


def _inproj_kernel(d_f, d_h, n_first, xa_ref, xb_ref, sc_ref, sh_ref, w_ref, b_ref, zr_ref, zi_ref, uh_ref, sg_ref):
    h = _rms_mod(_group_tile(n_first, xa_ref, xb_ref), sc_ref[0], sh_ref[0]).astype(BF16)

    def proj(c0, width):
        return _dot(h, w_ref[:, c0:c0 + width]) + b_ref[:, c0:c0 + width]

    zr_ref[...] = proj(0, d_f)
    zi_ref[...] = proj(d_f, d_f)
    chunk = 512
    for c in range(0, d_h, chunk):
        uh_ref[:, c:c + chunk] = proj(2 * d_f + c, chunk)
    d_g = sg_ref.shape[1]
    for c in range(0, d_g, chunk):
        sg_ref[:, c:c + chunk] = jax.nn.sigmoid(proj(2 * d_f + d_h + c, chunk)).astype(BF16)


def _inproj(xa, xb, sc, sh, w, b, L, d_f, d_h, d_g):
    D = xa.shape[1]
    T = xa.shape[0] + xb.shape[0]
    tm = TOKEN_TILE
    per_b = L // tm
    n_first = xa.shape[0] // tm
    n = w.shape[1]
    mod_spec = pl.BlockSpec((1, 1, D), lambda i: (i // per_b, 0, 0))
    return pl.pallas_call(
        functools.partial(_inproj_kernel, d_f, d_h, n_first),
        grid=(T // tm,),
        in_specs=_group_specs(n_first, tm, D) + [mod_spec, mod_spec, _const_spec((D, n)), _const_spec((1, n))],
        out_specs=[pl.BlockSpec((tm, d_f), lambda i: (i, 0)),
                   pl.BlockSpec((tm, d_f), lambda i: (i, 0)),
                   pl.BlockSpec((tm, d_h), lambda i: (i, 0)),
                   pl.BlockSpec((tm, d_g), lambda i: (i, 0))],
        out_shape=[jax.ShapeDtypeStruct((T, d_f), F32), jax.ShapeDtypeStruct((T, d_f), F32),
                   jax.ShapeDtypeStruct((T, d_h), F32), jax.ShapeDtypeStruct((T, d_g), BF16)],
        compiler_params=_params(("parallel",)),
        name="inproj",
    )(xa, xb, sc, sh, w, b)


def _fourier_kernel(na, zr_ref, zi_ref, st1_ref, st2_ref, o_ref, zpr, zpi, s1, op):
    m = FFT_MINOR
    pz, p1 = _pitch(m), _pitch(2 * na)

    def fill(j, r0):
        _store_padded(zpr, j, zr_ref[0, pl.ds(r0, ROW_CHUNK), :])
        _store_padded(zpi, j, zi_ref[0, pl.ds(r0, ROW_CHUNK), :])

    _for_row_chunks(na * m, fill)

    def stage1(b, carry):
        z = jnp.concatenate([zpr[pl.ds(b, na, stride=pz), :], zpi[pl.ds(b, na, stride=pz), :]], axis=0)
        s1[pl.ds(pl.multiple_of(b * p1, SUBLANES), 2 * na), :] = _dot(st1_ref[b], z.astype(BF16))
        return carry

    lax.fori_loop(0, m, stage1, 0, unroll=FFT_UNROLL)

    po = _pitch(na)

    def stage2(ka, carry):
        a = jnp.concatenate([s1[pl.ds(ka, m, stride=p1), :], s1[pl.ds(na + ka, m, stride=p1), :]], axis=0)
        op[pl.ds(ka, m, stride=po), :] = _dot(st2_ref[...], a.astype(BF16))
        return carry

    lax.fori_loop(0, na, stage2, 0, unroll=FFT_UNROLL)

    def unpad(j, r0):
        o_ref[0, pl.ds(r0, ROW_CHUNK), :] = _load_padded(op, j, na)

    _for_row_chunks(na * m, unpad)


def _fourier(zr, zi):
    B, L, C = zr.shape
    m = FFT_MINOR
    na = L // m
    ct = FFT_COLS
    st1, st2 = _fourier_tables(L)
    spec = pl.BlockSpec((1, L, ct), lambda j, b: (b, 0, j))
    pad = pltpu.VMEM((na * _pitch(m), ct), F32)
    return pl.pallas_call(
        functools.partial(_fourier_kernel, na),
        grid=(C // ct, B),
        in_specs=[spec, spec, _const_spec((m, 2 * na, 2 * na)), _const_spec((m, 2 * m))],
        out_specs=spec,
        out_shape=jax.ShapeDtypeStruct((B, L, C), F32),
        scratch_shapes=[pad, pad, pltpu.VMEM((m * _pitch(2 * na), ct), F32),
                        pltpu.VMEM((m * _pitch(na), ct), F32)],
        compiler_params=_params(("parallel", "parallel")),
        name="fourier",
    )(zr, zi, _mxu_table(st1), _mxu_table(st2))


def _hyena_kernel(conv_z, L, zin_ref, gin_ref, cwz_ref, cbz_ref, cwg_ref, cbg_ref, skip_ref, kf_ref,
                  fwd1_ref, inv1_ref, st2f_ref, st2i_ref, out_ref, zp, yp, s1, g2):
    m = FFT_MINOR
    na = 2 * L // m
    ah = na // 2
    pz, p1, p2 = _pitch(m), _pitch(2 * na), _pitch(2 * m)

    def fill(j, r0):
        if conv_z:
            _store_padded(zp, j, _short_conv_chunk(zin_ref, cwz_ref, cbz_ref, L, j, r0))
        else:
            _store_padded(zp, j, zin_ref[0, pl.ds(r0, ROW_CHUNK), :])

    _for_row_chunks(L, fill)

    def stage1(b, carry):
        slab = zp[pl.ds(b, ah, stride=pz), :].astype(BF16)
        s1[pl.ds(pl.multiple_of(b * p1, SUBLANES), 2 * na), :] = _dot(fwd1_ref[b], slab)
        return carry

    lax.fori_loop(0, m, stage1, 0, unroll=FFT_UNROLL)

    def stage2(ka, carry):
        a = jnp.concatenate([s1[pl.ds(ka, m, stride=p1), :], s1[pl.ds(na + ka, m, stride=p1), :]], axis=0)
        x = _dot(st2f_ref[...], a.astype(BF16))
        xr, xi = x[:m], x[m:]
        k = kf_ref[ka]
        kr, ki = k[:m], k[m:]
        y = jnp.concatenate([xr * kr - xi * ki, xr * ki + xi * kr], axis=0).astype(BF16)
        g2[pl.ds(pl.multiple_of(ka * p2, SUBLANES), 2 * m), :] = _dot(st2i_ref[...], y)
        return carry

    lax.fori_loop(0, na, stage2, 0, unroll=FFT_UNROLL)

    def stage3(b, carry):
        g = jnp.concatenate([g2[pl.ds(b, na, stride=p2), :], g2[pl.ds(m + b, na, stride=p2), :]], axis=0)
        yp[pl.ds(b, ah, stride=pz), :] = _dot(inv1_ref[b], g.astype(BF16))
        return carry

    lax.fori_loop(0, m, stage3, 0, unroll=FFT_UNROLL)
    skip = skip_ref[...]

    def finish(j, r0):
        gate = _short_conv_chunk(gin_ref, cwg_ref, cbg_ref, L, j, r0)
        out_ref[0, pl.ds(r0, ROW_CHUNK), :] = gate * (_load_padded(yp, j) + skip * _load_padded(zp, j))

    _for_row_chunks(L, finish)


def _hyena_order(order, zin, z_col0, uh, g_col0, conv_w, conv_b, skip, kf, L):
    B = uh.shape[0]
    m = FFT_MINOR
    ct = FFT_COLS
    ncol = D_HYENA // ct
    na = 2 * L // m
    ah = na // 2
    conv_z = order == 0
    fwd_half, inv_half, _ = _hyena_tables(L)
    st2f, st2i, _, _ = _stage2_tables()
    z_cols = (lambda j, b: (b, 0, z_col0 + j))
    g_cols = (lambda j, b: (b, 0, g_col0 + j))
    zw_col0 = z_col0 if conv_z else 0
    pad = pltpu.VMEM((ah * _pitch(m), ct), F32)
    return pl.pallas_call(
        functools.partial(_hyena_kernel, conv_z, L),
        grid=(ncol, B),
        in_specs=[pl.BlockSpec((1, L, ct), z_cols),
                  pl.BlockSpec((1, L, ct), g_cols),
                  pl.BlockSpec((3, ct), lambda j, b: (0, zw_col0 + j)),
                  pl.BlockSpec((1, ct), lambda j, b: (0, zw_col0 + j)),
                  pl.BlockSpec((3, ct), lambda j, b: (0, g_col0 + j)),
                  pl.BlockSpec((1, ct), lambda j, b: (0, g_col0 + j)),
                  pl.BlockSpec((1, ct), lambda j, b: (0, j)),
                  pl.BlockSpec((na, 2 * m, ct), lambda j, b: (0, 0, order * ncol + j),
                               pipeline_mode=pl.Buffered(1)),
                  _const_spec((m, 2 * na, ah)),
                  _const_spec((m, ah, 2 * na)),
                  _const_spec((2 * m, 2 * m)),
                  _const_spec((2 * m, 2 * m))],
        out_specs=pl.BlockSpec((1, L, ct), lambda j, b: (b, 0, j)),
        out_shape=jax.ShapeDtypeStruct((B, L, D_HYENA), F32),
        scratch_shapes=[pad, pad,
                        pltpu.VMEM((m * _pitch(2 * na), ct), F32),
                        pltpu.VMEM((na * _pitch(2 * m), ct), F32)],
        compiler_params=_params(("parallel", "arbitrary")),
        name=f"hyena{order}",
    )(zin, uh, conv_w, conv_b, conv_w, conv_b, skip, kf,
      _mxu_table(fwd_half), _mxu_table(inv_half), _mxu_table(st2f), _mxu_table(st2i))


def _merge_kernel(n_first, f_ref, z_ref, sg_ref, xa_ref, xb_ref, gt_ref, sc_ref, sh_ref, wf_ref, wh_ref, wo_ref,
                  rw_ref, rb_ref, tri_ref, x1_ref, h2_ref, route_ref, rt_ref, cnt_ref):
    i = pl.program_id(0)
    tm, D = xa_ref.shape
    sub = MERGE_SUBTILE
    lane = lax.broadcasted_iota(jnp.int32, (sub, LANES), 1).astype(F32)
    neg = -1e30

    def first_max(v):
        mx = jnp.max(v, axis=1, keepdims=True)
        return mx, jnp.min(jnp.where(v == mx, lane, float(LANES)), axis=1, keepdims=True)

    @pl.when(i == 0)
    def _():
        cnt_ref[...] = jnp.zeros_like(cnt_ref)

    def sub_tile(r0, count):
        rows = pl.ds(r0, sub)
        ya = _dot(f_ref[rows, :].astype(BF16), wf_ref[...])
        yb = _dot(z_ref[rows, :].astype(BF16), wh_ref[...])
        merged = sg_ref[rows, :D].astype(F32) * ya + sg_ref[rows, D:].astype(F32) * yb
        x = jnp.where(i < n_first, xa_ref[rows, :], xb_ref[rows, :])
        x1 = x + gt_ref[0] * _dot(merged.astype(BF16), wo_ref[...])
        x1_ref[rows, :] = x1
        h2 = _rms_mod(x1, sc_ref[0], sh_ref[0])
        _to_token_tiles(h2_ref, h2, r0 * SUBLANES)

        logits = _dot_hi(h2, rw_ref[...]) + rb_ref[...]
        gl = jnp.where(lane < N_GROUPS, logits, neg)
        gmax, g = first_max(gl)
        p_g = 1.0 / jnp.sum(jnp.exp(gl - gmax), axis=1, keepdims=True)
        lo = N_GROUPS + EXPERTS_PER_GROUP * g
        el = jnp.where((lane >= lo) & (lane < lo + EXPERTS_PER_GROUP), logits, neg)
        m1, i1 = first_max(el)
        m2, i2 = first_max(jnp.where(lane == i1, neg, el))
        r = jnp.exp(m2 - m1)
        wt1 = p_g / (1.0 + r)
        wt2 = p_g * r / (1.0 + r)
        e1 = i1 - N_GROUPS
        e2 = i2 - N_GROUPS

        onehot = ((lane == e1) | (lane == e2)).astype(BF16)
        before = _dot(tri_ref[...], onehot) + count
        r1 = jnp.sum(jnp.where(lane == e1, before, 0.0), axis=1, keepdims=True)
        r2 = jnp.sum(jnp.where(lane == e2, before, 0.0), axis=1, keepdims=True)

        packed = jnp.zeros((sub, LANES), F32)
        for slot, v in enumerate((e1, e2, wt1, wt2, r1, r2)):
            packed = jnp.where(lane == slot, v, packed)
        route_ref[rows, :] = packed
        rt_ref[:, rows] = packed.T[0:SUBLANES, :]
        return count + jnp.sum(onehot.astype(F32), axis=0, keepdims=True)

    count = cnt_ref[...]
    for h in range(tm // sub):
        count = sub_tile(h * sub, count)
    cnt_ref[...] = count


def _merge(f2d, z2d, sg, xa, xb, gt1, sc2, sh2, w_four, w_hyena, w_out, rw, rb, L):
    D = xa.shape[1]
    T = xa.shape[0] + xb.shape[0]
    assert D == SUBLANES * LANES
    tm = TOKEN_TILE
    per_b = L // tm
    n_first = xa.shape[0] // tm
    d_f = f2d.shape[1]
    d_h = z2d.shape[1]
    sub = MERGE_SUBTILE
    tri = jnp.asarray(np.tril(np.ones((sub, sub)), -1), BF16)
    mod_spec = pl.BlockSpec((1, 1, D), lambda i: (i // per_b, 0, 0))
    row = lambda w: pl.BlockSpec((tm, w), lambda i: (i, 0))
    return pl.pallas_call(
        functools.partial(_merge_kernel, n_first),
        grid=(T // tm,),
        in_specs=[row(d_f), row(d_h), row(2 * D)] + _group_specs(n_first, tm, D) + [
            mod_spec, mod_spec, mod_spec,
            _const_spec((d_f, D)), _const_spec((d_h, D)), _const_spec((D, D)),
            _const_spec((D, LANES)), _const_spec((1, LANES)), _const_spec((sub, sub))],
        out_specs=[row(D), pl.BlockSpec((tm * SUBLANES, LANES), lambda i: (i, 0)), row(LANES),
                   pl.BlockSpec((SUBLANES, tm), lambda i: (0, i)), pl.BlockSpec((1, LANES), lambda i: (0, 0))],
        out_shape=[jax.ShapeDtypeStruct((T, D), F32), jax.ShapeDtypeStruct((T * SUBLANES, LANES), F32),
                   jax.ShapeDtypeStruct((T, LANES), F32), jax.ShapeDtypeStruct((SUBLANES, T), F32),
                   jax.ShapeDtypeStruct((1, LANES), F32)],
        compiler_params=_params(("arbitrary",)),
        name="merge",
    )(f2d, z2d, sg, xa, xb, gt1, sc2, sh2, w_four, w_hyena, w_out, rw, rb, tri)


def _to_token_tiles(ref, val, row0=0):
    n = val.shape[0]
    for s in range(SUBLANES):
        ref[pl.ds(row0 + s, n, stride=SUBLANES), :] = val[:, s * LANES:(s + 1) * LANES]


def _from_token_tiles(ref, row0, n):
    return jnp.concatenate([ref[pl.ds(row0 + s, n, stride=SUBLANES), :] for s in range(SUBLANES)], axis=1)


def _gather_start(idx_ref, src_hbm, dst, row0, n_tok, sem):
    def issue(r, carry):
        src = pl.multiple_of(idx_ref[0, 0, r] * SUBLANES, SUBLANES)
        row = pl.multiple_of(row0 + r * SUBLANES, SUBLANES)
        pltpu.make_async_copy(src_hbm.at[pl.ds(src, SUBLANES)], dst.at[pl.ds(row, SUBLANES)], sem).start()
        return carry

    lax.fori_loop(0, n_tok, issue, 0, unroll=8)


def _gather_wait(src_hbm, dst, row0, n_tok, sem):
    rows = n_tok * SUBLANES
    pltpu.make_async_copy(src_hbm.at[pl.ds(0, rows)], dst.at[pl.ds(row0, rows)], sem).wait()


def _expert_kernel(blk_e_ref, nused_ref, cur_ref, nxt_ref, h_hbm, wg_ref, wu_ref, wd_ref, y_ref, xbuf, sem):
    i = pl.program_id(0)
    nused = nused_ref[0]
    slot = i % 2
    bm = MOE_BLOCK
    rows = bm * SUBLANES

    @pl.when(i == 0)
    def _():
        _gather_start(cur_ref, h_hbm, xbuf, 0, bm, sem.at[0])

    @pl.when(i + 1 < nused)
    def _():
        _gather_start(nxt_ref, h_hbm, xbuf, (1 - slot) * rows, bm, sem.at[1 - slot])

    @pl.when(i < nused)
    def _():
        _gather_wait(h_hbm, xbuf, slot * rows, bm, sem.at[slot])
        x = _from_token_tiles(xbuf, slot * rows, bm).astype(BF16)
        g = _dot(x, wg_ref[0])
        u = _dot(x, wu_ref[0])
        a = (g * jax.nn.sigmoid(g) * u).astype(BF16)
        _to_token_tiles(y_ref, _dot(a, wd_ref[0]))

    @pl.when(i >= nused)
    def _():
        y_ref[...] = jnp.zeros_like(y_ref)


def _experts(blk_e, nused, src_tok, h2, wg, wu, wd):
    nb = blk_e.shape[0]
    bm = MOE_BLOCK
    D, de = wg.shape[1:]
    assert D == SUBLANES * LANES and h2.shape[1] == LANES
    idx = src_tok.reshape(nb, 1, bm)
    grid_spec = pltpu.PrefetchScalarGridSpec(
        num_scalar_prefetch=2,
        grid=(nb,),
        in_specs=[pl.BlockSpec((1, 1, bm), lambda i, be, nu: (i, 0, 0), memory_space=pltpu.SMEM),
                  pl.BlockSpec((1, 1, bm), lambda i, be, nu: (jnp.minimum(i + 1, nb - 1), 0, 0),
                               memory_space=pltpu.SMEM),
                  pl.BlockSpec(memory_space=pl.ANY),
                  pl.BlockSpec((1, D, de), lambda i, be, nu: (be[i], 0, 0)),
                  pl.BlockSpec((1, D, de), lambda i, be, nu: (be[i], 0, 0)),
                  pl.BlockSpec((1, de, D), lambda i, be, nu: (be[i], 0, 0))],
        out_specs=pl.BlockSpec((bm * SUBLANES, LANES), lambda i, be, nu: (i, 0)),
        scratch_shapes=[pltpu.VMEM((2 * bm * SUBLANES, LANES), F32), pltpu.SemaphoreType.DMA((2,))],
    )
    return pl.pallas_call(
        _expert_kernel,
        grid_spec=grid_spec,
        out_shape=jax.ShapeDtypeStruct((nb * bm * SUBLANES, LANES), F32),
        compiler_params=_params(("arbitrary",)),
        name="experts",
    )(blk_e, nused, idx, idx, h2, wg, wu, wd)


def _combine_kernel(n_steps, cur_ref, nxt_ref, y_hbm, x1_ref, route_ref, gt_ref, gf_ref, o_ref, buf, sem):
    i = pl.program_id(0)
    tm = x1_ref.shape[0]
    slot = i % 2
    rows = 2 * tm * SUBLANES

    @pl.when(i == 0)
    def _():
        _gather_start(cur_ref, y_hbm, buf, 0, 2 * tm, sem.at[0])

    @pl.when(i + 1 < n_steps)
    def _():
        _gather_start(nxt_ref, y_hbm, buf, (1 - slot) * rows, 2 * tm, sem.at[1 - slot])

    _gather_wait(y_hbm, buf, slot * rows, 2 * tm, sem.at[slot])
    route = route_ref[...]
    y1 = _from_token_tiles(buf, slot * rows, tm)
    y2 = _from_token_tiles(buf, slot * rows + tm * SUBLANES, tm)
    x = x1_ref[...] + gt_ref[0] * (route[:, 2:3] * y1 + route[:, 3:4] * y2)
    o_ref[...] = x * lax.rsqrt(jnp.mean(x * x, axis=-1, keepdims=True) + EPS) * gf_ref[...]


def _combine(tile0, n_tok, dest_tiles, yb, x1, route, gt2, g_final, L):
    T, D = x1.shape
    tm = TOKEN_TILE
    per_b = L // tm
    n = n_tok // tm
    last = tile0 + n - 1
    smem = lambda f: pl.BlockSpec((1, 1, 2 * tm), lambda i: (f(i), 0, 0), memory_space=pltpu.SMEM)
    return pl.pallas_call(
        functools.partial(_combine_kernel, n),
        grid=(n,),
        in_specs=[smem(lambda i: tile0 + i), smem(lambda i: jnp.minimum(tile0 + i + 1, last)),
                  pl.BlockSpec(memory_space=pl.ANY),
                  pl.BlockSpec((tm, D), lambda i: (tile0 + i, 0)),
                  pl.BlockSpec((tm, LANES), lambda i: (tile0 + i, 0)),
                  pl.BlockSpec((1, 1, D), lambda i: ((tile0 + i) // per_b, 0, 0)),
                  pl.BlockSpec((1, D), lambda i: (0, 0))],
        out_specs=pl.BlockSpec((tm, D), lambda i: (i, 0)),
        out_shape=jax.ShapeDtypeStruct((n_tok, D), F32),
        scratch_shapes=[pltpu.VMEM((2 * 2 * tm * SUBLANES, LANES), F32), pltpu.SemaphoreType.DMA((2,))],
        compiler_params=_params(("arbitrary",)),
        name="combine",
    )(dest_tiles, dest_tiles, yb, x1, route, gt2, g_final.reshape(1, D))


def _encoder(xp, xs, c, p, g_final):
    n_prompt, L, D = xp.shape
    B = n_prompt + xs.shape[0]
    T = B * L
    xa, xb = xp.reshape(-1, D), xs.reshape(-1, D)
    d_f = N_FOURIER_GROUPS * FOURIER_GROUP_DIM
    d_h = (HYENA_ORDER + 1) * D_HYENA
    d_g = 2 * D

    mod = _ada(c, p["w_ada"], p["b_ada"])
    sh1, sc1, gt1, sh2, sc2, gt2 = [mod[:, k * D:(k + 1) * D].reshape(B, 1, D) for k in range(6)]

    wb = jnp.concatenate([p["w_in"][:, :d_f], jnp.broadcast_to(p["b_in"][None, :d_f], (8, d_f))], axis=0)
    fre, fim = _fold_channel_dft(wb)
    w_all = jnp.concatenate([fre[:D], fim[:D], p["w_in"][:, d_f:]], axis=1).astype(BF16)
    b_all = jnp.concatenate([fre[D], fim[D], p["b_in"][d_f:]])[None, :]
    zr, zi, uh, sg = _inproj(xa, xb, sc1, sh1, w_all, b_all, L, d_f, d_h, d_g)

    f = _fourier(zr.reshape(B, L, d_f), zi.reshape(B, L, d_f))

    ktime, ksum = _filt_time(L, p["filt_w1"], p["filt_b1"], p["filt_w2"], p["filt_b2"],
                             p["filt_w3"], p["filt_b3"], p["filt_freq"], p["filt_wout"])
    kf = _filt_fft(L, ktime, ksum)
    uh3 = uh.reshape(B, L, d_h)
    conv_b = p["conv_b"][None, :]
    ncol = D_HYENA // FFT_COLS
    skip = p["hyena_skip"]
    z = _hyena_order(0, uh3, 0, uh3, ncol, p["conv_w"], conv_b, skip[0:1], kf, L)
    z = _hyena_order(1, z, 0, uh3, 2 * ncol, p["conv_w"], conv_b, skip[1:2], kf, L)

    rw = jnp.zeros((D, LANES), F32).at[:, :N_GROUPS].set(p["router_w1"])
    rw = rw.at[:, N_GROUPS:N_GROUPS + N_EXPERTS].set(p["router_w2"])
    rb = jnp.zeros((1, LANES), F32).at[0, :N_GROUPS].set(p["router_b1"])
    rb = rb.at[0, N_GROUPS:N_GROUPS + N_EXPERTS].set(p["router_b2"])
    x1, h2, route, route_t, counts = _merge(f.reshape(T, d_f), z.reshape(T, D_HYENA), sg, xa, xb,
                                            gt1, sc2, sh2, p["w_four"].astype(BF16), p["w_hyena"].astype(BF16),
                                            p["w_out"].astype(BF16), rw, rb, L)

    bm = MOE_BLOCK
    tm = TOKEN_TILE
    nb = (T * TOP_K) // bm + N_EXPERTS
    e = route_t[0:2].astype(jnp.int32)
    rank = route_t[4:6].astype(jnp.int32)
    cnt = counts[0, :N_EXPERTS].astype(jnp.int32)
    pcnt = (cnt + bm - 1) // bm * bm
    pend = jnp.cumsum(pcnt)
    experts = jnp.arange(N_EXPERTS, dtype=jnp.int32)
    dest = rank + jnp.sum(jnp.where(e[..., None] == experts, pend - pcnt, 0), axis=-1)
    tok = jnp.broadcast_to(jnp.arange(T, dtype=jnp.int32)[None, :], (TOP_K, T))
    src_tok = jnp.zeros((nb * bm,), jnp.int32).at[dest.reshape(-1)].set(tok.reshape(-1))
    blk_row0 = jnp.arange(nb, dtype=jnp.int32)[:, None] * bm
    blk_e = jnp.minimum(jnp.sum((pend[None, :] <= blk_row0).astype(jnp.int32), axis=1), N_EXPERTS - 1)
    nused = (pend[-1] // bm).astype(jnp.int32).reshape(1)
    yb = _experts(blk_e, nused, src_tok, h2, p["exp_w_gate"].astype(BF16), p["exp_w_up"].astype(BF16),
                  p["exp_w_down"].astype(BF16))

    dest_tiles = dest.reshape(TOP_K, T // tm, tm).transpose(1, 0, 2).reshape(T // tm, 1, TOP_K * tm)
    t_prompt = n_prompt * L
    outs = []
    for tile0, n_tok in ((0, t_prompt), (t_prompt // tm, T - t_prompt)):
        outs.append(_combine(tile0, n_tok, dest_tiles, yb, x1, route, gt2, g_final, L))
    return outs[0].reshape(xp.shape), outs[1].reshape(xs.shape)


def kernel(x_prompt, x_sample, c_prompt, c_sample, w_ada, b_ada, w_in, b_in, conv_w, conv_b, filt_w1, filt_b1, filt_w2, filt_b2, filt_w3, filt_b3, filt_freq, filt_wout, hyena_skip, w_four, w_hyena, w_out, router_w1, router_b1, router_w2, router_b2, exp_w_gate, exp_w_up, exp_w_down, g_final):
    assert w_ada.shape[0] == 1, "single-layer block"
    assert x_prompt.shape[1:] == x_sample.shape[1:], "both request groups share sequence length and width"
    p = dict(w_ada=w_ada[0], b_ada=b_ada[0], w_in=w_in[0], b_in=b_in[0], conv_w=conv_w[0], conv_b=conv_b[0],
             filt_w1=filt_w1[0], filt_b1=filt_b1[0], filt_w2=filt_w2[0], filt_b2=filt_b2[0],
             filt_w3=filt_w3[0], filt_b3=filt_b3[0], filt_freq=filt_freq[0], filt_wout=filt_wout[0],
             hyena_skip=hyena_skip[0], w_four=w_four[0], w_hyena=w_hyena[0], w_out=w_out[0],
             router_w1=router_w1[0], router_b1=router_b1[0], router_w2=router_w2[0], router_b2=router_b2[0],
             exp_w_gate=exp_w_gate[0], exp_w_up=exp_w_up[0], exp_w_down=exp_w_down[0])
    c = jnp.concatenate([c_prompt, c_sample], axis=0)
    return _encoder(x_prompt, x_sample, c, p, g_final)
```

```python
import functools
import math

import numpy as np
import jax
import jax.numpy as jnp
from jax import lax
from jax.experimental import pallas as pl
from jax.experimental.pallas import tpu as pltpu

F32 = jnp.float32
BF16 = jnp.bfloat16
HI = lax.Precision.HIGHEST

EPS = 1e-6
FFT_MINOR = 64
N_FOURIER_GROUPS = 4
FOURIER_GROUP_DIM = 128
D_HYENA = 512
HYENA_ORDER = 2
POS_BANDS = 16
N_GROUPS = 4
EXPERTS_PER_GROUP = 8
N_EXPERTS = 32
TOP_K = 2
SHORT_DECAY_PCT = 0.3
LONG_DECAY_PCT = 1.5
DECAY_TARGET = 1e-2
LANES = 128
SUBLANES = 8
VMEM_LIMIT = 56 * 1024 * 1024

TOKEN_TILE = 512
MERGE_SUBTILE = 128
MOE_BLOCK = 256
FFT_COLS = LANES
ROW_CHUNK = 256
FFT_UNROLL = 16


def _dot(a, b):
    return jnp.dot(a, b, preferred_element_type=F32)


def _dot_hi(a, b):
    return jnp.dot(a, b, preferred_element_type=F32, precision=HI)


def _params(sem=None):
    return pltpu.CompilerParams(dimension_semantics=sem, vmem_limit_bytes=VMEM_LIMIT)


def _const_spec(shape):
    nd = len(shape)
    return pl.BlockSpec(shape, lambda *_: (0,) * nd, pipeline_mode=pl.Buffered(1))


def _pitch(rows):
    p = -(-rows // SUBLANES)
    return SUBLANES * (p if p % 2 else p + 1)


@functools.lru_cache(maxsize=None)
def _stage2_tables():
    i = np.arange(FFT_MINOR)
    ph = 2.0 * np.pi * np.outer(i, i) / FFT_MINOR
    c, s = np.cos(ph), np.sin(ph)
    fwd = np.block([[c, s], [-s, c]])
    inv = np.block([[c, -s], [s, c]])
    return fwd, inv, c, s


@functools.lru_cache(maxsize=None)
def _hyena_tables(L):
    n_fft = 2 * L
    m = FFT_MINOR
    na = n_fft // m
    nh = na // 2
    b = np.arange(m)[:, None, None]
    ka = np.arange(nh + 1)[None, :, None]

    def forward(a_count):
        a = np.arange(a_count)[None, None, :]
        th = 2.0 * np.pi * ((ka * (m * a + b)) % n_fft) / n_fft
        c, s = np.cos(th), -np.sin(th)
        pad = np.zeros((m, SUBLANES - 2, a_count))
        return np.concatenate([c[:, :nh], s[:, :nh], c[:, nh:], s[:, nh:], pad], axis=1)

    fwd_half = forward(nh)
    fwd_full = forward(na)
    weight = np.where(np.arange(nh) == 0, 1.0, 2.0)[None, :, None]
    inv_half = np.transpose(fwd_half[:, :2 * nh] * np.concatenate([weight, weight], axis=1), (0, 2, 1)) / n_fft
    sign = np.broadcast_to(((-1.0) ** np.arange(nh))[:, None] / n_fft, (nh, LANES))
    ph = np.pi * np.arange(m) / m
    cos_b = np.broadcast_to(np.cos(ph)[:, None], (m, LANES))
    sin_b = np.broadcast_to(np.sin(ph)[:, None], (m, LANES))
    return fwd_half, inv_half, fwd_full, sign, cos_b, sin_b


@functools.lru_cache(maxsize=None)
def _fourier_tables(L):
    na = L // FFT_MINOR
    b = np.arange(FFT_MINOR)[:, None, None]
    ka = np.arange(na)[None, :, None]
    a = np.arange(na)[None, None, :]
    th = 2.0 * np.pi * ((ka * (FFT_MINOR * a + b)) % L) / L
    c, s = np.cos(th), np.sin(th)
    st1 = np.concatenate([np.concatenate([c, s], axis=2),
                          np.concatenate([-s, c], axis=2)], axis=1)
    _, _, c2, s2 = _stage2_tables()
    st2 = np.concatenate([c2, s2], axis=1) / math.sqrt(L)
    return st1, st2


def _mxu_table(table):
    return jnp.asarray(table, F32).astype(BF16)


@functools.lru_cache(maxsize=None)
def _channel_dft():
    i = np.arange(FOURIER_GROUP_DIM)
    ph = 2.0 * np.pi * np.outer(i, i) / FOURIER_GROUP_DIM
    return np.concatenate([np.cos(ph), -np.sin(ph)], axis=1) / math.sqrt(FOURIER_GROUP_DIM)


def _ada_kernel(c_ref, w_ref, b_ref, o_ref):
    c = c_ref[...]
    o_ref[...] = _dot_hi(c * jax.nn.sigmoid(c), w_ref[...]) + b_ref[...]


def _ada(c, w_ada, b_ada):
    nb, d = c.shape
    n = w_ada.shape[1]
    tn = 1536
    return pl.pallas_call(
        _ada_kernel,
        grid=(n // tn,),
        in_specs=[pl.BlockSpec((nb, d), lambda j: (0, 0)),
                  pl.BlockSpec((d, tn), lambda j: (0, j)),
                  pl.BlockSpec((1, tn), lambda j: (0, j))],
        out_specs=pl.BlockSpec((nb, tn), lambda j: (0, j)),
        out_shape=jax.ShapeDtypeStruct((nb, n), F32),
        compiler_params=_params(("arbitrary",)),
        name="ada",
    )(c, w_ada, b_ada.reshape(1, n))


def _fold_kernel(w_ref, f_ref, re_ref, im_ref):
    z = _dot_hi(w_ref[...], f_ref[...])
    re_ref[...] = z[:, :FOURIER_GROUP_DIM]
    im_ref[...] = z[:, FOURIER_GROUP_DIM:]


def _fold_channel_dft(wb):
    rows = wb.shape[0]
    gd = FOURIER_GROUP_DIM
    fmat = jnp.asarray(_channel_dft(), F32)
    return pl.pallas_call(
        _fold_kernel,
        grid=(N_FOURIER_GROUPS,),
        in_specs=[pl.BlockSpec((rows, gd), lambda g: (0, g)),
                  pl.BlockSpec((gd, 2 * gd), lambda g: (0, 0))],
        out_specs=[pl.BlockSpec((rows, gd), lambda g: (0, g)),
                   pl.BlockSpec((rows, gd), lambda g: (0, g))],
        out_shape=[jax.ShapeDtypeStruct((rows, N_FOURIER_GROUPS * gd), F32)] * 2,
        compiler_params=_params(("arbitrary",)),
        name="fold",
    )(wb, fmat)


def _for_row_chunks(n_rows, body):
    def step(j, carry):
        body(j, pl.multiple_of(j * ROW_CHUNK, ROW_CHUNK))
        return carry

    lax.fori_loop(0, n_rows // ROW_CHUNK, step, 0)


def _store_padded(dst, j, val, run=FFT_MINOR):
    pz = _pitch(run)
    per = ROW_CHUNK // run
    for i in range(per):
        dst[pl.ds(pl.multiple_of((j * per + i) * pz, SUBLANES), run), :] = val[i * run:(i + 1) * run]


def _load_padded(src, j, run=FFT_MINOR):
    pz = _pitch(run)
    per = ROW_CHUNK // run
    return jnp.concatenate([src[pl.ds(pl.multiple_of((j * per + i) * pz, SUBLANES), run), :] for i in range(per)],
                           axis=0)


def _short_conv_chunk(src_ref, w_ref, b_ref, L, j, r0):
    R = ROW_CHUNK
    ct = src_ref.shape[-1]
    row = lax.broadcasted_iota(jnp.int32, (R, ct), 0)
    cur = src_ref[0, pl.ds(r0, R), :]
    before = src_ref[0, pl.ds(jnp.maximum(r0 - 1, 0), 1), :] * jnp.where(j > 0, 1.0, 0.0)
    after = src_ref[0, pl.ds(jnp.minimum(r0 + R, L - 1), 1), :] * jnp.where(j < L // R - 1, 1.0, 0.0)
    up = jnp.where(row == 0, before, pltpu.roll(cur, 1, 0))
    dn = jnp.where(row == R - 1, after, pltpu.roll(cur, R - 1, 0))
    return w_ref[0:1, :] * up + w_ref[1:2, :] * cur + w_ref[2:3, :] * dn + b_ref[...]


def _filt_time_kernel(L, rows, bands_ref, delt_ref, w1_ref, b1_ref, w2_ref, b2_ref, w3_ref, b3_ref,
                      fr_ref, wo_ref, k_ref, sum_ref):
    i = pl.program_id(0)
    n = i * rows + lax.broadcasted_iota(jnp.int32, (rows, 1), 0)
    fwd = n < L
    pos = jnp.where(fwd, n, 2 * L - n).astype(F32)
    t = pos * (1.0 / (L - 1))
    ang = (2.0 * math.pi / L) * pos * bands_ref[...]
    w1 = w1_ref[...]
    pre = (t * w1[0:1, :] + _dot_hi(jnp.cos(ang), w1[1:1 + POS_BANDS, :])
           + _dot_hi(-jnp.sin(ang), w1[1 + POS_BANDS:, :]) + b1_ref[...])
    fr = fr_ref[...]
    h = jnp.sin(fr[0:1, :] * pre)
    h = jnp.sin(fr[1:2, :] * (_dot_hi(h, w2_ref[...]) + b2_ref[...]))
    h = jnp.sin(fr[2:3, :] * (_dot_hi(h, w3_ref[...]) + b3_ref[...]))
    window = jnp.exp(-t * delt_ref[...])
    live = n != L
    parts = []
    for o in range(HYENA_ORDER):
        base = o * 2 * D_HYENA
        hf = _dot_hi(h, wo_ref[:, base:base + D_HYENA])
        hb = _dot_hi(h, wo_ref[:, base + D_HYENA:base + 2 * D_HYENA])
        parts.append(jnp.where(live, jnp.where(fwd, hf, hb) * window, 0.0))
    k = jnp.concatenate(parts, axis=1)
    k_ref[...] = k

    @pl.when(i == 0)
    def _():
        sum_ref[...] = jnp.zeros_like(sum_ref)

    sum_ref[...] += jnp.sum(jnp.abs(k), axis=0, keepdims=True)


def _filt_time(L, w1, b1, w2, b2, w3, b3, freq, wout):
    rows = 512
    n_fft = 2 * L
    cols = HYENA_ORDER * D_HYENA
    bands = jnp.linspace(1e-4, POS_BANDS - 1, POS_BANDS, dtype=F32).reshape(1, POS_BANDS)
    max_decay = math.log(DECAY_TARGET) / SHORT_DECAY_PCT
    min_decay = math.log(DECAY_TARGET) / LONG_DECAY_PCT
    deltas = jnp.abs(jnp.linspace(min_decay, max_decay, D_HYENA, dtype=F32)).reshape(1, D_HYENA)
    args = (bands, deltas, w1, b1.reshape(1, -1), w2, b2.reshape(1, -1), w3, b3.reshape(1, -1), freq, wout)
    return pl.pallas_call(
        functools.partial(_filt_time_kernel, L, rows),
        grid=(n_fft // rows,),
        in_specs=[pl.BlockSpec(a.shape, lambda i: (0, 0)) for a in args],
        out_specs=[pl.BlockSpec((rows, cols), lambda i: (i, 0)),
                   pl.BlockSpec((1, cols), lambda i: (0, 0))],
        out_shape=[jax.ShapeDtypeStruct((n_fft, cols), F32),
                   jax.ShapeDtypeStruct((1, cols), F32)],
        compiler_params=_params(("arbitrary",)),
        name="filt_time",
    )(*args)


def _filt_fft_kernel(na, k_ref, sum_ref, st1_ref, st2_ref, o_ref, kp, s1):
    m = FFT_MINOR
    nh = na // 2
    rows1 = 2 * nh + SUBLANES
    pz, p1 = _pitch(m), _pitch(rows1)
    _for_row_chunks(na * m, lambda j, r0: _store_padded(kp, j, k_ref[pl.ds(r0, ROW_CHUNK), :]))

    def stage1(b, carry):
        slab = kp[pl.ds(b, na, stride=pz), :]
        s1[pl.ds(pl.multiple_of(b * p1, SUBLANES), rows1), :] = _dot_hi(st1_ref[b], slab)
        return carry

    lax.fori_loop(0, m, stage1, 0, unroll=4)
    inv_norm = 1.0 / sum_ref[...]

    def spectrum_row(re_row, im_row):
        a = jnp.concatenate([s1[pl.ds(re_row, m, stride=p1), :], s1[pl.ds(im_row, m, stride=p1), :]], axis=0)
        return _dot_hi(st2_ref[...], a) * inv_norm

    def stage2(ka, carry):
        o_ref[ka] = spectrum_row(ka, nh + ka)
        return carry

    lax.fori_loop(0, nh, stage2, 0, unroll=4)
    o_ref[nh] = spectrum_row(2 * nh, 2 * nh + 1)


def _filt_fft(L, ktime, ksum):
    n_fft, cols = ktime.shape
    m = FFT_MINOR
    na = n_fft // m
    nh = na // 2
    rows1 = 2 * nh + SUBLANES
    ct = FFT_COLS
    fwd_full = _hyena_tables(L)[2]
    st2f = _stage2_tables()[0]
    return pl.pallas_call(
        functools.partial(_filt_fft_kernel, na),
        grid=(cols // ct,),
        in_specs=[pl.BlockSpec((n_fft, ct), lambda j: (0, j)),
                  pl.BlockSpec((1, ct), lambda j: (0, j)),
                  _const_spec((m, rows1, na)),
                  _const_spec((2 * m, 2 * m))],
        out_specs=pl.BlockSpec((nh + 1, 2 * m, ct), lambda j: (0, 0, j)),
        out_shape=jax.ShapeDtypeStruct((nh + 1, 2 * m, cols), F32),
        scratch_shapes=[pltpu.VMEM((na * _pitch(m), ct), F32),
                        pltpu.VMEM((m * _pitch(rows1), ct), F32)],
        compiler_params=_params(("arbitrary",)),
        name="filt_fft",
    )(ktime, ksum, jnp.asarray(fwd_full, F32), jnp.asarray(st2f, F32))


def _rms_mod(x, scale, shift):
    y = x * lax.rsqrt(jnp.mean(x * x, axis=-1, keepdims=True) + EPS)
    return y * (1.0 + scale) + shift


def _group_tile(n_first, a_ref, b_ref):
    return jnp.where(pl.program_id(0) < n_first, a_ref[...], b_ref[...])


def _group_specs(n_first, tm, D):
    return [pl.BlockSpec((tm, D), lambda i: (jnp.minimum(i, n_first - 1), 0)),
            pl.BlockSpec((tm, D), lambda i: (jnp.maximum(i - n_first, 0), 0))]


def _inproj_kernel(d_f, d_h, n_first, xa_ref, xb_ref, sc_ref, sh_ref, w_ref, b_ref, zr_ref, zi_ref, uh_ref, sg_ref):
    h = _rms_mod(_group_tile(n_first, xa_ref, xb_ref), sc_ref[0], sh_ref[0]).astype(BF16)

    def proj(c0, width):
        return _dot(h, w_ref[:, c0:c0 + width]) + b_ref[:, c0:c0 + width]

    zr_ref[...] = proj(0, d_f)
    zi_ref[...] = proj(d_f, d_f)
    chunk = 512
    for c in range(0, d_h, chunk):
        uh_ref[:, c:c + chunk] = proj(2 * d_f + c, chunk)
    d_g = sg_ref.shape[1]
    for c in range(0, d_g, chunk):
        sg_ref[:, c:c + chunk] = jax.nn.sigmoid(proj(2 * d_f + d_h + c, chunk)).astype(BF16)


def _inproj(xa, xb, sc, sh, w, b, L, d_f, d_h, d_g):
    D = xa.shape[1]
    T = xa.shape[0] + xb.shape[0]
    tm = TOKEN_TILE
    per_b = L // tm
    n_first = xa.shape[0] // tm
    n = w.shape[1]
    mod_spec = pl.BlockSpec((1, 1, D), lambda i: (i // per_b, 0, 0))
    return pl.pallas_call(
        functools.partial(_inproj_kernel, d_f, d_h, n_first),
        grid=(T // tm,),
        in_specs=_group_specs(n_first, tm, D) + [mod_spec, mod_spec, _const_spec((D, n)), _const_spec((1, n))],
        out_specs=[pl.BlockSpec((tm, d_f), lambda i: (i, 0)),
                   pl.BlockSpec((tm, d_f), lambda i: (i, 0)),
                   pl.BlockSpec((tm, d_h), lambda i: (i, 0)),
                   pl.BlockSpec((tm, d_g), lambda i: (i, 0))],
        out_shape=[jax.ShapeDtypeStruct((T, d_f), F32), jax.ShapeDtypeStruct((T, d_f), F32),
                   jax.ShapeDtypeStruct((T, d_h), F32), jax.ShapeDtypeStruct((T, d_g), BF16)],
        compiler_params=_params(("parallel",)),
        name="inproj",
    )(xa, xb, sc, sh, w, b)


def _fourier_kernel(na, zr_ref, zi_ref, st1_ref, st2_ref, o_ref, zpr, zpi, s1, op):
    m = FFT_MINOR
    pz, p1 = _pitch(m), _pitch(2 * na)

    def fill(j, r0):
        _store_padded(zpr, j, zr_ref[0, pl.ds(r0, ROW_CHUNK), :])
        _store_padded(zpi, j, zi_ref[0, pl.ds(r0, ROW_CHUNK), :])

    _for_row_chunks(na * m, fill)

    def stage1(b, carry):
        z = jnp.concatenate([zpr[pl.ds(b, na, stride=pz), :], zpi[pl.ds(b, na, stride=pz), :]], axis=0)
        s1[pl.ds(pl.multiple_of(b * p1, SUBLANES), 2 * na), :] = _dot(st1_ref[b], z.astype(BF16))
        return carry

    lax.fori_loop(0, m, stage1, 0, unroll=FFT_UNROLL)

    po = _pitch(na)

    def stage2(ka, carry):
        a = jnp.concatenate([s1[pl.ds(ka, m, stride=p1), :], s1[pl.ds(na + ka, m, stride=p1), :]], axis=0)
        op[pl.ds(ka, m, stride=po), :] = _dot(st2_ref[...], a.astype(BF16))
        return carry

    lax.fori_loop(0, na, stage2, 0, unroll=FFT_UNROLL)

    def unpad(j, r0):
        o_ref[0, pl.ds(r0, ROW_CHUNK), :] = _load_padded(op, j, na)

    _for_row_chunks(na * m, unpad)


def _fourier(zr, zi):
    B, L, C = zr.shape
    m = FFT_MINOR
    na = L // m
    ct = FFT_COLS
    st1, st2 = _fourier_tables(L)
    spec = pl.BlockSpec((1, L, ct), lambda j, b: (b, 0, j))
    pad = pltpu.VMEM((na * _pitch(m), ct), F32)
    return pl.pallas_call(
        functools.partial(_fourier_kernel, na),
        grid=(C // ct, B),
        in_specs=[spec, spec, _const_spec((m, 2 * na, 2 * na)), _const_spec((m, 2 * m))],
        out_specs=spec,
        out_shape=jax.ShapeDtypeStruct((B, L, C), F32),
        scratch_shapes=[pad, pad, pltpu.VMEM((m * _pitch(2 * na), ct), F32),
                        pltpu.VMEM((m * _pitch(na), ct), F32)],
        compiler_params=_params(("parallel", "parallel")),
        name="fourier",
    )(zr, zi, _mxu_table(st1), _mxu_table(st2))


def _hyena_kernel(conv_z, L, zin_ref, gin_ref, cwz_ref, cbz_ref, cwg_ref, cbg_ref, skip_ref, kf_ref,
                  fwd1_ref, inv1_ref, st2f_ref, st2i_ref, sign_ref, cosb_ref, sinb_ref, out_ref, zp, yp, s1, g2, hh):
    m = FFT_MINOR
    nh = L // m
    rows1 = 2 * nh + SUBLANES
    pz, p1, p2 = _pitch(m), _pitch(rows1), _pitch(2 * m)

    def fill(j, r0):
        if conv_z:
            _store_padded(zp, j, _short_conv_chunk(zin_ref, cwz_ref, cbz_ref, L, j, r0))
        else:
            _store_padded(zp, j, zin_ref[0, pl.ds(r0, ROW_CHUNK), :])

    _for_row_chunks(L, fill)

    def stage1(b, carry):
        slab = zp[pl.ds(b, nh, stride=pz), :].astype(BF16)
        s1[pl.ds(pl.multiple_of(b * p1, SUBLANES), rows1), :] = _dot(fwd1_ref[b], slab)
        return carry

    lax.fori_loop(0, m, stage1, 0, unroll=FFT_UNROLL)

    def spectrum_row(ka, re_row, im_row):
        a = jnp.concatenate([s1[pl.ds(re_row, m, stride=p1), :], s1[pl.ds(im_row, m, stride=p1), :]], axis=0)
        x = _dot(st2f_ref[...], a.astype(BF16))
        xr, xi = x[:m], x[m:]
        k = kf_ref[ka]
        kr, ki = k[:m], k[m:]
        y = jnp.concatenate([xr * kr - xi * ki, xr * ki + xi * kr], axis=0).astype(BF16)
        return _dot(st2i_ref[...], y)

    def stage2(ka, carry):
        g2[pl.ds(pl.multiple_of(ka * p2, SUBLANES), 2 * m), :] = spectrum_row(ka, ka, nh + ka)
        return carry

    lax.fori_loop(0, nh, stage2, 0, unroll=FFT_UNROLL)
    g_mid = spectrum_row(nh, 2 * nh, 2 * nh + 1)
    hh[...] = cosb_ref[...] * g_mid[:m] - sinb_ref[...] * g_mid[m:]
    sign = sign_ref[...]

    def stage3(b, carry):
        g = jnp.concatenate([g2[pl.ds(b, nh, stride=p2), :], g2[pl.ds(m + b, nh, stride=p2), :]], axis=0)
        yp[pl.ds(b, nh, stride=pz), :] = _dot(inv1_ref[b], g.astype(BF16)) + sign * hh[pl.ds(b, 1), :]
        return carry

    lax.fori_loop(0, m, stage3, 0, unroll=FFT_UNROLL)
    skip = skip_ref[...]

    def finish(j, r0):
        gate = _short_conv_chunk(gin_ref, cwg_ref, cbg_ref, L, j, r0)
        out_ref[0, pl.ds(r0, ROW_CHUNK), :] = gate * (_load_padded(yp, j) + skip * _load_padded(zp, j))

    _for_row_chunks(L, finish)


def _hyena_order(order, zin, z_col0, uh, g_col0, conv_w, conv_b, skip, kf, L):
    B = uh.shape[0]
    m = FFT_MINOR
    ct = FFT_COLS
    ncol = D_HYENA // ct
    nh = L // m
    rows1 = 2 * nh + SUBLANES
    conv_z = order == 0
    fwd_half, inv_half, _, sign, cos_b, sin_b = _hyena_tables(L)
    st2f, st2i, _, _ = _stage2_tables()
    z_cols = (lambda j, b: (b, 0, z_col0 + j))
    g_cols = (lambda j, b: (b, 0, g_col0 + j))
    zw_col0 = z_col0 if conv_z else 0
    pad = pltpu.VMEM((nh * _pitch(m), ct), F32)
    return pl.pallas_call(
        functools.partial(_hyena_kernel, conv_z, L),
        grid=(ncol, B),
        in_specs=[pl.BlockSpec((1, L, ct), z_cols),
                  pl.BlockSpec((1, L, ct), g_cols),
                  pl.BlockSpec((3, ct), lambda j, b: (0, zw_col0 + j)),
                  pl.BlockSpec((1, ct), lambda j, b: (0, zw_col0 + j)),
                  pl.BlockSpec((3, ct), lambda j, b: (0, g_col0 + j)),
                  pl.BlockSpec((1, ct), lambda j, b: (0, g_col0 + j)),
                  pl.BlockSpec((1, ct), lambda j, b: (0, j)),
                  pl.BlockSpec((nh + 1, 2 * m, ct), lambda j, b: (0, 0, order * ncol + j),
                               pipeline_mode=pl.Buffered(1)),
                  _const_spec((m, rows1, nh)),
                  _const_spec((m, nh, 2 * nh)),
                  _const_spec((2 * m, 2 * m)),
                  _const_spec((2 * m, 2 * m)),
                  _const_spec((nh, LANES)), _const_spec((m, LANES)), _const_spec((m, LANES))],
        out_specs=pl.BlockSpec((1, L, ct), lambda j, b: (b, 0, j)),
        out_shape=jax.ShapeDtypeStruct((B, L, D_HYENA), F32),
        scratch_shapes=[pad, pad,
                        pltpu.VMEM((m * _pitch(rows1), ct), F32),
                        pltpu.VMEM((nh * _pitch(2 * m), ct), F32),
                        pltpu.VMEM((m, ct), F32)],
        compiler_params=_params(("parallel", "arbitrary")),
        name=f"hyena{order}",
    )(zin, uh, conv_w, conv_b, conv_w, conv_b, skip, kf,
      _mxu_table(fwd_half), _mxu_table(inv_half), _mxu_table(st2f), _mxu_table(st2i),
      jnp.asarray(sign, F32), jnp.asarray(cos_b, F32), jnp.asarray(sin_b, F32))


def _merge_kernel(n_first, f_ref, z_ref, sg_ref, xa_ref, xb_ref, gt_ref, sc_ref, sh_ref, wf_ref, wh_ref, wo_ref,
                  rw_ref, rb_ref, tri_ref, x1_ref, h2_ref, route_ref, rt_ref, cnt_ref):
    i = pl.program_id(0)
    tm, D = xa_ref.shape
    sub = MERGE_SUBTILE
    lane = lax.broadcasted_iota(jnp.int32, (sub, LANES), 1).astype(F32)
    neg = -1e30

    def first_max(v):
        mx = jnp.max(v, axis=1, keepdims=True)
        return mx, jnp.min(jnp.where(v == mx, lane, float(LANES)), axis=1, keepdims=True)

    @pl.when(i == 0)
    def _():
        cnt_ref[...] = jnp.zeros_like(cnt_ref)

    def sub_tile(r0, count):
        rows = pl.ds(r0, sub)
        ya = _dot(f_ref[rows, :].astype(BF16), wf_ref[...])
        yb = _dot(z_ref[rows, :].astype(BF16), wh_ref[...])
        merged = sg_ref[rows, :D].astype(F32) * ya + sg_ref[rows, D:].astype(F32) * yb
        x = jnp.where(i < n_first, xa_ref[rows, :], xb_ref[rows, :])
        x1 = x + gt_ref[0] * _dot(merged.astype(BF16), wo_ref[...])
        x1_ref[rows, :] = x1
        h2 = _rms_mod(x1, sc_ref[0], sh_ref[0])
        _to_token_tiles(h2_ref, h2, r0 * SUBLANES)

        logits = _dot_hi(h2, rw_ref[...]) + rb_ref[...]
        gl = jnp.where(lane < N_GROUPS, logits, neg)
        gmax, g = first_max(gl)
        p_g = 1.0 / jnp.sum(jnp.exp(gl - gmax), axis=1, keepdims=True)
        lo = N_GROUPS + EXPERTS_PER_GROUP * g
        el = jnp.where((lane >= lo) & (lane < lo + EXPERTS_PER_GROUP), logits, neg)
        m1, i1 = first_max(el)
        m2, i2 = first_max(jnp.where(lane == i1, neg, el))
        r = jnp.exp(m2 - m1)
        wt1 = p_g / (1.0 + r)
        wt2 = p_g * r / (1.0 + r)
        e1 = i1 - N_GROUPS
        e2 = i2 - N_GROUPS

        onehot = ((lane == e1) | (lane == e2)).astype(BF16)
        before = _dot(tri_ref[...], onehot) + count
        r1 = jnp.sum(jnp.where(lane == e1, before, 0.0), axis=1, keepdims=True)
        r2 = jnp.sum(jnp.where(lane == e2, before, 0.0), axis=1, keepdims=True)

        packed = jnp.zeros((sub, LANES), F32)
        for slot, v in enumerate((e1, e2, wt1, wt2, r1, r2)):
            packed = jnp.where(lane == slot, v, packed)
        route_ref[rows, :] = packed
        rt_ref[:, rows] = packed.T[0:SUBLANES, :]
        return count + jnp.sum(onehot.astype(F32), axis=0, keepdims=True)

    count = cnt_ref[...]
    for h in range(tm // sub):
        count = sub_tile(h * sub, count)
    cnt_ref[...] = count


def _merge(f2d, z2d, sg, xa, xb, gt1, sc2, sh2, w_four, w_hyena, w_out, rw, rb, L):
    D = xa.shape[1]
    T = xa.shape[0] + xb.shape[0]
    assert D == SUBLANES * LANES
    tm = TOKEN_TILE
    per_b = L // tm
    n_first = xa.shape[0] // tm
    d_f = f2d.shape[1]
    d_h = z2d.shape[1]
    sub = MERGE_SUBTILE
    tri = jnp.asarray(np.tril(np.ones((sub, sub)), -1), BF16)
    mod_spec = pl.BlockSpec((1, 1, D), lambda i: (i // per_b, 0, 0))
    row = lambda w: pl.BlockSpec((tm, w), lambda i: (i, 0))
    return pl.pallas_call(
        functools.partial(_merge_kernel, n_first),
        grid=(T // tm,),
        in_specs=[row(d_f), row(d_h), row(2 * D)] + _group_specs(n_first, tm, D) + [
            mod_spec, mod_spec, mod_spec,
            _const_spec((d_f, D)), _const_spec((d_h, D)), _const_spec((D, D)),
            _const_spec((D, LANES)), _const_spec((1, LANES)), _const_spec((sub, sub))],
        out_specs=[row(D), pl.BlockSpec((tm * SUBLANES, LANES), lambda i: (i, 0)), row(LANES),
                   pl.BlockSpec((SUBLANES, tm), lambda i: (0, i)), pl.BlockSpec((1, LANES), lambda i: (0, 0))],
        out_shape=[jax.ShapeDtypeStruct((T, D), F32), jax.ShapeDtypeStruct((T * SUBLANES, LANES), F32),
                   jax.ShapeDtypeStruct((T, LANES), F32), jax.ShapeDtypeStruct((SUBLANES, T), F32),
                   jax.ShapeDtypeStruct((1, LANES), F32)],
        compiler_params=_params(("arbitrary",)),
        name="merge",
    )(f2d, z2d, sg, xa, xb, gt1, sc2, sh2, w_four, w_hyena, w_out, rw, rb, tri)


def _to_token_tiles(ref, val, row0=0):
    n = val.shape[0]
    for s in range(SUBLANES):
        ref[pl.ds(row0 + s, n, stride=SUBLANES), :] = val[:, s * LANES:(s + 1) * LANES]


def _from_token_tiles(ref, row0, n):
    return jnp.concatenate([ref[pl.ds(row0 + s, n, stride=SUBLANES), :] for s in range(SUBLANES)], axis=1)


def _gather_start(idx_ref, src_hbm, dst, row0, n_tok, sem):
    def issue(r, carry):
        src = pl.multiple_of(idx_ref[0, 0, r] * SUBLANES, SUBLANES)
        row = pl.multiple_of(row0 + r * SUBLANES, SUBLANES)
        pltpu.make_async_copy(src_hbm.at[pl.ds(src, SUBLANES)], dst.at[pl.ds(row, SUBLANES)], sem).start()
        return carry

    lax.fori_loop(0, n_tok, issue, 0, unroll=8)


def _gather_wait(src_hbm, dst, row0, n_tok, sem):
    rows = n_tok * SUBLANES
    pltpu.make_async_copy(src_hbm.at[pl.ds(0, rows)], dst.at[pl.ds(row0, rows)], sem).wait()


def _expert_kernel(blk_e_ref, nused_ref, cur_ref, nxt_ref, h_hbm, wg_ref, wu_ref, wd_ref, y_ref,
                   xbuf, wg16, wu16, wd16, sem):
    i = pl.program_id(0)
    nused = nused_ref[0]
    slot = i % 2
    bm = MOE_BLOCK
    rows = bm * SUBLANES

    @pl.when((i == 0) | (blk_e_ref[i] != blk_e_ref[jnp.maximum(i - 1, 0)]))
    def _():
        wg16[...] = wg_ref[0].astype(BF16)
        wu16[...] = wu_ref[0].astype(BF16)
        wd16[...] = wd_ref[0].astype(BF16)

    @pl.when(i == 0)
    def _():
        _gather_start(cur_ref, h_hbm, xbuf, 0, bm, sem.at[0])

    @pl.when(i + 1 < nused)
    def _():
        _gather_start(nxt_ref, h_hbm, xbuf, (1 - slot) * rows, bm, sem.at[1 - slot])

    @pl.when(i < nused)
    def _():
        _gather_wait(h_hbm, xbuf, slot * rows, bm, sem.at[slot])
        x = _from_token_tiles(xbuf, slot * rows, bm).astype(BF16)
        g = _dot(x, wg16[...])
        u = _dot(x, wu16[...])
        a = (g * jax.nn.sigmoid(g) * u).astype(BF16)
        _to_token_tiles(y_ref, _dot(a, wd16[...]))

    @pl.when(i >= nused)
    def _():
        y_ref[...] = jnp.zeros_like(y_ref)


def _experts(blk_e, nused, src_tok, h2, wg, wu, wd):
    nb = blk_e.shape[0]
    bm = MOE_BLOCK
    D, de = wg.shape[1:]
    assert D == SUBLANES * LANES and h2.shape[1] == LANES
    idx = src_tok.reshape(nb, 1, bm)
    grid_spec = pltpu.PrefetchScalarGridSpec(
        num_scalar_prefetch=2,
        grid=(nb,),
        in_specs=[pl.BlockSpec((1, 1, bm), lambda i, be, nu: (i, 0, 0), memory_space=pltpu.SMEM),
                  pl.BlockSpec((1, 1, bm), lambda i, be, nu: (jnp.minimum(i + 1, nb - 1), 0, 0),
                               memory_space=pltpu.SMEM),
                  pl.BlockSpec(memory_space=pl.ANY),
                  pl.BlockSpec((1, D, de), lambda i, be, nu: (be[i], 0, 0)),
                  pl.BlockSpec((1, D, de), lambda i, be, nu: (be[i], 0, 0)),
                  pl.BlockSpec((1, de, D), lambda i, be, nu: (be[i], 0, 0))],
        out_specs=pl.BlockSpec((bm * SUBLANES, LANES), lambda i, be, nu: (i, 0)),
        scratch_shapes=[pltpu.VMEM((2 * bm * SUBLANES, LANES), F32),
                        pltpu.VMEM((D, de), BF16), pltpu.VMEM((D, de), BF16), pltpu.VMEM((de, D), BF16),
                        pltpu.SemaphoreType.DMA((2,))],
    )
    return pl.pallas_call(
        _expert_kernel,
        grid_spec=grid_spec,
        out_shape=jax.ShapeDtypeStruct((nb * bm * SUBLANES, LANES), F32),
        compiler_params=_params(("arbitrary",)),
        name="experts",
    )(blk_e, nused, idx, idx, h2, wg, wu, wd)


def _combine_kernel(n_steps, cur_ref, nxt_ref, y_hbm, x1_ref, route_ref, gt_ref, gf_ref, o_ref, buf, sem):
    i = pl.program_id(0)
    tm = x1_ref.shape[0]
    slot = i % 2
    rows = 2 * tm * SUBLANES

    @pl.when(i == 0)
    def _():
        _gather_start(cur_ref, y_hbm, buf, 0, 2 * tm, sem.at[0])

    @pl.when(i + 1 < n_steps)
    def _():
        _gather_start(nxt_ref, y_hbm, buf, (1 - slot) * rows, 2 * tm, sem.at[1 - slot])

    _gather_wait(y_hbm, buf, slot * rows, 2 * tm, sem.at[slot])
    route = route_ref[...]
    y1 = _from_token_tiles(buf, slot * rows, tm)
    y2 = _from_token_tiles(buf, slot * rows + tm * SUBLANES, tm)
    x = x1_ref[...] + gt_ref[0] * (route[:, 2:3] * y1 + route[:, 3:4] * y2)
    o_ref[...] = x * lax.rsqrt(jnp.mean(x * x, axis=-1, keepdims=True) + EPS) * gf_ref[...]


def _combine(tile0, n_tok, dest_tiles, yb, x1, route, gt2, g_final, L):
    T, D = x1.shape
    tm = TOKEN_TILE
    per_b = L // tm
    n = n_tok // tm
    last = tile0 + n - 1
    smem = lambda f: pl.BlockSpec((1, 1, 2 * tm), lambda i: (f(i), 0, 0), memory_space=pltpu.SMEM)
    return pl.pallas_call(
        functools.partial(_combine_kernel, n),
        grid=(n,),
        in_specs=[smem(lambda i: tile0 + i), smem(lambda i: jnp.minimum(tile0 + i + 1, last)),
                  pl.BlockSpec(memory_space=pl.ANY),
                  pl.BlockSpec((tm, D), lambda i: (tile0 + i, 0)),
                  pl.BlockSpec((tm, LANES), lambda i: (tile0 + i, 0)),
                  pl.BlockSpec((1, 1, D), lambda i: ((tile0 + i) // per_b, 0, 0)),
                  pl.BlockSpec((1, D), lambda i: (0, 0))],
        out_specs=pl.BlockSpec((tm, D), lambda i: (i, 0)),
        out_shape=jax.ShapeDtypeStruct((n_tok, D), F32),
        scratch_shapes=[pltpu.VMEM((2 * 2 * tm * SUBLANES, LANES), F32), pltpu.SemaphoreType.DMA((2,))],
        compiler_params=_params(("arbitrary",)),
        name="combine",
    )(dest_tiles, dest_tiles, yb, x1, route, gt2, g_final.reshape(1, D))


def _encoder(xp, xs, c, p, g_final):
    n_prompt, L, D = xp.shape
    B = n_prompt + xs.shape[0]
    T = B * L
    xa, xb = xp.reshape(-1, D), xs.reshape(-1, D)
    d_f = N_FOURIER_GROUPS * FOURIER_GROUP_DIM
    d_h = (HYENA_ORDER + 1) * D_HYENA
    d_g = 2 * D

    mod = _ada(c, p["w_ada"], p["b_ada"])
    sh1, sc1, gt1, sh2, sc2, gt2 = [mod[:, k * D:(k + 1) * D].reshape(B, 1, D) for k in range(6)]

    wb = jnp.concatenate([p["w_in"][:, :d_f], jnp.broadcast_to(p["b_in"][None, :d_f], (8, d_f))], axis=0)
    fre, fim = _fold_channel_dft(wb)
    w_all = jnp.concatenate([fre[:D], fim[:D], p["w_in"][:, d_f:]], axis=1).astype(BF16)
    b_all = jnp.concatenate([fre[D], fim[D], p["b_in"][d_f:]])[None, :]
    zr, zi, uh, sg = _inproj(xa, xb, sc1, sh1, w_all, b_all, L, d_f, d_h, d_g)

    f = _fourier(zr.reshape(B, L, d_f), zi.reshape(B, L, d_f))

    ktime, ksum = _filt_time(L, p["filt_w1"], p["filt_b1"], p["filt_w2"], p["filt_b2"],
                             p["filt_w3"], p["filt_b3"], p["filt_freq"], p["filt_wout"])
    kf = _filt_fft(L, ktime, ksum)
    uh3 = uh.reshape(B, L, d_h)
    conv_b = p["conv_b"][None, :]
    ncol = D_HYENA // FFT_COLS
    skip = p["hyena_skip"]
    z = _hyena_order(0, uh3, 0, uh3, ncol, p["conv_w"], conv_b, skip[0:1], kf, L)
    z = _hyena_order(1, z, 0, uh3, 2 * ncol, p["conv_w"], conv_b, skip[1:2], kf, L)

    rw = jnp.zeros((D, LANES), F32).at[:, :N_GROUPS].set(p["router_w1"])
    rw = rw.at[:, N_GROUPS:N_GROUPS + N_EXPERTS].set(p["router_w2"])
    rb = jnp.zeros((1, LANES), F32).at[0, :N_GROUPS].set(p["router_b1"])
    rb = rb.at[0, N_GROUPS:N_GROUPS + N_EXPERTS].set(p["router_b2"])
    x1, h2, route, route_t, counts = _merge(f.reshape(T, d_f), z.reshape(T, D_HYENA), sg, xa, xb,
                                            gt1, sc2, sh2, p["w_four"].astype(BF16), p["w_hyena"].astype(BF16),
                                            p["w_out"].astype(BF16), rw, rb, L)

    bm = MOE_BLOCK
    tm = TOKEN_TILE
    nb = (T * TOP_K) // bm + N_EXPERTS
    e = route_t[0:2].astype(jnp.int32)
    rank = route_t[4:6].astype(jnp.int32)
    cnt = counts[0, :N_EXPERTS].astype(jnp.int32)
    pcnt = (cnt + bm - 1) // bm * bm
    pend = jnp.cumsum(pcnt)
    experts = jnp.arange(N_EXPERTS, dtype=jnp.int32)
    dest = rank + jnp.sum(jnp.where(e[..., None] == experts, pend - pcnt, 0), axis=-1)
    tok = jnp.broadcast_to(jnp.arange(T, dtype=jnp.int32)[None, :], (TOP_K, T))
    src_tok = jnp.zeros((nb * bm,), jnp.int32).at[dest.reshape(-1)].set(
        tok.reshape(-1), unique_indices=True, mode="promise_in_bounds")
    blk_row0 = jnp.arange(nb, dtype=jnp.int32)[:, None] * bm
    blk_e = jnp.minimum(jnp.sum((pend[None, :] <= blk_row0).astype(jnp.int32), axis=1), N_EXPERTS - 1)
    nused = (pend[-1] // bm).astype(jnp.int32).reshape(1)
    yb = _experts(blk_e, nused, src_tok, h2, p["exp_w_gate"], p["exp_w_up"], p["exp_w_down"])

    dest_tiles = dest.reshape(TOP_K, T // tm, tm).transpose(1, 0, 2).reshape(T // tm, 1, TOP_K * tm)
    t_prompt = n_prompt * L
    outs = []
    for tile0, n_tok in ((0, t_prompt), (t_prompt // tm, T - t_prompt)):
        outs.append(_combine(tile0, n_tok, dest_tiles, yb, x1, route, gt2, g_final, L))
    return outs[0].reshape(xp.shape), outs[1].reshape(xs.shape)


def kernel(x_prompt, x_sample, c_prompt, c_sample, w_ada, b_ada, w_in, b_in, conv_w, conv_b, filt_w1, filt_b1, filt_w2, filt_b2, filt_w3, filt_b3, filt_freq, filt_wout, hyena_skip, w_four, w_hyena, w_out, router_w1, router_b1, router_w2, router_b2, exp_w_gate, exp_w_up, exp_w_down, g_final):
    assert w_ada.shape[0] == 1, "single-layer block"
    assert x_prompt.shape[1:] == x_sample.shape[1:], "both request groups share sequence length and width"
    p = dict(w_ada=w_ada[0], b_ada=b_ada[0], w_in=w_in[0], b_in=b_in[0], conv_w=conv_w[0], conv_b=conv_b[0],
             filt_w1=filt_w1[0], filt_b1=filt_b1[0], filt_w2=filt_w2[0], filt_b2=filt_b2[0],
             filt_w3=filt_w3[0], filt_b3=filt_b3[0], filt_freq=filt_freq[0], filt_wout=filt_wout[0],
             hyena_skip=hyena_skip[0], w_four=w_four[0], w_hyena=w_hyena[0], w_out=w_out[0],
             router_w1=router_w1[0], router_b1=router_b1[0], router_w2=router_w2[0], router_b2=router_b2[0],
             exp_w_gate=exp_w_gate[0], exp_w_up=exp_w_up[0], exp_w_down=exp_w_down[0])
    c = jnp.concatenate([c_prompt, c_sample], axis=0)
    return _encoder(x_prompt, x_sample, c, p, g_final)
```

```python
import functools
import math

import numpy as np
import jax
import jax.numpy as jnp
from jax import lax
from jax.experimental import pallas as pl
from jax.experimental.pallas import tpu as pltpu

F32 = jnp.float32
BF16 = jnp.bfloat16
HI = lax.Precision.HIGHEST

EPS = 1e-6
FFT_MINOR = 64
N_FOURIER_GROUPS = 4
FOURIER_GROUP_DIM = 128
D_HYENA = 512
HYENA_ORDER = 2
POS_BANDS = 16
N_GROUPS = 4
EXPERTS_PER_GROUP = 8
N_EXPERTS = 32
TOP_K = 2
SHORT_DECAY_PCT = 0.3
LONG_DECAY_PCT = 1.5
DECAY_TARGET = 1e-2
LANES = 128
SUBLANES = 8
VMEM_LIMIT = 56 * 1024 * 1024

TOKEN_TILE = 512
MERGE_SUBTILE = 128
MOE_BLOCK = 256
FFT_COLS = LANES
ROW_CHUNK = 256
FFT_UNROLL = 16


def _dot(a, b):
    return jnp.dot(a, b, preferred_element_type=F32)


def _dot_hi(a, b):
    return jnp.dot(a, b, preferred_element_type=F32, precision=HI)


def _params(sem=None):
    return pltpu.CompilerParams(dimension_semantics=sem, vmem_limit_bytes=VMEM_LIMIT)


def _const_spec(shape):
    nd = len(shape)
    return pl.BlockSpec(shape, lambda *_: (0,) * nd, pipeline_mode=pl.Buffered(1))


def _pitch(rows):
    p = -(-rows // SUBLANES)
    return SUBLANES * (p if p % 2 else p + 1)


@functools.lru_cache(maxsize=None)
def _stage2_tables():
    i = np.arange(FFT_MINOR)
    ph = 2.0 * np.pi * np.outer(i, i) / FFT_MINOR
    c, s = np.cos(ph), np.sin(ph)
    fwd = np.block([[c, s], [-s, c]])
    inv = np.block([[c, -s], [s, c]])
    return fwd, inv, c, s


@functools.lru_cache(maxsize=None)
def _hyena_tables(L):
    n_fft = 2 * L
    m = FFT_MINOR
    na = n_fft // m
    nh = na // 2
    b = np.arange(m)[:, None, None]
    ka = np.arange(nh + 1)[None, :, None]

    def forward(a_count):
        a = np.arange(a_count)[None, None, :]
        th = 2.0 * np.pi * ((ka * (m * a + b)) % n_fft) / n_fft
        c, s = np.cos(th), -np.sin(th)
        pad = np.zeros((m, SUBLANES - 2, a_count))
        return np.concatenate([c[:, :nh], s[:, :nh], c[:, nh:], s[:, nh:], pad], axis=1)

    fwd_half = forward(nh)
    fwd_full = forward(na)
    weight = np.where(np.arange(nh) == 0, 1.0, 2.0)[None, :, None]
    inv_half = np.transpose(fwd_half[:, :2 * nh] * np.concatenate([weight, weight], axis=1), (0, 2, 1)) / n_fft
    sign = np.broadcast_to(((-1.0) ** np.arange(nh))[:, None] / n_fft, (nh, LANES))
    ph = np.pi * np.arange(m) / m
    cos_b = np.broadcast_to(np.cos(ph)[:, None], (m, LANES))
    sin_b = np.broadcast_to(np.sin(ph)[:, None], (m, LANES))
    return fwd_half, inv_half, fwd_full, sign, cos_b, sin_b


@functools.lru_cache(maxsize=None)
def _fourier_tables(L):
    na = L // FFT_MINOR
    b = np.arange(FFT_MINOR)[:, None, None]
    ka = np.arange(na)[None, :, None]
    a = np.arange(na)[None, None, :]
    th = 2.0 * np.pi * ((ka * (FFT_MINOR * a + b)) % L) / L
    c, s = np.cos(th), np.sin(th)
    st1 = np.concatenate([np.concatenate([c, s], axis=2),
                          np.concatenate([-s, c], axis=2)], axis=1)
    _, _, c2, s2 = _stage2_tables()
    st2 = np.concatenate([c2, s2], axis=1) / math.sqrt(L)
    return st1, st2


def _mxu_table(table):
    return jnp.asarray(table, F32).astype(BF16)


@functools.lru_cache(maxsize=None)
def _channel_dft():
    i = np.arange(FOURIER_GROUP_DIM)
    ph = 2.0 * np.pi * np.outer(i, i) / FOURIER_GROUP_DIM
    return np.concatenate([np.cos(ph), -np.sin(ph)], axis=1) / math.sqrt(FOURIER_GROUP_DIM)


def _ada_kernel(c_ref, w_ref, b_ref, o_ref):
    c = c_ref[...]
    o_ref[...] = _dot_hi(c * jax.nn.sigmoid(c), w_ref[...]) + b_ref[...]


def _ada(c, w_ada, b_ada):
    nb, d = c.shape
    n = w_ada.shape[1]
    tn = 1536
    return pl.pallas_call(
        _ada_kernel,
        grid=(n // tn,),
        in_specs=[pl.BlockSpec((nb, d), lambda j: (0, 0)),
                  pl.BlockSpec((d, tn), lambda j: (0, j)),
                  pl.BlockSpec((1, tn), lambda j: (0, j))],
        out_specs=pl.BlockSpec((nb, tn), lambda j: (0, j)),
        out_shape=jax.ShapeDtypeStruct((nb, n), F32),
        compiler_params=_params(("arbitrary",)),
        name="ada",
    )(c, w_ada, b_ada.reshape(1, n))


def _fold_kernel(w_ref, f_ref, re_ref, im_ref):
    z = _dot_hi(w_ref[...], f_ref[...])
    re_ref[...] = z[:, :FOURIER_GROUP_DIM]
    im_ref[...] = z[:, FOURIER_GROUP_DIM:]


def _fold_channel_dft(wb):
    rows = wb.shape[0]
    gd = FOURIER_GROUP_DIM
    fmat = jnp.asarray(_channel_dft(), F32)
    return pl.pallas_call(
        _fold_kernel,
        grid=(N_FOURIER_GROUPS,),
        in_specs=[pl.BlockSpec((rows, gd), lambda g: (0, g)),
                  pl.BlockSpec((gd, 2 * gd), lambda g: (0, 0))],
        out_specs=[pl.BlockSpec((rows, gd), lambda g: (0, g)),
                   pl.BlockSpec((rows, gd), lambda g: (0, g))],
        out_shape=[jax.ShapeDtypeStruct((rows, N_FOURIER_GROUPS * gd), F32)] * 2,
        compiler_params=_params(("arbitrary",)),
        name="fold",
    )(wb, fmat)


def _for_row_chunks(n_rows, body):
    def step(j, carry):
        body(j, pl.multiple_of(j * ROW_CHUNK, ROW_CHUNK))
        return carry

    lax.fori_loop(0, n_rows // ROW_CHUNK, step, 0)


def _store_padded(dst, j, val, run=FFT_MINOR):
    pz = _pitch(run)
    per = ROW_CHUNK // run
    for i in range(per):
        dst[pl.ds(pl.multiple_of((j * per + i) * pz, SUBLANES), run), :] = val[i * run:(i + 1) * run]


def _load_padded(src, j, run=FFT_MINOR):
    pz = _pitch(run)
    per = ROW_CHUNK // run
    return jnp.concatenate([src[pl.ds(pl.multiple_of((j * per + i) * pz, SUBLANES), run), :] for i in range(per)],
                           axis=0)


def _short_conv_chunk(src_ref, w_ref, b_ref, L, j, r0):
    R = ROW_CHUNK
    ct = src_ref.shape[-1]
    row = lax.broadcasted_iota(jnp.int32, (R, ct), 0)
    cur = src_ref[0, pl.ds(r0, R), :]
    before = src_ref[0, pl.ds(jnp.maximum(r0 - 1, 0), 1), :] * jnp.where(j > 0, 1.0, 0.0)
    after = src_ref[0, pl.ds(jnp.minimum(r0 + R, L - 1), 1), :] * jnp.where(j < L // R - 1, 1.0, 0.0)
    up = jnp.where(row == 0, before, pltpu.roll(cur, 1, 0))
    dn = jnp.where(row == R - 1, after, pltpu.roll(cur, R - 1, 0))
    return w_ref[0:1, :] * up + w_ref[1:2, :] * cur + w_ref[2:3, :] * dn + b_ref[...]


def _filt_time_kernel(L, rows, bands_ref, delt_ref, w1_ref, b1_ref, w2_ref, b2_ref, w3_ref, b3_ref,
                      fr_ref, wo_ref, k_ref, sum_ref):
    i = pl.program_id(0)
    n = i * rows + lax.broadcasted_iota(jnp.int32, (rows, 1), 0)
    fwd = n < L
    pos = jnp.where(fwd, n, 2 * L - n).astype(F32)
    t = pos * (1.0 / (L - 1))
    ang = (2.0 * math.pi / L) * pos * bands_ref[...]
    w1 = w1_ref[...]
    pre = (t * w1[0:1, :] + _dot_hi(jnp.cos(ang), w1[1:1 + POS_BANDS, :])
           + _dot_hi(-jnp.sin(ang), w1[1 + POS_BANDS:, :]) + b1_ref[...])
    fr = fr_ref[...]
    h = jnp.sin(fr[0:1, :] * pre)
    h = jnp.sin(fr[1:2, :] * (_dot_hi(h, w2_ref[...]) + b2_ref[...]))
    h = jnp.sin(fr[2:3, :] * (_dot_hi(h, w3_ref[...]) + b3_ref[...]))
    window = jnp.exp(-t * delt_ref[...])
    live = n != L
    parts = []
    for o in range(HYENA_ORDER):
        base = o * 2 * D_HYENA
        hf = _dot_hi(h, wo_ref[:, base:base + D_HYENA])
        hb = _dot_hi(h, wo_ref[:, base + D_HYENA:base + 2 * D_HYENA])
        parts.append(jnp.where(live, jnp.where(fwd, hf, hb) * window, 0.0))
    k = jnp.concatenate(parts, axis=1)
    k_ref[...] = k

    @pl.when(i == 0)
    def _():
        sum_ref[...] = jnp.zeros_like(sum_ref)

    sum_ref[...] += jnp.sum(jnp.abs(k), axis=0, keepdims=True)


def _filt_time(L, w1, b1, w2, b2, w3, b3, freq, wout):
    rows = 512
    n_fft = 2 * L
    cols = HYENA_ORDER * D_HYENA
    bands = jnp.linspace(1e-4, POS_BANDS - 1, POS_BANDS, dtype=F32).reshape(1, POS_BANDS)
    max_decay = math.log(DECAY_TARGET) / SHORT_DECAY_PCT
    min_decay = math.log(DECAY_TARGET) / LONG_DECAY_PCT
    deltas = jnp.abs(jnp.linspace(min_decay, max_decay, D_HYENA, dtype=F32)).reshape(1, D_HYENA)
    args = (bands, deltas, w1, b1.reshape(1, -1), w2, b2.reshape(1, -1), w3, b3.reshape(1, -1), freq, wout)
    return pl.pallas_call(
        functools.partial(_filt_time_kernel, L, rows),
        grid=(n_fft // rows,),
        in_specs=[pl.BlockSpec(a.shape, lambda i: (0, 0)) for a in args],
        out_specs=[pl.BlockSpec((rows, cols), lambda i: (i, 0)),
                   pl.BlockSpec((1, cols), lambda i: (0, 0))],
        out_shape=[jax.ShapeDtypeStruct((n_fft, cols), F32),
                   jax.ShapeDtypeStruct((1, cols), F32)],
        compiler_params=_params(("arbitrary",)),
        name="filt_time",
    )(*args)


def _filt_fft_kernel(na, k_ref, sum_ref, st1_ref, st2_ref, o_ref, kp, s1):
    m = FFT_MINOR
    nh = na // 2
    rows1 = 2 * nh + SUBLANES
    pz, p1 = _pitch(m), _pitch(rows1)
    _for_row_chunks(na * m, lambda j, r0: _store_padded(kp, j, k_ref[pl.ds(r0, ROW_CHUNK), :]))

    def stage1(b, carry):
        slab = kp[pl.ds(b, na, stride=pz), :]
        s1[pl.ds(pl.multiple_of(b * p1, SUBLANES), rows1), :] = _dot_hi(st1_ref[b], slab)
        return carry

    lax.fori_loop(0, m, stage1, 0, unroll=4)
    inv_norm = 1.0 / sum_ref[...]

    def spectrum_row(re_row, im_row):
        a = jnp.concatenate([s1[pl.ds(re_row, m, stride=p1), :], s1[pl.ds(im_row, m, stride=p1), :]], axis=0)
        return _dot_hi(st2_ref[...], a) * inv_norm

    def stage2(ka, carry):
        o_ref[ka] = spectrum_row(ka, nh + ka)
        return carry

    lax.fori_loop(0, nh, stage2, 0, unroll=4)
    o_ref[nh] = spectrum_row(2 * nh, 2 * nh + 1)


def _filt_fft(L, ktime, ksum):
    n_fft, cols = ktime.shape
    m = FFT_MINOR
    na = n_fft // m
    nh = na // 2
    rows1 = 2 * nh + SUBLANES
    ct = FFT_COLS
    fwd_full = _hyena_tables(L)[2]
    st2f = _stage2_tables()[0]
    return pl.pallas_call(
        functools.partial(_filt_fft_kernel, na),
        grid=(cols // ct,),
        in_specs=[pl.BlockSpec((n_fft, ct), lambda j: (0, j)),
                  pl.BlockSpec((1, ct), lambda j: (0, j)),
                  _const_spec((m, rows1, na)),
                  _const_spec((2 * m, 2 * m))],
        out_specs=pl.BlockSpec((nh + 1, 2 * m, ct), lambda j: (0, 0, j)),
        out_shape=jax.ShapeDtypeStruct((nh + 1, 2 * m, cols), F32),
        scratch_shapes=[pltpu.VMEM((na * _pitch(m), ct), F32),
                        pltpu.VMEM((m * _pitch(rows1), ct), F32)],
        compiler_params=_params(("arbitrary",)),
        name="filt_fft",
    )(ktime, ksum, jnp.asarray(fwd_full, F32), jnp.asarray(st2f, F32))


def _rms_mod(x, scale, shift):
    y = x * lax.rsqrt(jnp.mean(x * x, axis=-1, keepdims=True) + EPS)
    return y * (1.0 + scale) + shift


def _group_tile(n_first, a_ref, b_ref):
    return jnp.where(pl.program_id(0) < n_first, a_ref[...], b_ref[...])


def _group_specs(n_first, tm, D):
    return [pl.BlockSpec((tm, D), lambda i: (jnp.minimum(i, n_first - 1), 0)),
            pl.BlockSpec((tm, D), lambda i: (jnp.maximum(i - n_first, 0), 0))]


def _inproj_kernel(d_f, d_h, n_first, xa_ref, xb_ref, sc_ref, sh_ref, w_ref, b_ref, zr_ref, zi_ref, uh_ref, sg_ref):
    h = _rms_mod(_group_tile(n_first, xa_ref, xb_ref), sc_ref[0], sh_ref[0]).astype(BF16)

    def proj(c0, width):
        return _dot(h, w_ref[:, c0:c0 + width]) + b_ref[:, c0:c0 + width]

    zr_ref[...] = proj(0, d_f)
    zi_ref[...] = proj(d_f, d_f)
    chunk = 512
    for c in range(0, d_h, chunk):
        uh_ref[:, c:c + chunk] = proj(2 * d_f + c, chunk)
    d_g = sg_ref.shape[1]
    for c in range(0, d_g, chunk):
        sg_ref[:, c:c + chunk] = jax.nn.sigmoid(proj(2 * d_f + d_h + c, chunk)).astype(BF16)


def _inproj(xa, xb, sc, sh, w, b, L, d_f, d_h, d_g):
    D = xa.shape[1]
    T = xa.shape[0] + xb.shape[0]
    tm = TOKEN_TILE
    per_b = L // tm
    n_first = xa.shape[0] // tm
    n = w.shape[1]
    mod_spec = pl.BlockSpec((1, 1, D), lambda i: (i // per_b, 0, 0))
    return pl.pallas_call(
        functools.partial(_inproj_kernel, d_f, d_h, n_first),
        grid=(T // tm,),
        in_specs=_group_specs(n_first, tm, D) + [mod_spec, mod_spec, _const_spec((D, n)), _const_spec((1, n))],
        out_specs=[pl.BlockSpec((tm, d_f), lambda i: (i, 0)),
                   pl.BlockSpec((tm, d_f), lambda i: (i, 0)),
                   pl.BlockSpec((tm, d_h), lambda i: (i, 0)),
                   pl.BlockSpec((tm, d_g), lambda i: (i, 0))],
        out_shape=[jax.ShapeDtypeStruct((T, d_f), F32), jax.ShapeDtypeStruct((T, d_f), F32),
                   jax.ShapeDtypeStruct((T, d_h), F32), jax.ShapeDtypeStruct((T, d_g), BF16)],
        compiler_params=_params(("parallel",)),
        name="inproj",
    )(xa, xb, sc, sh, w, b)


def _fourier_kernel(na, zr_ref, zi_ref, st1_ref, st2_ref, o_ref, zpr, zpi, s1, op):
    m = FFT_MINOR
    pz, p1 = _pitch(m), _pitch(2 * na)

    def fill(j, r0):
        _store_padded(zpr, j, zr_ref[0, pl.ds(r0, ROW_CHUNK), :])
        _store_padded(zpi, j, zi_ref[0, pl.ds(r0, ROW_CHUNK), :])

    _for_row_chunks(na * m, fill)

    def stage1(b, carry):
        z = jnp.concatenate([zpr[pl.ds(b, na, stride=pz), :], zpi[pl.ds(b, na, stride=pz), :]], axis=0)
        s1[pl.ds(pl.multiple_of(b * p1, SUBLANES), 2 * na), :] = _dot(st1_ref[b], z.astype(BF16))
        return carry

    lax.fori_loop(0, m, stage1, 0, unroll=FFT_UNROLL)

    po = _pitch(na)

    def stage2(ka, carry):
        a = jnp.concatenate([s1[pl.ds(ka, m, stride=p1), :], s1[pl.ds(na + ka, m, stride=p1), :]], axis=0)
        op[pl.ds(ka, m, stride=po), :] = _dot(st2_ref[...], a.astype(BF16))
        return carry

    lax.fori_loop(0, na, stage2, 0, unroll=FFT_UNROLL)

    def unpad(j, r0):
        o_ref[0, pl.ds(r0, ROW_CHUNK), :] = _load_padded(op, j, na)

    _for_row_chunks(na * m, unpad)


def _fourier(zr, zi):
    B, L, C = zr.shape
    m = FFT_MINOR
    na = L // m
    ct = FFT_COLS
    st1, st2 = _fourier_tables(L)
    spec = pl.BlockSpec((1, L, ct), lambda j, b: (b, 0, j))
    pad = pltpu.VMEM((na * _pitch(m), ct), F32)
    return pl.pallas_call(
        functools.partial(_fourier_kernel, na),
        grid=(C // ct, B),
        in_specs=[spec, spec, _const_spec((m, 2 * na, 2 * na)), _const_spec((m, 2 * m))],
        out_specs=spec,
        out_shape=jax.ShapeDtypeStruct((B, L, C), F32),
        scratch_shapes=[pad, pad, pltpu.VMEM((m * _pitch(2 * na), ct), F32),
                        pltpu.VMEM((m * _pitch(na), ct), F32)],
        compiler_params=_params(("parallel", "parallel")),
        name="fourier",
    )(zr, zi, _mxu_table(st1), _mxu_table(st2))


def _hyena_kernel(conv_z, L, zin_ref, gin_ref, cwz_ref, cbz_ref, cwg_ref, cbg_ref, skip_ref, kf_ref,
                  fwd1_ref, inv1_ref, st2f_ref, st2i_ref, sign_ref, cosb_ref, sinb_ref, out_ref, zp, yp, s1, g2, hh):
    m = FFT_MINOR
    nh = L // m
    rows1 = 2 * nh + SUBLANES
    pz, p1, p2 = _pitch(m), _pitch(rows1), _pitch(2 * m)

    def fill(j, r0):
        if conv_z:
            _store_padded(zp, j, _short_conv_chunk(zin_ref, cwz_ref, cbz_ref, L, j, r0))
        else:
            _store_padded(zp, j, zin_ref[0, pl.ds(r0, ROW_CHUNK), :])

    _for_row_chunks(L, fill)

    def stage1(b, carry):
        slab = zp[pl.ds(b, nh, stride=pz), :].astype(BF16)
        s1[pl.ds(pl.multiple_of(b * p1, SUBLANES), rows1), :] = _dot(fwd1_ref[b], slab)
        return carry

    lax.fori_loop(0, m, stage1, 0, unroll=FFT_UNROLL)

    def spectrum_row(ka, re_row, im_row):
        a = jnp.concatenate([s1[pl.ds(re_row, m, stride=p1), :], s1[pl.ds(im_row, m, stride=p1), :]], axis=0)
        x = _dot(st2f_ref[...], a.astype(BF16))
        xr, xi = x[:m], x[m:]
        k = kf_ref[ka]
        kr, ki = k[:m], k[m:]
        y = jnp.concatenate([xr * kr - xi * ki, xr * ki + xi * kr], axis=0).astype(BF16)
        return _dot(st2i_ref[...], y)

    def stage2(ka, carry):
        g2[pl.ds(pl.multiple_of(ka * p2, SUBLANES), 2 * m), :] = spectrum_row(ka, ka, nh + ka)
        return carry

    lax.fori_loop(0, nh, stage2, 0, unroll=FFT_UNROLL)
    g_mid = spectrum_row(nh, 2 * nh, 2 * nh + 1)
    hh[...] = cosb_ref[...] * g_mid[:m] - sinb_ref[...] * g_mid[m:]
    sign = sign_ref[...]

    def stage3(b, carry):
        g = jnp.concatenate([g2[pl.ds(b, nh, stride=p2), :], g2[pl.ds(m + b, nh, stride=p2), :]], axis=0)
        yp[pl.ds(b, nh, stride=pz), :] = _dot(inv1_ref[b], g.astype(BF16)) + sign * hh[pl.ds(b, 1), :]
        return carry

    lax.fori_loop(0, m, stage3, 0, unroll=FFT_UNROLL)
    skip = skip_ref[...]

    def finish(j, r0):
        gate = _short_conv_chunk(gin_ref, cwg_ref, cbg_ref, L, j, r0)
        out_ref[0, pl.ds(r0, ROW_CHUNK), :] = gate * (_load_padded(yp, j) + skip * _load_padded(zp, j))

    _for_row_chunks(L, finish)


def _hyena_order(order, zin, z_col0, uh, g_col0, conv_w, conv_b, skip, kf, L):
    B = uh.shape[0]
    m = FFT_MINOR
    ct = FFT_COLS
    ncol = D_HYENA // ct
    nh = L // m
    rows1 = 2 * nh + SUBLANES
    conv_z = order == 0
    fwd_half, inv_half, _, sign, cos_b, sin_b = _hyena_tables(L)
    st2f, st2i, _, _ = _stage2_tables()
    z_cols = (lambda j, b: (b, 0, z_col0 + j))
    g_cols = (lambda j, b: (b, 0, g_col0 + j))
    zw_col0 = z_col0 if conv_z else 0
    pad = pltpu.VMEM((nh * _pitch(m), ct), F32)
    return pl.pallas_call(
        functools.partial(_hyena_kernel, conv_z, L),
        grid=(ncol, B),
        in_specs=[pl.BlockSpec((1, L, ct), z_cols),
                  pl.BlockSpec((1, L, ct), g_cols),
                  pl.BlockSpec((3, ct), lambda j, b: (0, zw_col0 + j)),
                  pl.BlockSpec((1, ct), lambda j, b: (0, zw_col0 + j)),
                  pl.BlockSpec((3, ct), lambda j, b: (0, g_col0 + j)),
                  pl.BlockSpec((1, ct), lambda j, b: (0, g_col0 + j)),
                  pl.BlockSpec((1, ct), lambda j, b: (0, j)),
                  pl.BlockSpec((nh + 1, 2 * m, ct), lambda j, b: (0, 0, order * ncol + j),
                               pipeline_mode=pl.Buffered(1)),
                  _const_spec((m, rows1, nh)),
                  _const_spec((m, nh, 2 * nh)),
                  _const_spec((2 * m, 2 * m)),
                  _const_spec((2 * m, 2 * m)),
                  _const_spec((nh, LANES)), _const_spec((m, LANES)), _const_spec((m, LANES))],
        out_specs=pl.BlockSpec((1, L, ct), lambda j, b: (b, 0, j)),
        out_shape=jax.ShapeDtypeStruct((B, L, D_HYENA), F32),
        scratch_shapes=[pad, pad,
                        pltpu.VMEM((m * _pitch(rows1), ct), F32),
                        pltpu.VMEM((nh * _pitch(2 * m), ct), F32),
                        pltpu.VMEM((m, ct), F32)],
        compiler_params=_params(("parallel", "arbitrary")),
        name=f"hyena{order}",
    )(zin, uh, conv_w, conv_b, conv_w, conv_b, skip, kf,
      _mxu_table(fwd_half), _mxu_table(inv_half), _mxu_table(st2f), _mxu_table(st2i),
      jnp.asarray(sign, F32), jnp.asarray(cos_b, F32), jnp.asarray(sin_b, F32))


def _merge_kernel(n_first, f_ref, z_ref, sg_ref, xa_ref, xb_ref, gt_ref, sc_ref, sh_ref, wf_ref, wh_ref, wo_ref,
                  rw_ref, rb_ref, tri_ref, x1_ref, h2_ref, route_ref, rt_ref, cnt_ref):
    i = pl.program_id(0)
    tm, D = xa_ref.shape
    sub = MERGE_SUBTILE
    lane = lax.broadcasted_iota(jnp.int32, (sub, LANES), 1).astype(F32)
    neg = -1e30

    def first_max(v):
        mx = jnp.max(v, axis=1, keepdims=True)
        return mx, jnp.min(jnp.where(v == mx, lane, float(LANES)), axis=1, keepdims=True)

    @pl.when(i == 0)
    def _():
        cnt_ref[...] = jnp.zeros_like(cnt_ref)

    def sub_tile(r0, count):
        rows = pl.ds(r0, sub)
        ya = _dot(f_ref[rows, :].astype(BF16), wf_ref[...])
        yb = _dot(z_ref[rows, :].astype(BF16), wh_ref[...])
        merged = sg_ref[rows, :D].astype(F32) * ya + sg_ref[rows, D:].astype(F32) * yb
        x = jnp.where(i < n_first, xa_ref[rows, :], xb_ref[rows, :])
        x1 = x + gt_ref[0] * _dot(merged.astype(BF16), wo_ref[...])
        x1_ref[rows, :] = x1
        h2 = _rms_mod(x1, sc_ref[0], sh_ref[0])
        _to_token_tiles(h2_ref, h2, r0 * SUBLANES)

        logits = _dot_hi(h2, rw_ref[...]) + rb_ref[...]
        gl = jnp.where(lane < N_GROUPS, logits, neg)
        gmax, g = first_max(gl)
        p_g = 1.0 / jnp.sum(jnp.exp(gl - gmax), axis=1, keepdims=True)
        lo = N_GROUPS + EXPERTS_PER_GROUP * g
        el = jnp.where((lane >= lo) & (lane < lo + EXPERTS_PER_GROUP), logits, neg)
        m1, i1 = first_max(el)
        m2, i2 = first_max(jnp.where(lane == i1, neg, el))
        r = jnp.exp(m2 - m1)
        wt1 = p_g / (1.0 + r)
        wt2 = p_g * r / (1.0 + r)
        e1 = i1 - N_GROUPS
        e2 = i2 - N_GROUPS

        onehot = ((lane == e1) | (lane == e2)).astype(BF16)
        before = _dot(tri_ref[...], onehot) + count
        r1 = jnp.sum(jnp.where(lane == e1, before, 0.0), axis=1, keepdims=True)
        r2 = jnp.sum(jnp.where(lane == e2, before, 0.0), axis=1, keepdims=True)

        packed = jnp.zeros((sub, LANES), F32)
        for slot, v in enumerate((e1, e2, wt1, wt2, r1, r2)):
            packed = jnp.where(lane == slot, v, packed)
        route_ref[rows, :] = packed
        rt_ref[:, rows] = packed.T[0:SUBLANES, :]
        return count + jnp.sum(onehot.astype(F32), axis=0, keepdims=True)

    count = cnt_ref[...]
    for h in range(tm // sub):
        count = sub_tile(h * sub, count)
    cnt_ref[...] = count


def _merge(f2d, z2d, sg, xa, xb, gt1, sc2, sh2, w_four, w_hyena, w_out, rw, rb, L):
    D = xa.shape[1]
    T = xa.shape[0] + xb.shape[0]
    assert D == SUBLANES * LANES
    tm = TOKEN_TILE
    per_b = L // tm
    n_first = xa.shape[0] // tm
    d_f = f2d.shape[1]
    d_h = z2d.shape[1]
    sub = MERGE_SUBTILE
    tri = jnp.asarray(np.tril(np.ones((sub, sub)), -1), BF16)
    mod_spec = pl.BlockSpec((1, 1, D), lambda i: (i // per_b, 0, 0))
    row = lambda w: pl.BlockSpec((tm, w), lambda i: (i, 0))
    return pl.pallas_call(
        functools.partial(_merge_kernel, n_first),
        grid=(T // tm,),
        in_specs=[row(d_f), row(d_h), row(2 * D)] + _group_specs(n_first, tm, D) + [
            mod_spec, mod_spec, mod_spec,
            _const_spec((d_f, D)), _const_spec((d_h, D)), _const_spec((D, D)),
            _const_spec((D, LANES)), _const_spec((1, LANES)), _const_spec((sub, sub))],
        out_specs=[row(D), pl.BlockSpec((tm * SUBLANES, LANES), lambda i: (i, 0)), row(LANES),
                   pl.BlockSpec((SUBLANES, tm), lambda i: (0, i)), pl.BlockSpec((1, LANES), lambda i: (0, 0))],
        out_shape=[jax.ShapeDtypeStruct((T, D), F32), jax.ShapeDtypeStruct((T * SUBLANES, LANES), F32),
                   jax.ShapeDtypeStruct((T, LANES), F32), jax.ShapeDtypeStruct((SUBLANES, T), F32),
                   jax.ShapeDtypeStruct((1, LANES), F32)],
        compiler_params=_params(("arbitrary",)),
        name="merge",
    )(f2d, z2d, sg, xa, xb, gt1, sc2, sh2, w_four, w_hyena, w_out, rw, rb, tri)


def _to_token_tiles(ref, val, row0=0):
    n = val.shape[0]
    for s in range(SUBLANES):
        ref[pl.ds(row0 + s, n, stride=SUBLANES), :] = val[:, s * LANES:(s + 1) * LANES]


def _from_token_tiles(ref, row0, n):
    return jnp.concatenate([ref[pl.ds(row0 + s, n, stride=SUBLANES), :] for s in range(SUBLANES)], axis=1)


def _gather_start(idx_ref, src_hbm, dst, row0, n_tok, sem):
    def issue(r, carry):
        src = pl.multiple_of(idx_ref[0, 0, r] * SUBLANES, SUBLANES)
        row = pl.multiple_of(row0 + r * SUBLANES, SUBLANES)
        pltpu.make_async_copy(src_hbm.at[pl.ds(src, SUBLANES)], dst.at[pl.ds(row, SUBLANES)], sem).start()
        return carry

    lax.fori_loop(0, n_tok, issue, 0, unroll=8)


def _gather_wait(src_hbm, dst, row0, n_tok, sem):
    rows = n_tok * SUBLANES
    pltpu.make_async_copy(src_hbm.at[pl.ds(0, rows)], dst.at[pl.ds(row0, rows)], sem).wait()


def _dispatch_kernel(n_steps, pend_ref, dest_ref, h_hbm, x_hbm, zero, sem, zsem):
    i = pl.program_id(0)
    slot = i % 2
    tm = dest_ref.shape[2] // TOP_K
    bm = MOE_BLOCK

    def token_rows(ref, tok):
        return ref.at[pl.ds(pl.multiple_of(tok * SUBLANES, SUBLANES), SUBLANES)]

    def drain(n_tok, s):
        rows = n_tok * SUBLANES
        pltpu.make_async_copy(h_hbm.at[pl.ds(0, rows)], x_hbm.at[pl.ds(0, rows)], s).wait()

    @pl.when(i == 0)
    def _():
        zero[...] = jnp.zeros_like(zero)
        n_blocks = x_hbm.shape[0] // (bm * SUBLANES)
        used = pend_ref[N_EXPERTS - 1] // bm

        def zero_block(first_tok):
            row = pl.multiple_of(first_tok * SUBLANES, bm * SUBLANES)
            pltpu.make_async_copy(zero, x_hbm.at[pl.ds(row, bm * SUBLANES)], zsem).start()

        def per_expert(e, carry):
            zero_block(jnp.maximum(pend_ref[e] - bm, 0))
            return carry

        def per_unused(blk, carry):
            zero_block(blk * bm)
            return carry

        def drain_block(blk, carry):
            drain(bm, zsem)
            return carry

        lax.fori_loop(0, N_EXPERTS, per_expert, 0)
        lax.fori_loop(used, n_blocks, per_unused, 0)
        lax.fori_loop(0, N_EXPERTS + n_blocks - used, drain_block, 0)

    for k in range(TOP_K):
        def issue(r, carry, k=k):
            src = token_rows(h_hbm, i * tm + r)
            pltpu.make_async_copy(src, token_rows(x_hbm, dest_ref[0, 0, k * tm + r]), sem.at[slot]).start()
            return carry

        lax.fori_loop(0, tm, issue, 0, unroll=8)

    @pl.when(i > 0)
    def _():
        drain(TOP_K * tm, sem.at[1 - slot])

    @pl.when(i == n_steps - 1)
    def _():
        drain(TOP_K * tm, sem.at[slot])


def _dispatch(pend, dest_tiles, h2, n_rows):
    n, _, tm2 = dest_tiles.shape
    grid_spec = pltpu.PrefetchScalarGridSpec(
        num_scalar_prefetch=1,
        grid=(n,),
        in_specs=[pl.BlockSpec((1, 1, tm2), lambda i, pe: (i, 0, 0), memory_space=pltpu.SMEM),
                  pl.BlockSpec(memory_space=pl.ANY)],
        out_specs=pl.BlockSpec(memory_space=pl.ANY),
        scratch_shapes=[pltpu.VMEM((MOE_BLOCK * SUBLANES, LANES), F32), pltpu.SemaphoreType.DMA((2,)),
                        pltpu.SemaphoreType.DMA(())],
    )
    return pl.pallas_call(
        functools.partial(_dispatch_kernel, n),
        grid_spec=grid_spec,
        out_shape=jax.ShapeDtypeStruct((n_rows * SUBLANES, LANES), F32),
        compiler_params=_params(("arbitrary",)),
        name="dispatch",
    )(pend, dest_tiles, h2)


def _expert_kernel(blk_e_ref, nused_ref, x_ref, wg_ref, wu_ref, wd_ref, y_ref, wg16, wu16, wd16):
    i = pl.program_id(0)
    nused = nused_ref[0]
    bm = MOE_BLOCK

    @pl.when((i == 0) | (blk_e_ref[i] != blk_e_ref[jnp.maximum(i - 1, 0)]))
    def _():
        wg16[...] = wg_ref[0].astype(BF16)
        wu16[...] = wu_ref[0].astype(BF16)
        wd16[...] = wd_ref[0].astype(BF16)

    @pl.when(i < nused)
    def _():
        x = _from_token_tiles(x_ref, 0, bm).astype(BF16)
        g = _dot(x, wg16[...])
        u = _dot(x, wu16[...])
        a = (g * jax.nn.sigmoid(g) * u).astype(BF16)
        _to_token_tiles(y_ref, _dot(a, wd16[...]))

    @pl.when(i >= nused)
    def _():
        y_ref[...] = jnp.zeros_like(y_ref)


def _experts(blk_e, nused, xb, wg, wu, wd):
    nb = blk_e.shape[0]
    bm = MOE_BLOCK
    D, de = wg.shape[1:]
    assert D == SUBLANES * LANES and xb.shape == (nb * bm * SUBLANES, LANES)
    grid_spec = pltpu.PrefetchScalarGridSpec(
        num_scalar_prefetch=2,
        grid=(nb,),
        in_specs=[pl.BlockSpec((bm * SUBLANES, LANES), lambda i, be, nu: (jnp.minimum(i, nu[0] - 1), 0)),
                  pl.BlockSpec((1, D, de), lambda i, be, nu: (be[i], 0, 0)),
                  pl.BlockSpec((1, D, de), lambda i, be, nu: (be[i], 0, 0)),
                  pl.BlockSpec((1, de, D), lambda i, be, nu: (be[i], 0, 0))],
        out_specs=pl.BlockSpec((bm * SUBLANES, LANES), lambda i, be, nu: (i, 0)),
        scratch_shapes=[pltpu.VMEM((D, de), BF16), pltpu.VMEM((D, de), BF16), pltpu.VMEM((de, D), BF16)],
    )
    return pl.pallas_call(
        _expert_kernel,
        grid_spec=grid_spec,
        out_shape=jax.ShapeDtypeStruct((nb * bm * SUBLANES, LANES), F32),
        compiler_params=_params(("arbitrary",)),
        name="experts",
    )(blk_e, nused, xb, wg, wu, wd)


def _combine_kernel(n_steps, cur_ref, nxt_ref, y_hbm, x1_ref, route_ref, gt_ref, gf_ref, o_ref, buf, sem):
    i = pl.program_id(0)
    tm = x1_ref.shape[0]
    slot = i % 2
    rows = 2 * tm * SUBLANES

    @pl.when(i == 0)
    def _():
        _gather_start(cur_ref, y_hbm, buf, 0, 2 * tm, sem.at[0])

    @pl.when(i + 1 < n_steps)
    def _():
        _gather_start(nxt_ref, y_hbm, buf, (1 - slot) * rows, 2 * tm, sem.at[1 - slot])

    _gather_wait(y_hbm, buf, slot * rows, 2 * tm, sem.at[slot])
    route = route_ref[...]
    y1 = _from_token_tiles(buf, slot * rows, tm)
    y2 = _from_token_tiles(buf, slot * rows + tm * SUBLANES, tm)
    x = x1_ref[...] + gt_ref[0] * (route[:, 2:3] * y1 + route[:, 3:4] * y2)
    o_ref[...] = x * lax.rsqrt(jnp.mean(x * x, axis=-1, keepdims=True) + EPS) * gf_ref[...]


def _combine(tile0, n_tok, dest_tiles, yb, x1, route, gt2, g_final, L):
    T, D = x1.shape
    tm = TOKEN_TILE
    per_b = L // tm
    n = n_tok // tm
    last = tile0 + n - 1
    smem = lambda f: pl.BlockSpec((1, 1, 2 * tm), lambda i: (f(i), 0, 0), memory_space=pltpu.SMEM)
    return pl.pallas_call(
        functools.partial(_combine_kernel, n),
        grid=(n,),
        in_specs=[smem(lambda i: tile0 + i), smem(lambda i: jnp.minimum(tile0 + i + 1, last)),
                  pl.BlockSpec(memory_space=pl.ANY),
                  pl.BlockSpec((tm, D), lambda i: (tile0 + i, 0)),
                  pl.BlockSpec((tm, LANES), lambda i: (tile0 + i, 0)),
                  pl.BlockSpec((1, 1, D), lambda i: ((tile0 + i) // per_b, 0, 0)),
                  pl.BlockSpec((1, D), lambda i: (0, 0))],
        out_specs=pl.BlockSpec((tm, D), lambda i: (i, 0)),
        out_shape=jax.ShapeDtypeStruct((n_tok, D), F32),
        scratch_shapes=[pltpu.VMEM((2 * 2 * tm * SUBLANES, LANES), F32), pltpu.SemaphoreType.DMA((2,))],
        compiler_params=_params(("arbitrary",)),
        name="combine",
    )(dest_tiles, dest_tiles, yb, x1, route, gt2, g_final.reshape(1, D))


def _encoder(xp, xs, c, p, g_final):
    n_prompt, L, D = xp.shape
    B = n_prompt + xs.shape[0]
    T = B * L
    xa, xb = xp.reshape(-1, D), xs.reshape(-1, D)
    d_f = N_FOURIER_GROUPS * FOURIER_GROUP_DIM
    d_h = (HYENA_ORDER + 1) * D_HYENA
    d_g = 2 * D

    mod = _ada(c, p["w_ada"], p["b_ada"])
    sh1, sc1, gt1, sh2, sc2, gt2 = [mod[:, k * D:(k + 1) * D].reshape(B, 1, D) for k in range(6)]

    wb = jnp.concatenate([p["w_in"][:, :d_f], jnp.broadcast_to(p["b_in"][None, :d_f], (8, d_f))], axis=0)
    fre, fim = _fold_channel_dft(wb)
    w_all = jnp.concatenate([fre[:D], fim[:D], p["w_in"][:, d_f:]], axis=1).astype(BF16)
    b_all = jnp.concatenate([fre[D], fim[D], p["b_in"][d_f:]])[None, :]
    zr, zi, uh, sg = _inproj(xa, xb, sc1, sh1, w_all, b_all, L, d_f, d_h, d_g)

    f = _fourier(zr.reshape(B, L, d_f), zi.reshape(B, L, d_f))

    ktime, ksum = _filt_time(L, p["filt_w1"], p["filt_b1"], p["filt_w2"], p["filt_b2"],
                             p["filt_w3"], p["filt_b3"], p["filt_freq"], p["filt_wout"])
    kf = _filt_fft(L, ktime, ksum)
    uh3 = uh.reshape(B, L, d_h)
    conv_b = p["conv_b"][None, :]
    ncol = D_HYENA // FFT_COLS
    skip = p["hyena_skip"]
    z = _hyena_order(0, uh3, 0, uh3, ncol, p["conv_w"], conv_b, skip[0:1], kf, L)
    z = _hyena_order(1, z, 0, uh3, 2 * ncol, p["conv_w"], conv_b, skip[1:2], kf, L)

    rw = jnp.zeros((D, LANES), F32).at[:, :N_GROUPS].set(p["router_w1"])
    rw = rw.at[:, N_GROUPS:N_GROUPS + N_EXPERTS].set(p["router_w2"])
    rb = jnp.zeros((1, LANES), F32).at[0, :N_GROUPS].set(p["router_b1"])
    rb = rb.at[0, N_GROUPS:N_GROUPS + N_EXPERTS].set(p["router_b2"])
    x1, h2, route, route_t, counts = _merge(f.reshape(T, d_f), z.reshape(T, D_HYENA), sg, xa, xb,
                                            gt1, sc2, sh2, p["w_four"].astype(BF16), p["w_hyena"].astype(BF16),
                                            p["w_out"].astype(BF16), rw, rb, L)

    bm = MOE_BLOCK
    tm = TOKEN_TILE
    nb = (T * TOP_K) // bm + N_EXPERTS
    e = route_t[0:2].astype(jnp.int32)
    rank = route_t[4:6].astype(jnp.int32)
    cnt = counts[0, :N_EXPERTS].astype(jnp.int32)
    pcnt = (cnt + bm - 1) // bm * bm
    pend = jnp.cumsum(pcnt)
    experts = jnp.arange(N_EXPERTS, dtype=jnp.int32)
    dest = rank + jnp.sum(jnp.where(e[..., None] == experts, pend - pcnt, 0), axis=-1)
    blk_row0 = jnp.arange(nb, dtype=jnp.int32)[:, None] * bm
    blk_e = jnp.minimum(jnp.sum((pend[None, :] <= blk_row0).astype(jnp.int32), axis=1), N_EXPERTS - 1)
    nused = (pend[-1] // bm).astype(jnp.int32).reshape(1)
    dest_tiles = dest.reshape(TOP_K, T // tm, tm).transpose(1, 0, 2).reshape(T // tm, 1, TOP_K * tm)
    xb = _dispatch(pend.astype(jnp.int32), dest_tiles, h2, nb * bm)
    yb = _experts(blk_e, nused, xb, p["exp_w_gate"], p["exp_w_up"], p["exp_w_down"])

    t_prompt = n_prompt * L
    outs = []
    for tile0, n_tok in ((0, t_prompt), (t_prompt // tm, T - t_prompt)):
        outs.append(_combine(tile0, n_tok, dest_tiles, yb, x1, route, gt2, g_final, L))
    return outs[0].reshape(xp.shape), outs[1].reshape(xs.shape)


def kernel(x_prompt, x_sample, c_prompt, c_sample, w_ada, b_ada, w_in, b_in, conv_w, conv_b, filt_w1, filt_b1, filt_w2, filt_b2, filt_w3, filt_b3, filt_freq, filt_wout, hyena_skip, w_four, w_hyena, w_out, router_w1, router_b1, router_w2, router_b2, exp_w_gate, exp_w_up, exp_w_down, g_final):
    assert w_ada.shape[0] == 1, "single-layer block"
    assert x_prompt.shape[1:] == x_sample.shape[1:], "both request groups share sequence length and width"
    p = dict(w_ada=w_ada[0], b_ada=b_ada[0], w_in=w_in[0], b_in=b_in[0], conv_w=conv_w[0], conv_b=conv_b[0],
             filt_w1=filt_w1[0], filt_b1=filt_b1[0], filt_w2=filt_w2[0], filt_b2=filt_b2[0],
             filt_w3=filt_w3[0], filt_b3=filt_b3[0], filt_freq=filt_freq[0], filt_wout=filt_wout[0],
             hyena_skip=hyena_skip[0], w_four=w_four[0], w_hyena=w_hyena[0], w_out=w_out[0],
             router_w1=router_w1[0], router_b1=router_b1[0], router_w2=router_w2[0], router_b2=router_b2[0],
             exp_w_gate=exp_w_gate[0], exp_w_up=exp_w_up[0], exp_w_down=exp_w_down[0])
    c = jnp.concatenate([c_prompt, c_sample], axis=0)
    return _encoder(x_prompt, x_sample, c, p, g_final)
```

```python
import functools
import math

import numpy as np
import jax
import jax.numpy as jnp
from jax import lax
from jax.experimental import pallas as pl
from jax.experimental.pallas import tpu as pltpu

F32 = jnp.float32
BF16 = jnp.bfloat16
HI = lax.Precision.HIGHEST

EPS = 1e-6
FFT_MINOR = 64
N_FOURIER_GROUPS = 4
FOURIER_GROUP_DIM = 128
D_HYENA = 512
HYENA_ORDER = 2
POS_BANDS = 16
N_GROUPS = 4
EXPERTS_PER_GROUP = 8
N_EXPERTS = 32
TOP_K = 2
SHORT_DECAY_PCT = 0.3
LONG_DECAY_PCT = 1.5
DECAY_TARGET = 1e-2
LANES = 128
SUBLANES = 8
VMEM_LIMIT = 56 * 1024 * 1024

TOKEN_TILE = 512
MERGE_SUBTILE = 128
MOE_BLOCK = 256
FFT_COLS = LANES
ROW_CHUNK = 256
FFT_UNROLL = 16


def _dot(a, b):
    return jnp.dot(a, b, preferred_element_type=F32)


def _dot_hi(a, b):
    return jnp.dot(a, b, preferred_element_type=F32, precision=HI)


def _params(sem=None):
    return pltpu.CompilerParams(dimension_semantics=sem, vmem_limit_bytes=VMEM_LIMIT)


def _const_spec(shape):
    nd = len(shape)
    return pl.BlockSpec(shape, lambda *_: (0,) * nd, pipeline_mode=pl.Buffered(1))


def _pitch(rows):
    p = -(-rows // SUBLANES)
    return SUBLANES * (p if p % 2 else p + 1)


@functools.lru_cache(maxsize=None)
def _stage2_tables():
    i = np.arange(FFT_MINOR)
    ph = 2.0 * np.pi * np.outer(i, i) / FFT_MINOR
    c, s = np.cos(ph), np.sin(ph)
    fwd = np.block([[c, s], [-s, c]])
    inv = np.block([[c, -s], [s, c]])
    return fwd, inv, c, s


@functools.lru_cache(maxsize=None)
def _hyena_tables(L):
    n_fft = 2 * L
    m = FFT_MINOR
    na = n_fft // m
    nh = na // 2
    b = np.arange(m)[:, None, None]
    ka = np.arange(nh + 1)[None, :, None]

    def forward(a_count):
        a = np.arange(a_count)[None, None, :]
        th = 2.0 * np.pi * ((ka * (m * a + b)) % n_fft) / n_fft
        c, s = np.cos(th), -np.sin(th)
        pad = np.zeros((m, SUBLANES - 2, a_count))
        return np.concatenate([c[:, :nh], s[:, :nh], c[:, nh:], s[:, nh:], pad], axis=1)

    fwd_half = forward(nh)
    fwd_full = forward(na)
    weight = np.where(np.arange(nh) == 0, 1.0, 2.0)[None, :, None]
    inv_half = np.transpose(fwd_half[:, :2 * nh] * np.concatenate([weight, weight], axis=1), (0, 2, 1)) / n_fft
    sign = np.broadcast_to(((-1.0) ** np.arange(nh))[:, None] / n_fft, (nh, LANES))
    ph = np.pi * np.arange(m) / m
    cos_b = np.broadcast_to(np.cos(ph)[:, None], (m, LANES))
    sin_b = np.broadcast_to(np.sin(ph)[:, None], (m, LANES))
    return fwd_half, inv_half, fwd_full, sign, cos_b, sin_b


@functools.lru_cache(maxsize=None)
def _fourier_tables(L):
    na = L // FFT_MINOR
    b = np.arange(FFT_MINOR)[:, None, None]
    ka = np.arange(na)[None, :, None]
    a = np.arange(na)[None, None, :]
    th = 2.0 * np.pi * ((ka * (FFT_MINOR * a + b)) % L) / L
    c, s = np.cos(th), np.sin(th)
    st1 = np.concatenate([np.concatenate([c, s], axis=2),
                          np.concatenate([-s, c], axis=2)], axis=1)
    _, _, c2, s2 = _stage2_tables()
    st2 = np.concatenate([c2, s2], axis=1) / math.sqrt(L)
    return st1, st2


def _mxu_table(table):
    return jnp.asarray(table, F32).astype(BF16)


@functools.lru_cache(maxsize=None)
def _channel_dft():
    i = np.arange(FOURIER_GROUP_DIM)
    ph = 2.0 * np.pi * np.outer(i, i) / FOURIER_GROUP_DIM
    return np.concatenate([np.cos(ph), -np.sin(ph)], axis=1) / math.sqrt(FOURIER_GROUP_DIM)


def _ada_kernel(c_ref, w_ref, b_ref, o_ref):
    c = c_ref[...]
    o_ref[...] = _dot_hi(c * jax.nn.sigmoid(c), w_ref[...]) + b_ref[...]


def _ada(c, w_ada, b_ada):
    nb, d = c.shape
    n = w_ada.shape[1]
    tn = 1536
    return pl.pallas_call(
        _ada_kernel,
        grid=(n // tn,),
        in_specs=[pl.BlockSpec((nb, d), lambda j: (0, 0)),
                  pl.BlockSpec((d, tn), lambda j: (0, j)),
                  pl.BlockSpec((1, tn), lambda j: (0, j))],
        out_specs=pl.BlockSpec((nb, tn), lambda j: (0, j)),
        out_shape=jax.ShapeDtypeStruct((nb, n), F32),
        compiler_params=_params(("arbitrary",)),
        name="ada",
    )(c, w_ada, b_ada.reshape(1, n))


def _fold_kernel(w_ref, f_ref, re_ref, im_ref):
    z = _dot_hi(w_ref[...], f_ref[...])
    re_ref[...] = z[:, :FOURIER_GROUP_DIM]
    im_ref[...] = z[:, FOURIER_GROUP_DIM:]


def _fold_channel_dft(wb):
    rows = wb.shape[0]
    gd = FOURIER_GROUP_DIM
    fmat = jnp.asarray(_channel_dft(), F32)
    return pl.pallas_call(
        _fold_kernel,
        grid=(N_FOURIER_GROUPS,),
        in_specs=[pl.BlockSpec((rows, gd), lambda g: (0, g)),
                  pl.BlockSpec((gd, 2 * gd), lambda g: (0, 0))],
        out_specs=[pl.BlockSpec((rows, gd), lambda g: (0, g)),
                   pl.BlockSpec((rows, gd), lambda g: (0, g))],
        out_shape=[jax.ShapeDtypeStruct((rows, N_FOURIER_GROUPS * gd), F32)] * 2,
        compiler_params=_params(("arbitrary",)),
        name="fold",
    )(wb, fmat)


def _for_row_chunks(n_rows, body):
    def step(j, carry):
        body(j, pl.multiple_of(j * ROW_CHUNK, ROW_CHUNK))
        return carry

    lax.fori_loop(0, n_rows // ROW_CHUNK, step, 0)


def _store_padded(dst, j, val, run=FFT_MINOR):
    pz = _pitch(run)
    per = ROW_CHUNK // run
    for i in range(per):
        dst[pl.ds(pl.multiple_of((j * per + i) * pz, SUBLANES), run), :] = val[i * run:(i + 1) * run]


def _load_padded(src, j, run=FFT_MINOR):
    pz = _pitch(run)
    per = ROW_CHUNK // run
    return jnp.concatenate([src[pl.ds(pl.multiple_of((j * per + i) * pz, SUBLANES), run), :] for i in range(per)],
                           axis=0)


def _short_conv_chunk(src_ref, w_ref, b_ref, L, j, r0):
    R = ROW_CHUNK
    ct = src_ref.shape[-1]
    row = lax.broadcasted_iota(jnp.int32, (R, ct), 0)
    cur = src_ref[0, pl.ds(r0, R), :]
    before = src_ref[0, pl.ds(jnp.maximum(r0 - 1, 0), 1), :] * jnp.where(j > 0, 1.0, 0.0)
    after = src_ref[0, pl.ds(jnp.minimum(r0 + R, L - 1), 1), :] * jnp.where(j < L // R - 1, 1.0, 0.0)
    up = jnp.where(row == 0, before, pltpu.roll(cur, 1, 0))
    dn = jnp.where(row == R - 1, after, pltpu.roll(cur, R - 1, 0))
    return w_ref[0:1, :] * up + w_ref[1:2, :] * cur + w_ref[2:3, :] * dn + b_ref[...]


def _filt_time_kernel(L, rows, bands_ref, delt_ref, w1_ref, b1_ref, w2_ref, b2_ref, w3_ref, b3_ref,
                      fr_ref, wo_ref, k_ref, sum_ref):
    i = pl.program_id(0)
    n = i * rows + lax.broadcasted_iota(jnp.int32, (rows, 1), 0)
    fwd = n < L
    pos = jnp.where(fwd, n, 2 * L - n).astype(F32)
    t = pos * (1.0 / (L - 1))
    ang = (2.0 * math.pi / L) * pos * bands_ref[...]
    w1 = w1_ref[...]
    pre = (t * w1[0:1, :] + _dot_hi(jnp.cos(ang), w1[1:1 + POS_BANDS, :])
           + _dot_hi(-jnp.sin(ang), w1[1 + POS_BANDS:, :]) + b1_ref[...])
    fr = fr_ref[...]
    h = jnp.sin(fr[0:1, :] * pre)
    h = jnp.sin(fr[1:2, :] * (_dot_hi(h, w2_ref[...]) + b2_ref[...]))
    h = jnp.sin(fr[2:3, :] * (_dot_hi(h, w3_ref[...]) + b3_ref[...]))
    window = jnp.exp(-t * delt_ref[...])
    live = n != L
    parts = []
    for o in range(HYENA_ORDER):
        base = o * 2 * D_HYENA
        hf = _dot_hi(h, wo_ref[:, base:base + D_HYENA])
        hb = _dot_hi(h, wo_ref[:, base + D_HYENA:base + 2 * D_HYENA])
        parts.append(jnp.where(live, jnp.where(fwd, hf, hb) * window, 0.0))
    k = jnp.concatenate(parts, axis=1)
    k_ref[...] = k

    @pl.when(i == 0)
    def _():
        sum_ref[...] = jnp.zeros_like(sum_ref)

    sum_ref[...] += jnp.sum(jnp.abs(k), axis=0, keepdims=True)


def _filt_time(L, w1, b1, w2, b2, w3, b3, freq, wout):
    rows = 512
    n_fft = 2 * L
    cols = HYENA_ORDER * D_HYENA
    bands = jnp.linspace(1e-4, POS_BANDS - 1, POS_BANDS, dtype=F32).reshape(1, POS_BANDS)
    max_decay = math.log(DECAY_TARGET) / SHORT_DECAY_PCT
    min_decay = math.log(DECAY_TARGET) / LONG_DECAY_PCT
    deltas = jnp.abs(jnp.linspace(min_decay, max_decay, D_HYENA, dtype=F32)).reshape(1, D_HYENA)
    args = (bands, deltas, w1, b1.reshape(1, -1), w2, b2.reshape(1, -1), w3, b3.reshape(1, -1), freq, wout)
    return pl.pallas_call(
        functools.partial(_filt_time_kernel, L, rows),
        grid=(n_fft // rows,),
        in_specs=[pl.BlockSpec(a.shape, lambda i: (0, 0)) for a in args],
        out_specs=[pl.BlockSpec((rows, cols), lambda i: (i, 0)),
                   pl.BlockSpec((1, cols), lambda i: (0, 0))],
        out_shape=[jax.ShapeDtypeStruct((n_fft, cols), F32),
                   jax.ShapeDtypeStruct((1, cols), F32)],
        compiler_params=_params(("arbitrary",)),
        name="filt_time",
    )(*args)


def _filt_fft_kernel(na, k_ref, sum_ref, st1_ref, st2_ref, o_ref, kp, s1):
    m = FFT_MINOR
    nh = na // 2
    rows1 = 2 * nh + SUBLANES
    pz, p1 = _pitch(m), _pitch(rows1)
    _for_row_chunks(na * m, lambda j, r0: _store_padded(kp, j, k_ref[pl.ds(r0, ROW_CHUNK), :]))

    def stage1(b, carry):
        slab = kp[pl.ds(b, na, stride=pz), :]
        s1[pl.ds(pl.multiple_of(b * p1, SUBLANES), rows1), :] = _dot_hi(st1_ref[b], slab)
        return carry

    lax.fori_loop(0, m, stage1, 0, unroll=4)
    inv_norm = 1.0 / sum_ref[...]

    def spectrum_row(re_row, im_row):
        a = jnp.concatenate([s1[pl.ds(re_row, m, stride=p1), :], s1[pl.ds(im_row, m, stride=p1), :]], axis=0)
        return _dot_hi(st2_ref[...], a) * inv_norm

    def stage2(ka, carry):
        o_ref[ka] = spectrum_row(ka, nh + ka)
        return carry

    lax.fori_loop(0, nh, stage2, 0, unroll=4)
    o_ref[nh] = spectrum_row(2 * nh, 2 * nh + 1)


def _filt_fft(L, ktime, ksum):
    n_fft, cols = ktime.shape
    m = FFT_MINOR
    na = n_fft // m
    nh = na // 2
    rows1 = 2 * nh + SUBLANES
    ct = FFT_COLS
    fwd_full = _hyena_tables(L)[2]
    st2f = _stage2_tables()[0]
    return pl.pallas_call(
        functools.partial(_filt_fft_kernel, na),
        grid=(cols // ct,),
        in_specs=[pl.BlockSpec((n_fft, ct), lambda j: (0, j)),
                  pl.BlockSpec((1, ct), lambda j: (0, j)),
                  _const_spec((m, rows1, na)),
                  _const_spec((2 * m, 2 * m))],
        out_specs=pl.BlockSpec((nh + 1, 2 * m, ct), lambda j: (0, 0, j)),
        out_shape=jax.ShapeDtypeStruct((nh + 1, 2 * m, cols), F32),
        scratch_shapes=[pltpu.VMEM((na * _pitch(m), ct), F32),
                        pltpu.VMEM((m * _pitch(rows1), ct), F32)],
        compiler_params=_params(("arbitrary",)),
        name="filt_fft",
    )(ktime, ksum, jnp.asarray(fwd_full, F32), jnp.asarray(st2f, F32))


def _rms_mod(x, scale, shift):
    y = x * lax.rsqrt(jnp.mean(x * x, axis=-1, keepdims=True) + EPS)
    return y * (1.0 + scale) + shift


def _group_tile(n_first, a_ref, b_ref):
    return jnp.where(pl.program_id(0) < n_first, a_ref[...], b_ref[...])


def _group_specs(n_first, tm, D):
    return [pl.BlockSpec((tm, D), lambda i: (jnp.minimum(i, n_first - 1), 0)),
            pl.BlockSpec((tm, D), lambda i: (jnp.maximum(i - n_first, 0), 0))]


def _inproj_kernel(d_f, d_h, n_first, xa_ref, xb_ref, sc_ref, sh_ref, w_ref, b_ref, zr_ref, zi_ref, uh_ref, sg_ref):
    h = _rms_mod(_group_tile(n_first, xa_ref, xb_ref), sc_ref[0], sh_ref[0]).astype(BF16)

    def proj(c0, width):
        return _dot(h, w_ref[:, c0:c0 + width]) + b_ref[:, c0:c0 + width]

    zr_ref[...] = proj(0, d_f)
    zi_ref[...] = proj(d_f, d_f)
    chunk = 512
    for c in range(0, d_h, chunk):
        uh_ref[:, c:c + chunk] = proj(2 * d_f + c, chunk)
    d_g = sg_ref.shape[1]
    for c in range(0, d_g, chunk):
        sg_ref[:, c:c + chunk] = jax.nn.sigmoid(proj(2 * d_f + d_h + c, chunk)).astype(BF16)


def _inproj(xa, xb, sc, sh, w, b, L, d_f, d_h, d_g):
    D = xa.shape[1]
    T = xa.shape[0] + xb.shape[0]
    tm = TOKEN_TILE
    per_b = L // tm
    n_first = xa.shape[0] // tm
    n = w.shape[1]
    mod_spec = pl.BlockSpec((1, 1, D), lambda i: (i // per_b, 0, 0))
    return pl.pallas_call(
        functools.partial(_inproj_kernel, d_f, d_h, n_first),
        grid=(T // tm,),
        in_specs=_group_specs(n_first, tm, D) + [mod_spec, mod_spec, _const_spec((D, n)), _const_spec((1, n))],
        out_specs=[pl.BlockSpec((tm, d_f), lambda i: (i, 0)),
                   pl.BlockSpec((tm, d_f), lambda i: (i, 0)),
                   pl.BlockSpec((tm, d_h), lambda i: (i, 0)),
                   pl.BlockSpec((tm, d_g), lambda i: (i, 0))],
        out_shape=[jax.ShapeDtypeStruct((T, d_f), F32), jax.ShapeDtypeStruct((T, d_f), F32),
                   jax.ShapeDtypeStruct((T, d_h), F32), jax.ShapeDtypeStruct((T, d_g), BF16)],
        compiler_params=_params(("parallel",)),
        name="inproj",
    )(xa, xb, sc, sh, w, b)


def _fourier_kernel(na, zr_ref, zi_ref, st1_ref, st2_ref, o_ref, zpr, zpi, s1, op):
    m = FFT_MINOR
    pz, p1 = _pitch(m), _pitch(2 * na)

    def fill(j, r0):
        _store_padded(zpr, j, zr_ref[0, pl.ds(r0, ROW_CHUNK), :])
        _store_padded(zpi, j, zi_ref[0, pl.ds(r0, ROW_CHUNK), :])

    _for_row_chunks(na * m, fill)

    def stage1(b, carry):
        z = jnp.concatenate([zpr[pl.ds(b, na, stride=pz), :], zpi[pl.ds(b, na, stride=pz), :]], axis=0)
        s1[pl.ds(pl.multiple_of(b * p1, SUBLANES), 2 * na), :] = _dot(st1_ref[b], z.astype(BF16))
        return carry

    lax.fori_loop(0, m, stage1, 0, unroll=FFT_UNROLL)

    po = _pitch(na)

    def stage2(ka, carry):
        a = jnp.concatenate([s1[pl.ds(ka, m, stride=p1), :], s1[pl.ds(na + ka, m, stride=p1), :]], axis=0)
        op[pl.ds(ka, m, stride=po), :] = _dot(st2_ref[...], a.astype(BF16))
        return carry

    lax.fori_loop(0, na, stage2, 0, unroll=FFT_UNROLL)

    def unpad(j, r0):
        o_ref[0, pl.ds(r0, ROW_CHUNK), :] = _load_padded(op, j, na)

    _for_row_chunks(na * m, unpad)


def _fourier(zr, zi):
    B, L, C = zr.shape
    m = FFT_MINOR
    na = L // m
    ct = FFT_COLS
    st1, st2 = _fourier_tables(L)
    spec = pl.BlockSpec((1, L, ct), lambda j, b: (b, 0, j))
    pad = pltpu.VMEM((na * _pitch(m), ct), F32)
    return pl.pallas_call(
        functools.partial(_fourier_kernel, na),
        grid=(C // ct, B),
        in_specs=[spec, spec, _const_spec((m, 2 * na, 2 * na)), _const_spec((m, 2 * m))],
        out_specs=spec,
        out_shape=jax.ShapeDtypeStruct((B, L, C), F32),
        scratch_shapes=[pad, pad, pltpu.VMEM((m * _pitch(2 * na), ct), F32),
                        pltpu.VMEM((m * _pitch(na), ct), F32)],
        compiler_params=_params(("parallel", "parallel")),
        name="fourier",
    )(zr, zi, _mxu_table(st1), _mxu_table(st2))


def _hyena_kernel(conv_z, L, zin_ref, gin_ref, cwz_ref, cbz_ref, cwg_ref, cbg_ref, skip_ref, kf_ref,
                  fwd1_ref, inv1_ref, st2f_ref, st2i_ref, sign_ref, cosb_ref, sinb_ref, out_ref, zp, yp, s1, g2, hh):
    m = FFT_MINOR
    nh = L // m
    rows1 = 2 * nh + SUBLANES
    pz, p1, p2 = _pitch(m), _pitch(rows1), _pitch(2 * m)

    def fill(j, r0):
        if conv_z:
            _store_padded(zp, j, _short_conv_chunk(zin_ref, cwz_ref, cbz_ref, L, j, r0))
        else:
            _store_padded(zp, j, zin_ref[0, pl.ds(r0, ROW_CHUNK), :])

    _for_row_chunks(L, fill)

    def stage1(b, carry):
        slab = zp[pl.ds(b, nh, stride=pz), :].astype(BF16)
        s1[pl.ds(pl.multiple_of(b * p1, SUBLANES), rows1), :] = _dot(fwd1_ref[b], slab)
        return carry

    lax.fori_loop(0, m, stage1, 0, unroll=FFT_UNROLL)

    def spectrum_row(ka, re_row, im_row):
        a = jnp.concatenate([s1[pl.ds(re_row, m, stride=p1), :], s1[pl.ds(im_row, m, stride=p1), :]], axis=0)
        x = _dot(st2f_ref[...], a.astype(BF16))
        xr, xi = x[:m], x[m:]
        k = kf_ref[ka]
        kr, ki = k[:m], k[m:]
        y = jnp.concatenate([xr * kr - xi * ki, xr * ki + xi * kr], axis=0).astype(BF16)
        return _dot(st2i_ref[...], y)

    def stage2(ka, carry):
        g2[pl.ds(pl.multiple_of(ka * p2, SUBLANES), 2 * m), :] = spectrum_row(ka, ka, nh + ka)
        return carry

    lax.fori_loop(0, nh, stage2, 0, unroll=FFT_UNROLL)
    g_mid = spectrum_row(nh, 2 * nh, 2 * nh + 1)
    hh[...] = cosb_ref[...] * g_mid[:m] - sinb_ref[...] * g_mid[m:]
    sign = sign_ref[...]

    def stage3(b, carry):
        g = jnp.concatenate([g2[pl.ds(b, nh, stride=p2), :], g2[pl.ds(m + b, nh, stride=p2), :]], axis=0)
        yp[pl.ds(b, nh, stride=pz), :] = _dot(inv1_ref[b], g.astype(BF16)) + sign * hh[pl.ds(b, 1), :]
        return carry

    lax.fori_loop(0, m, stage3, 0, unroll=FFT_UNROLL)
    skip = skip_ref[...]

    def finish(j, r0):
        gate = _short_conv_chunk(gin_ref, cwg_ref, cbg_ref, L, j, r0)
        out_ref[0, pl.ds(r0, ROW_CHUNK), :] = gate * (_load_padded(yp, j) + skip * _load_padded(zp, j))

    _for_row_chunks(L, finish)


def _hyena_order(order, zin, z_col0, uh, g_col0, conv_w, conv_b, skip, kf, L):
    B = uh.shape[0]
    m = FFT_MINOR
    ct = FFT_COLS
    ncol = D_HYENA // ct
    nh = L // m
    rows1 = 2 * nh + SUBLANES
    conv_z = order == 0
    fwd_half, inv_half, _, sign, cos_b, sin_b = _hyena_tables(L)
    st2f, st2i, _, _ = _stage2_tables()
    z_cols = (lambda j, b: (b, 0, z_col0 + j))
    g_cols = (lambda j, b: (b, 0, g_col0 + j))
    zw_col0 = z_col0 if conv_z else 0
    pad = pltpu.VMEM((nh * _pitch(m), ct), F32)
    return pl.pallas_call(
        functools.partial(_hyena_kernel, conv_z, L),
        grid=(ncol, B),
        in_specs=[pl.BlockSpec((1, L, ct), z_cols),
                  pl.BlockSpec((1, L, ct), g_cols),
                  pl.BlockSpec((3, ct), lambda j, b: (0, zw_col0 + j)),
                  pl.BlockSpec((1, ct), lambda j, b: (0, zw_col0 + j)),
                  pl.BlockSpec((3, ct), lambda j, b: (0, g_col0 + j)),
                  pl.BlockSpec((1, ct), lambda j, b: (0, g_col0 + j)),
                  pl.BlockSpec((1, ct), lambda j, b: (0, j)),
                  pl.BlockSpec((nh + 1, 2 * m, ct), lambda j, b: (0, 0, order * ncol + j),
                               pipeline_mode=pl.Buffered(1)),
                  _const_spec((m, rows1, nh)),
                  _const_spec((m, nh, 2 * nh)),
                  _const_spec((2 * m, 2 * m)),
                  _const_spec((2 * m, 2 * m)),
                  _const_spec((nh, LANES)), _const_spec((m, LANES)), _const_spec((m, LANES))],
        out_specs=pl.BlockSpec((1, L, ct), lambda j, b: (b, 0, j)),
        out_shape=jax.ShapeDtypeStruct((B, L, D_HYENA), F32),
        scratch_shapes=[pad, pad,
                        pltpu.VMEM((m * _pitch(rows1), ct), F32),
                        pltpu.VMEM((nh * _pitch(2 * m), ct), F32),
                        pltpu.VMEM((m, ct), F32)],
        compiler_params=_params(("parallel", "arbitrary")),
        name=f"hyena{order}",
    )(zin, uh, conv_w, conv_b, conv_w, conv_b, skip, kf,
      _mxu_table(fwd_half), _mxu_table(inv_half), _mxu_table(st2f), _mxu_table(st2i),
      jnp.asarray(sign, F32), jnp.asarray(cos_b, F32), jnp.asarray(sin_b, F32))


def _merge_kernel(n_first, f_ref, z_ref, sg_ref, xa_ref, xb_ref, gt_ref, sc_ref, sh_ref, wf_ref, wh_ref, wo_ref,
                  rw_ref, rb_ref, tri_ref, x1_ref, h2_ref, route_ref, rt_ref, cnt_ref):
    i = pl.program_id(0)
    tm, D = xa_ref.shape
    sub = MERGE_SUBTILE
    lane = lax.broadcasted_iota(jnp.int32, (sub, LANES), 1).astype(F32)
    neg = -1e30

    def first_max(v):
        mx = jnp.max(v, axis=1, keepdims=True)
        return mx, jnp.min(jnp.where(v == mx, lane, float(LANES)), axis=1, keepdims=True)

    @pl.when(i == 0)
    def _():
        cnt_ref[...] = jnp.zeros_like(cnt_ref)

    def sub_tile(r0, count):
        rows = pl.ds(r0, sub)
        ya = _dot(f_ref[rows, :].astype(BF16), wf_ref[...])
        yb = _dot(z_ref[rows, :].astype(BF16), wh_ref[...])
        merged = sg_ref[rows, :D].astype(F32) * ya + sg_ref[rows, D:].astype(F32) * yb
        x = jnp.where(i < n_first, xa_ref[rows, :], xb_ref[rows, :])
        x1 = x + gt_ref[0] * _dot(merged.astype(BF16), wo_ref[...])
        x1_ref[rows, :] = x1
        h2 = _rms_mod(x1, sc_ref[0], sh_ref[0])
        _to_token_tiles(h2_ref, h2, r0 * SUBLANES)

        logits = _dot_hi(h2, rw_ref[...]) + rb_ref[...]
        gl = jnp.where(lane < N_GROUPS, logits, neg)
        gmax, g = first_max(gl)
        p_g = 1.0 / jnp.sum(jnp.exp(gl - gmax), axis=1, keepdims=True)
        lo = N_GROUPS + EXPERTS_PER_GROUP * g
        el = jnp.where((lane >= lo) & (lane < lo + EXPERTS_PER_GROUP), logits, neg)
        m1, i1 = first_max(el)
        m2, i2 = first_max(jnp.where(lane == i1, neg, el))
        r = jnp.exp(m2 - m1)
        wt1 = p_g / (1.0 + r)
        wt2 = p_g * r / (1.0 + r)
        e1 = i1 - N_GROUPS
        e2 = i2 - N_GROUPS

        onehot = ((lane == e1) | (lane == e2)).astype(BF16)
        before = _dot(tri_ref[...], onehot) + count
        r1 = jnp.sum(jnp.where(lane == e1, before, 0.0), axis=1, keepdims=True)
        r2 = jnp.sum(jnp.where(lane == e2, before, 0.0), axis=1, keepdims=True)

        packed = jnp.zeros((sub, LANES), F32)
        for slot, v in enumerate((e1, e2, wt1, wt2, r1, r2)):
            packed = jnp.where(lane == slot, v, packed)
        route_ref[rows, :] = packed
        rt_ref[:, rows] = packed.T[0:SUBLANES, :]
        return count + jnp.sum(onehot.astype(F32), axis=0, keepdims=True)

    count = cnt_ref[...]
    for h in range(tm // sub):
        count = sub_tile(h * sub, count)
    cnt_ref[...] = count


def _merge(f2d, z2d, sg, xa, xb, gt1, sc2, sh2, w_four, w_hyena, w_out, rw, rb, L):
    D = xa.shape[1]
    T = xa.shape[0] + xb.shape[0]
    assert D == SUBLANES * LANES
    tm = TOKEN_TILE
    per_b = L // tm
    n_first = xa.shape[0] // tm
    d_f = f2d.shape[1]
    d_h = z2d.shape[1]
    sub = MERGE_SUBTILE
    tri = jnp.asarray(np.tril(np.ones((sub, sub)), -1), BF16)
    mod_spec = pl.BlockSpec((1, 1, D), lambda i: (i // per_b, 0, 0))
    row = lambda w: pl.BlockSpec((tm, w), lambda i: (i, 0))
    return pl.pallas_call(
        functools.partial(_merge_kernel, n_first),
        grid=(T // tm,),
        in_specs=[row(d_f), row(d_h), row(2 * D)] + _group_specs(n_first, tm, D) + [
            mod_spec, mod_spec, mod_spec,
            _const_spec((d_f, D)), _const_spec((d_h, D)), _const_spec((D, D)),
            _const_spec((D, LANES)), _const_spec((1, LANES)), _const_spec((sub, sub))],
        out_specs=[row(D), pl.BlockSpec((tm * SUBLANES, LANES), lambda i: (i, 0)), row(LANES),
                   pl.BlockSpec((SUBLANES, tm), lambda i: (0, i)), pl.BlockSpec((1, LANES), lambda i: (0, 0))],
        out_shape=[jax.ShapeDtypeStruct((T, D), F32), jax.ShapeDtypeStruct((T * SUBLANES, LANES), F32),
                   jax.ShapeDtypeStruct((T, LANES), F32), jax.ShapeDtypeStruct((SUBLANES, T), F32),
                   jax.ShapeDtypeStruct((1, LANES), F32)],
        compiler_params=_params(("arbitrary",)),
        name="merge",
    )(f2d, z2d, sg, xa, xb, gt1, sc2, sh2, w_four, w_hyena, w_out, rw, rb, tri)


def _to_token_tiles(ref, val, row0=0):
    n = val.shape[0]
    for s in range(SUBLANES):
        ref[pl.ds(row0 + s, n, stride=SUBLANES), :] = val[:, s * LANES:(s + 1) * LANES]


def _from_token_tiles(ref, row0, n):
    return jnp.concatenate([ref[pl.ds(row0 + s, n, stride=SUBLANES), :] for s in range(SUBLANES)], axis=1)


def _gather_start(idx_ref, src_hbm, dst, row0, n_tok, sem):
    def issue(r, carry):
        src = pl.multiple_of(idx_ref[0, 0, r] * SUBLANES, SUBLANES)
        row = pl.multiple_of(row0 + r * SUBLANES, SUBLANES)
        pltpu.make_async_copy(src_hbm.at[pl.ds(src, SUBLANES)], dst.at[pl.ds(row, SUBLANES)], sem).start()
        return carry

    lax.fori_loop(0, n_tok, issue, 0, unroll=8)


def _gather_wait(src_hbm, dst, row0, n_tok, sem):
    rows = n_tok * SUBLANES
    pltpu.make_async_copy(src_hbm.at[pl.ds(0, rows)], dst.at[pl.ds(row0, rows)], sem).wait()


def _dispatch_kernel(pend_ref, dest_ref, h_ref, x_hbm, zero, sem, zsem):
    i = pl.program_id(0)
    tm = dest_ref.shape[2] // TOP_K
    bm = MOE_BLOCK

    def token_rows(ref, tok):
        return ref.at[pl.ds(pl.multiple_of(tok * SUBLANES, SUBLANES), SUBLANES)]

    def drain(n_tok, s):
        rows = n_tok * SUBLANES
        pltpu.make_async_copy(zero.at[pl.ds(0, rows)], x_hbm.at[pl.ds(0, rows)], s).wait()

    @pl.when(i == 0)
    def _():
        zero[...] = jnp.zeros_like(zero)
        n_blocks = x_hbm.shape[0] // (bm * SUBLANES)
        used = pend_ref[N_EXPERTS - 1] // bm

        def zero_block(first_tok):
            row = pl.multiple_of(first_tok * SUBLANES, bm * SUBLANES)
            pltpu.make_async_copy(zero, x_hbm.at[pl.ds(row, bm * SUBLANES)], zsem).start()

        def per_expert(e, carry):
            zero_block(jnp.maximum(pend_ref[e] - bm, 0))
            return carry

        def per_unused(blk, carry):
            zero_block(blk * bm)
            return carry

        def drain_block(blk, carry):
            drain(bm, zsem)
            return carry

        lax.fori_loop(0, N_EXPERTS, per_expert, 0)
        lax.fori_loop(used, n_blocks, per_unused, 0)
        lax.fori_loop(0, N_EXPERTS + n_blocks - used, drain_block, 0)

    for k in range(TOP_K):
        def issue(r, carry, k=k):
            dst = token_rows(x_hbm, dest_ref[0, 0, k * tm + r])
            pltpu.make_async_copy(token_rows(h_ref, r), dst, sem).start()
            return carry

        lax.fori_loop(0, tm, issue, 0, unroll=8)

    for k in range(TOP_K):
        pltpu.make_async_copy(h_ref, x_hbm.at[pl.ds(0, tm * SUBLANES)], sem).wait()


def _dispatch(pend, dest_tiles, h2, n_rows):
    n, _, tm2 = dest_tiles.shape
    tm = tm2 // TOP_K
    grid_spec = pltpu.PrefetchScalarGridSpec(
        num_scalar_prefetch=1,
        grid=(n,),
        in_specs=[pl.BlockSpec((1, 1, tm2), lambda i, pe: (i, 0, 0), memory_space=pltpu.SMEM),
                  pl.BlockSpec((tm * SUBLANES, LANES), lambda i, pe: (i, 0))],
        out_specs=pl.BlockSpec(memory_space=pl.ANY),
        scratch_shapes=[pltpu.VMEM((MOE_BLOCK * SUBLANES, LANES), F32), pltpu.SemaphoreType.DMA(()),
                        pltpu.SemaphoreType.DMA(())],
    )
    return pl.pallas_call(
        _dispatch_kernel,
        grid_spec=grid_spec,
        out_shape=jax.ShapeDtypeStruct((n_rows * SUBLANES, LANES), F32),
        compiler_params=_params(("arbitrary",)),
        name="dispatch",
    )(pend, dest_tiles, h2)


def _expert_kernel(blk_e_ref, nused_ref, x_ref, wg_ref, wu_ref, wd_ref, y_ref, wg16, wu16, wd16):
    i = pl.program_id(0)
    nused = nused_ref[0]
    bm = MOE_BLOCK

    @pl.when((i == 0) | (blk_e_ref[i] != blk_e_ref[jnp.maximum(i - 1, 0)]))
    def _():
        wg16[...] = wg_ref[0].astype(BF16)
        wu16[...] = wu_ref[0].astype(BF16)
        wd16[...] = wd_ref[0].astype(BF16)

    @pl.when(i < nused)
    def _():
        x = _from_token_tiles(x_ref, 0, bm).astype(BF16)
        g = _dot(x, wg16[...])
        u = _dot(x, wu16[...])
        a = (g * jax.nn.sigmoid(g) * u).astype(BF16)
        _to_token_tiles(y_ref, _dot(a, wd16[...]))

    @pl.when(i >= nused)
    def _():
        y_ref[...] = jnp.zeros_like(y_ref)


def _experts(blk_e, nused, xb, wg, wu, wd):
    nb = blk_e.shape[0]
    bm = MOE_BLOCK
    D, de = wg.shape[1:]
    assert D == SUBLANES * LANES and xb.shape == (nb * bm * SUBLANES, LANES)
    grid_spec = pltpu.PrefetchScalarGridSpec(
        num_scalar_prefetch=2,
        grid=(nb,),
        in_specs=[pl.BlockSpec((bm * SUBLANES, LANES), lambda i, be, nu: (jnp.minimum(i, nu[0] - 1), 0)),
                  pl.BlockSpec((1, D, de), lambda i, be, nu: (be[i], 0, 0)),
                  pl.BlockSpec((1, D, de), lambda i, be, nu: (be[i], 0, 0)),
                  pl.BlockSpec((1, de, D), lambda i, be, nu: (be[i], 0, 0))],
        out_specs=pl.BlockSpec((bm * SUBLANES, LANES), lambda i, be, nu: (i, 0)),
        scratch_shapes=[pltpu.VMEM((D, de), BF16), pltpu.VMEM((D, de), BF16), pltpu.VMEM((de, D), BF16)],
    )
    return pl.pallas_call(
        _expert_kernel,
        grid_spec=grid_spec,
        out_shape=jax.ShapeDtypeStruct((nb * bm * SUBLANES, LANES), F32),
        compiler_params=_params(("arbitrary",)),
        name="experts",
    )(blk_e, nused, xb, wg, wu, wd)


def _combine_kernel(n_steps, cur_ref, nxt_ref, y_hbm, x1_ref, route_ref, gt_ref, gf_ref, o_ref, buf, sem):
    i = pl.program_id(0)
    tm = x1_ref.shape[0]
    slot = i % 2
    rows = 2 * tm * SUBLANES

    @pl.when(i == 0)
    def _():
        _gather_start(cur_ref, y_hbm, buf, 0, 2 * tm, sem.at[0])

    @pl.when(i + 1 < n_steps)
    def _():
        _gather_start(nxt_ref, y_hbm, buf, (1 - slot) * rows, 2 * tm, sem.at[1 - slot])

    _gather_wait(y_hbm, buf, slot * rows, 2 * tm, sem.at[slot])
    route = route_ref[...]
    y1 = _from_token_tiles(buf, slot * rows, tm)
    y2 = _from_token_tiles(buf, slot * rows + tm * SUBLANES, tm)
    x = x1_ref[...] + gt_ref[0] * (route[:, 2:3] * y1 + route[:, 3:4] * y2)
    o_ref[...] = x * lax.rsqrt(jnp.mean(x * x, axis=-1, keepdims=True) + EPS) * gf_ref[...]


def _combine(tile0, n_tok, dest_tiles, yb, x1, route, gt2, g_final, L):
    T, D = x1.shape
    tm = TOKEN_TILE
    per_b = L // tm
    n = n_tok // tm
    last = tile0 + n - 1
    smem = lambda f: pl.BlockSpec((1, 1, 2 * tm), lambda i: (f(i), 0, 0), memory_space=pltpu.SMEM)
    return pl.pallas_call(
        functools.partial(_combine_kernel, n),
        grid=(n,),
        in_specs=[smem(lambda i: tile0 + i), smem(lambda i: jnp.minimum(tile0 + i + 1, last)),
                  pl.BlockSpec(memory_space=pl.ANY),
                  pl.BlockSpec((tm, D), lambda i: (tile0 + i, 0)),
                  pl.BlockSpec((tm, LANES), lambda i: (tile0 + i, 0)),
                  pl.BlockSpec((1, 1, D), lambda i: ((tile0 + i) // per_b, 0, 0)),
                  pl.BlockSpec((1, D), lambda i: (0, 0))],
        out_specs=pl.BlockSpec((tm, D), lambda i: (i, 0)),
        out_shape=jax.ShapeDtypeStruct((n_tok, D), F32),
        scratch_shapes=[pltpu.VMEM((2 * 2 * tm * SUBLANES, LANES), F32), pltpu.SemaphoreType.DMA((2,))],
        compiler_params=_params(("arbitrary",)),
        name="combine",
    )(dest_tiles, dest_tiles, yb, x1, route, gt2, g_final.reshape(1, D))


def _encoder(xp, xs, c, p, g_final):
    n_prompt, L, D = xp.shape
    B = n_prompt + xs.shape[0]
    T = B * L
    xa, xb = xp.reshape(-1, D), xs.reshape(-1, D)
    d_f = N_FOURIER_GROUPS * FOURIER_GROUP_DIM
    d_h = (HYENA_ORDER + 1) * D_HYENA
    d_g = 2 * D

    mod = _ada(c, p["w_ada"], p["b_ada"])
    sh1, sc1, gt1, sh2, sc2, gt2 = [mod[:, k * D:(k + 1) * D].reshape(B, 1, D) for k in range(6)]

    wb = jnp.concatenate([p["w_in"][:, :d_f], jnp.broadcast_to(p["b_in"][None, :d_f], (8, d_f))], axis=0)
    fre, fim = _fold_channel_dft(wb)
    w_all = jnp.concatenate([fre[:D], fim[:D], p["w_in"][:, d_f:]], axis=1).astype(BF16)
    b_all = jnp.concatenate([fre[D], fim[D], p["b_in"][d_f:]])[None, :]
    zr, zi, uh, sg = _inproj(xa, xb, sc1, sh1, w_all, b_all, L, d_f, d_h, d_g)

    f = _fourier(zr.reshape(B, L, d_f), zi.reshape(B, L, d_f))

    ktime, ksum = _filt_time(L, p["filt_w1"], p["filt_b1"], p["filt_w2"], p["filt_b2"],
                             p["filt_w3"], p["filt_b3"], p["filt_freq"], p["filt_wout"])
    kf = _filt_fft(L, ktime, ksum)
    uh3 = uh.reshape(B, L, d_h)
    conv_b = p["conv_b"][None, :]
    ncol = D_HYENA // FFT_COLS
    skip = p["hyena_skip"]
    z = _hyena_order(0, uh3, 0, uh3, ncol, p["conv_w"], conv_b, skip[0:1], kf, L)
    z = _hyena_order(1, z, 0, uh3, 2 * ncol, p["conv_w"], conv_b, skip[1:2], kf, L)

    rw = jnp.zeros((D, LANES), F32).at[:, :N_GROUPS].set(p["router_w1"])
    rw = rw.at[:, N_GROUPS:N_GROUPS + N_EXPERTS].set(p["router_w2"])
    rb = jnp.zeros((1, LANES), F32).at[0, :N_GROUPS].set(p["router_b1"])
    rb = rb.at[0, N_GROUPS:N_GROUPS + N_EXPERTS].set(p["router_b2"])
    x1, h2, route, route_t, counts = _merge(f.reshape(T, d_f), z.reshape(T, D_HYENA), sg, xa, xb,
                                            gt1, sc2, sh2, p["w_four"].astype(BF16), p["w_hyena"].astype(BF16),
                                            p["w_out"].astype(BF16), rw, rb, L)

    bm = MOE_BLOCK
    tm = TOKEN_TILE
    nb = (T * TOP_K) // bm + N_EXPERTS
    e = route_t[0:2].astype(jnp.int32)
    rank = route_t[4:6].astype(jnp.int32)
    cnt = counts[0, :N_EXPERTS].astype(jnp.int32)
    pcnt = (cnt + bm - 1) // bm * bm
    pend = jnp.cumsum(pcnt)
    experts = jnp.arange(N_EXPERTS, dtype=jnp.int32)
    dest = rank + jnp.sum(jnp.where(e[..., None] == experts, pend - pcnt, 0), axis=-1)
    blk_row0 = jnp.arange(nb, dtype=jnp.int32)[:, None] * bm
    blk_e = jnp.minimum(jnp.sum((pend[None, :] <= blk_row0).astype(jnp.int32), axis=1), N_EXPERTS - 1)
    nused = (pend[-1] // bm).astype(jnp.int32).reshape(1)
    dest_tiles = dest.reshape(TOP_K, T // tm, tm).transpose(1, 0, 2).reshape(T // tm, 1, TOP_K * tm)
    xb = _dispatch(pend.astype(jnp.int32), dest_tiles, h2, nb * bm)
    yb = _experts(blk_e, nused, xb, p["exp_w_gate"], p["exp_w_up"], p["exp_w_down"])

    t_prompt = n_prompt * L
    outs = []
    for tile0, n_tok in ((0, t_prompt), (t_prompt // tm, T - t_prompt)):
        outs.append(_combine(tile0, n_tok, dest_tiles, yb, x1, route, gt2, g_final, L))
    return outs[0].reshape(xp.shape), outs[1].reshape(xs.shape)


def kernel(x_prompt, x_sample, c_prompt, c_sample, w_ada, b_ada, w_in, b_in, conv_w, conv_b, filt_w1, filt_b1, filt_w2, filt_b2, filt_w3, filt_b3, filt_freq, filt_wout, hyena_skip, w_four, w_hyena, w_out, router_w1, router_b1, router_w2, router_b2, exp_w_gate, exp_w_up, exp_w_down, g_final):
    assert w_ada.shape[0] == 1, "single-layer block"
    assert x_prompt.shape[1:] == x_sample.shape[1:], "both request groups share sequence length and width"
    p = dict(w_ada=w_ada[0], b_ada=b_ada[0], w_in=w_in[0], b_in=b_in[0], conv_w=conv_w[0], conv_b=conv_b[0],
             filt_w1=filt_w1[0], filt_b1=filt_b1[0], filt_w2=filt_w2[0], filt_b2=filt_b2[0],
             filt_w3=filt_w3[0], filt_b3=filt_b3[0], filt_freq=filt_freq[0], filt_wout=filt_wout[0],
             hyena_skip=hyena_skip[0], w_four=w_four[0], w_hyena=w_hyena[0], w_out=w_out[0],
             router_w1=router_w1[0], router_b1=router_b1[0], router_w2=router_w2[0], router_b2=router_b2[0],
             exp_w_gate=exp_w_gate[0], exp_w_up=exp_w_up[0], exp_w_down=exp_w_down[0])
    c = jnp.concatenate([c_prompt, c_sample], axis=0)
    return _encoder(x_prompt, x_sample, c, p, g_final)
```

```python
import functools
import math

import numpy as np
import jax
import jax.numpy as jnp
from jax import lax
from jax.experimental import pallas as pl
from jax.experimental.pallas import tpu as pltpu

F32 = jnp.float32
BF16 = jnp.bfloat16
HI = lax.Precision.HIGHEST

EPS = 1e-6
FFT_MINOR = 64
N_FOURIER_GROUPS = 4
FOURIER_GROUP_DIM = 128
D_HYENA = 512
HYENA_ORDER = 2
POS_BANDS = 16
N_GROUPS = 4
EXPERTS_PER_GROUP = 8
N_EXPERTS = 32
TOP_K = 2
SHORT_DECAY_PCT = 0.3
LONG_DECAY_PCT = 1.5
DECAY_TARGET = 1e-2
LANES = 128
SUBLANES = 8
VMEM_LIMIT = 56 * 1024 * 1024

TOKEN_TILE = 512
MERGE_SUBTILE = 128
MOE_BLOCK = 512
FFT_COLS = LANES
ROW_CHUNK = 256
FFT_UNROLL = 16
DMA_GROUP = 8


def _dot(a, b):
    return jnp.dot(a, b, preferred_element_type=F32)


def _dot_hi(a, b):
    return jnp.dot(a, b, preferred_element_type=F32, precision=HI)


def _split_bf16(x):
    hi = x.astype(BF16)
    return hi, (x - hi.astype(F32)).astype(BF16)


def _params(sem=None):
    return pltpu.CompilerParams(dimension_semantics=sem, vmem_limit_bytes=VMEM_LIMIT)


def _const_spec(shape):
    nd = len(shape)
    return pl.BlockSpec(shape, lambda *_: (0,) * nd, pipeline_mode=pl.Buffered(1))


def _pitch(rows):
    p = -(-rows // SUBLANES)
    return SUBLANES * (p if p % 2 else p + 1)


@functools.lru_cache(maxsize=None)
def _stage2_tables():
    i = np.arange(FFT_MINOR)
    ph = 2.0 * np.pi * np.outer(i, i) / FFT_MINOR
    c, s = np.cos(ph), np.sin(ph)
    fwd = np.block([[c, s], [-s, c]])
    inv = np.block([[c, -s], [s, c]])
    return fwd, inv, c, s


@functools.lru_cache(maxsize=None)
def _hyena_tables(L):
    n_fft = 2 * L
    m = FFT_MINOR
    na = n_fft // m
    nh = na // 2
    b = np.arange(m)[:, None, None]
    ka = np.arange(nh + 1)[None, :, None]

    def forward(a_count):
        a = np.arange(a_count)[None, None, :]
        th = 2.0 * np.pi * ((ka * (m * a + b)) % n_fft) / n_fft
        c, s = np.cos(th), -np.sin(th)
        pad = np.zeros((m, SUBLANES - 2, a_count))
        return np.concatenate([c[:, :nh], s[:, :nh], c[:, nh:], s[:, nh:], pad], axis=1)

    fwd_half = forward(nh)
    fwd_full = forward(na)
    weight = np.where(np.arange(nh) == 0, 1.0, 2.0)[None, :, None]
    inv_half = np.transpose(fwd_half[:, :2 * nh] * np.concatenate([weight, weight], axis=1), (0, 2, 1)) / n_fft
    sign = np.broadcast_to(((-1.0) ** np.arange(nh))[:, None] / n_fft, (nh, LANES))
    ph = np.pi * np.arange(m) / m
    cos_b = np.broadcast_to(np.cos(ph)[:, None], (m, LANES))
    sin_b = np.broadcast_to(np.sin(ph)[:, None], (m, LANES))
    return fwd_half, inv_half, fwd_full, sign, cos_b, sin_b


@functools.lru_cache(maxsize=None)
def _fourier_tables(L):
    na = L // FFT_MINOR
    b = np.arange(FFT_MINOR)[:, None, None]
    ka = np.arange(na)[None, :, None]
    a = np.arange(na)[None, None, :]
    th = 2.0 * np.pi * ((ka * (FFT_MINOR * a + b)) % L) / L
    c, s = np.cos(th), np.sin(th)
    st1 = np.concatenate([np.concatenate([c, s], axis=2),
                          np.concatenate([-s, c], axis=2)], axis=1)
    _, _, c2, s2 = _stage2_tables()
    st2 = np.concatenate([c2, s2], axis=1) / math.sqrt(L)
    return st1, st2


def _mxu_table(table):
    return jnp.asarray(table, F32).astype(BF16)


@functools.lru_cache(maxsize=None)
def _channel_dft():
    i = np.arange(FOURIER_GROUP_DIM)
    ph = 2.0 * np.pi * np.outer(i, i) / FOURIER_GROUP_DIM
    return np.concatenate([np.cos(ph), -np.sin(ph)], axis=1) / math.sqrt(FOURIER_GROUP_DIM)


def _ada_kernel(c_ref, w_ref, b_ref, o_ref):
    c = c_ref[...]
    o_ref[...] = _dot_hi(c * jax.nn.sigmoid(c), w_ref[...]) + b_ref[...]


def _ada(c, w_ada, b_ada):
    nb, d = c.shape
    n = w_ada.shape[1]
    tn = 1536
    return pl.pallas_call(
        _ada_kernel,
        grid=(n // tn,),
        in_specs=[pl.BlockSpec((nb, d), lambda j: (0, 0)),
                  pl.BlockSpec((d, tn), lambda j: (0, j)),
                  pl.BlockSpec((1, tn), lambda j: (0, j))],
        out_specs=pl.BlockSpec((nb, tn), lambda j: (0, j)),
        out_shape=jax.ShapeDtypeStruct((nb, n), F32),
        compiler_params=_params(("arbitrary",)),
        name="ada",
    )(c, w_ada, b_ada.reshape(1, n))


def _fold_kernel(w_ref, f_ref, re_ref, im_ref):
    z = _dot_hi(w_ref[...], f_ref[...])
    re_ref[...] = z[:, :FOURIER_GROUP_DIM]
    im_ref[...] = z[:, FOURIER_GROUP_DIM:]


def _fold_channel_dft(wb):
    rows = wb.shape[0]
    gd = FOURIER_GROUP_DIM
    fmat = jnp.asarray(_channel_dft(), F32)
    return pl.pallas_call(
        _fold_kernel,
        grid=(N_FOURIER_GROUPS,),
        in_specs=[pl.BlockSpec((rows, gd), lambda g: (0, g)),
                  pl.BlockSpec((gd, 2 * gd), lambda g: (0, 0))],
        out_specs=[pl.BlockSpec((rows, gd), lambda g: (0, g)),
                   pl.BlockSpec((rows, gd), lambda g: (0, g))],
        out_shape=[jax.ShapeDtypeStruct((rows, N_FOURIER_GROUPS * gd), F32)] * 2,
        compiler_params=_params(("arbitrary",)),
        name="fold",
    )(wb, fmat)


def _for_row_chunks(n_rows, body):
    def step(j, carry):
        body(j, pl.multiple_of(j * ROW_CHUNK, ROW_CHUNK))
        return carry

    lax.fori_loop(0, n_rows // ROW_CHUNK, step, 0)


def _store_padded(dst, j, val, run=FFT_MINOR):
    pz = _pitch(run)
    per = ROW_CHUNK // run
    for i in range(per):
        dst[pl.ds(pl.multiple_of((j * per + i) * pz, SUBLANES), run), :] = val[i * run:(i + 1) * run]


def _load_padded(src, j, run=FFT_MINOR):
    pz = _pitch(run)
    per = ROW_CHUNK // run
    return jnp.concatenate([src[pl.ds(pl.multiple_of((j * per + i) * pz, SUBLANES), run), :] for i in range(per)],
                           axis=0)


def _short_conv_chunk(src_ref, w_ref, b_ref, L, j, r0):
    R = ROW_CHUNK
    ct = src_ref.shape[-1]
    row = lax.broadcasted_iota(jnp.int32, (R, ct), 0)
    cur = src_ref[0, pl.ds(r0, R), :]
    before = src_ref[0, pl.ds(jnp.maximum(r0 - 1, 0), 1), :] * jnp.where(j > 0, 1.0, 0.0)
    after = src_ref[0, pl.ds(jnp.minimum(r0 + R, L - 1), 1), :] * jnp.where(j < L // R - 1, 1.0, 0.0)
    up = jnp.where(row == 0, before, pltpu.roll(cur, 1, 0))
    dn = jnp.where(row == R - 1, after, pltpu.roll(cur, R - 1, 0))
    return w_ref[0:1, :] * up + w_ref[1:2, :] * cur + w_ref[2:3, :] * dn + b_ref[...]


def _filt_time_kernel(L, rows, bands_ref, delt_ref, w1_ref, b1_ref, w2_ref, b2_ref, w3_ref, b3_ref,
                      fr_ref, wo_ref, k_ref, sum_ref):
    i = pl.program_id(0)
    n = i * rows + lax.broadcasted_iota(jnp.int32, (rows, 1), 0)
    fwd = n < L
    pos = jnp.where(fwd, n, 2 * L - n).astype(F32)
    t = pos * (1.0 / (L - 1))
    ang = (2.0 * math.pi / L) * pos * bands_ref[...]
    w1 = w1_ref[...]
    pre = (t * w1[0:1, :] + _dot_hi(jnp.cos(ang), w1[1:1 + POS_BANDS, :])
           + _dot_hi(-jnp.sin(ang), w1[1 + POS_BANDS:, :]) + b1_ref[...])
    fr = fr_ref[...]
    h = jnp.sin(fr[0:1, :] * pre)
    h = jnp.sin(fr[1:2, :] * (_dot_hi(h, w2_ref[...]) + b2_ref[...]))
    h = jnp.sin(fr[2:3, :] * (_dot_hi(h, w3_ref[...]) + b3_ref[...]))
    window = jnp.exp(-t * delt_ref[...])
    live = n != L
    parts = []
    for o in range(HYENA_ORDER):
        base = o * 2 * D_HYENA
        hf = _dot_hi(h, wo_ref[:, base:base + D_HYENA])
        hb = _dot_hi(h, wo_ref[:, base + D_HYENA:base + 2 * D_HYENA])
        parts.append(jnp.where(live, jnp.where(fwd, hf, hb) * window, 0.0))
    k = jnp.concatenate(parts, axis=1)
    k_ref[...] = k

    @pl.when(i == 0)
    def _():
        sum_ref[...] = jnp.zeros_like(sum_ref)

    sum_ref[...] += jnp.sum(jnp.abs(k), axis=0, keepdims=True)


def _filt_time(L, w1, b1, w2, b2, w3, b3, freq, wout):
    rows = 512
    n_fft = 2 * L
    cols = HYENA_ORDER * D_HYENA
    bands = jnp.linspace(1e-4, POS_BANDS - 1, POS_BANDS, dtype=F32).reshape(1, POS_BANDS)
    max_decay = math.log(DECAY_TARGET) / SHORT_DECAY_PCT
    min_decay = math.log(DECAY_TARGET) / LONG_DECAY_PCT
    deltas = jnp.abs(jnp.linspace(min_decay, max_decay, D_HYENA, dtype=F32)).reshape(1, D_HYENA)
    args = (bands, deltas, w1, b1.reshape(1, -1), w2, b2.reshape(1, -1), w3, b3.reshape(1, -1), freq, wout)
    return pl.pallas_call(
        functools.partial(_filt_time_kernel, L, rows),
        grid=(n_fft // rows,),
        in_specs=[pl.BlockSpec(a.shape, lambda i: (0, 0)) for a in args],
        out_specs=[pl.BlockSpec((rows, cols), lambda i: (i, 0)),
                   pl.BlockSpec((1, cols), lambda i: (0, 0))],
        out_shape=[jax.ShapeDtypeStruct((n_fft, cols), F32),
                   jax.ShapeDtypeStruct((1, cols), F32)],
        compiler_params=_params(("arbitrary",)),
        name="filt_time",
    )(*args)


def _filt_fft_kernel(na, k_ref, sum_ref, st1_ref, st2_ref, o_ref, kp, s1):
    m = FFT_MINOR
    nh = na // 2
    rows1 = 2 * nh + SUBLANES
    pz, p1 = _pitch(m), _pitch(rows1)
    _for_row_chunks(na * m, lambda j, r0: _store_padded(kp, j, k_ref[pl.ds(r0, ROW_CHUNK), :]))

    def stage1(b, carry):
        slab = kp[pl.ds(b, na, stride=pz), :]
        s1[pl.ds(pl.multiple_of(b * p1, SUBLANES), rows1), :] = _dot_hi(st1_ref[b], slab)
        return carry

    lax.fori_loop(0, m, stage1, 0, unroll=4)
    inv_norm = 1.0 / sum_ref[...]

    def spectrum_row(re_row, im_row):
        a = jnp.concatenate([s1[pl.ds(re_row, m, stride=p1), :], s1[pl.ds(im_row, m, stride=p1), :]], axis=0)
        return _dot_hi(st2_ref[...], a) * inv_norm

    def stage2(ka, carry):
        o_ref[ka] = spectrum_row(ka, nh + ka)
        return carry

    lax.fori_loop(0, nh, stage2, 0, unroll=4)
    o_ref[nh] = spectrum_row(2 * nh, 2 * nh + 1)


def _filt_fft(L, ktime, ksum):
    n_fft, cols = ktime.shape
    m = FFT_MINOR
    na = n_fft // m
    nh = na // 2
    rows1 = 2 * nh + SUBLANES
    ct = FFT_COLS
    fwd_full = _hyena_tables(L)[2]
    st2f = _stage2_tables()[0]
    return pl.pallas_call(
        functools.partial(_filt_fft_kernel, na),
        grid=(cols // ct,),
        in_specs=[pl.BlockSpec((n_fft, ct), lambda j: (0, j)),
                  pl.BlockSpec((1, ct), lambda j: (0, j)),
                  _const_spec((m, rows1, na)),
                  _const_spec((2 * m, 2 * m))],
        out_specs=pl.BlockSpec((nh + 1, 2 * m, ct), lambda j: (0, 0, j)),
        out_shape=jax.ShapeDtypeStruct((nh + 1, 2 * m, cols), F32),
        scratch_shapes=[pltpu.VMEM((na * _pitch(m), ct), F32),
                        pltpu.VMEM((m * _pitch(rows1), ct), F32)],
        compiler_params=_params(("arbitrary",)),
        name="filt_fft",
    )(ktime, ksum, jnp.asarray(fwd_full, F32), jnp.asarray(st2f, F32))


def _rms_mod(x, scale, shift):
    y = x * lax.rsqrt(jnp.mean(x * x, axis=-1, keepdims=True) + EPS)
    return y * (1.0 + scale) + shift


def _group_tile(n_first, a_ref, b_ref):
    return jnp.where(pl.program_id(0) < n_first, a_ref[...], b_ref[...])


def _group_specs(n_first, tm, D):
    return [pl.BlockSpec((tm, D), lambda i: (jnp.minimum(i, n_first - 1), 0)),
            pl.BlockSpec((tm, D), lambda i: (jnp.maximum(i - n_first, 0), 0))]


def _inproj_kernel(d_f, d_h, n_first, xa_ref, xb_ref, sc_ref, sh_ref, w_ref, b_ref, zr_ref, zi_ref, uh_ref, sg_ref):
    h = _rms_mod(_group_tile(n_first, xa_ref, xb_ref), sc_ref[0], sh_ref[0]).astype(BF16)

    def proj(c0, width):
        return _dot(h, w_ref[:, c0:c0 + width]) + b_ref[:, c0:c0 + width]

    zr_ref[...] = proj(0, d_f)
    zi_ref[...] = proj(d_f, d_f)
    chunk = 512
    for c in range(0, d_h, chunk):
        uh_ref[:, c:c + chunk] = proj(2 * d_f + c, chunk)
    d_g = sg_ref.shape[1]
    for c in range(0, d_g, chunk):
        sg_ref[:, c:c + chunk] = jax.nn.sigmoid(proj(2 * d_f + d_h + c, chunk)).astype(BF16)


def _inproj(xa, xb, sc, sh, w, b, L, d_f, d_h, d_g):
    D = xa.shape[1]
    T = xa.shape[0] + xb.shape[0]
    tm = TOKEN_TILE
    per_b = L // tm
    n_first = xa.shape[0] // tm
    n = w.shape[1]
    mod_spec = pl.BlockSpec((1, 1, D), lambda i: (i // per_b, 0, 0))
    return pl.pallas_call(
        functools.partial(_inproj_kernel, d_f, d_h, n_first),
        grid=(T // tm,),
        in_specs=_group_specs(n_first, tm, D) + [mod_spec, mod_spec, _const_spec((D, n)), _const_spec((1, n))],
        out_specs=[pl.BlockSpec((tm, d_f), lambda i: (i, 0)),
                   pl.BlockSpec((tm, d_f), lambda i: (i, 0)),
                   pl.BlockSpec((tm, d_h), lambda i: (i, 0)),
                   pl.BlockSpec((tm, d_g), lambda i: (i, 0))],
        out_shape=[jax.ShapeDtypeStruct((T, d_f), F32), jax.ShapeDtypeStruct((T, d_f), F32),
                   jax.ShapeDtypeStruct((T, d_h), F32), jax.ShapeDtypeStruct((T, d_g), BF16)],
        compiler_params=_params(("parallel",)),
        name="inproj",
    )(xa, xb, sc, sh, w, b)


def _fourier_kernel(na, zr_ref, zi_ref, st1_ref, st2_ref, o_ref, zpr, zpi, s1, op):
    m = FFT_MINOR
    pz, p1 = _pitch(m), _pitch(2 * na)

    def fill(j, r0):
        _store_padded(zpr, j, zr_ref[0, pl.ds(r0, ROW_CHUNK), :])
        _store_padded(zpi, j, zi_ref[0, pl.ds(r0, ROW_CHUNK), :])

    _for_row_chunks(na * m, fill)

    def stage1(b, carry):
        z = jnp.concatenate([zpr[pl.ds(b, na, stride=pz), :], zpi[pl.ds(b, na, stride=pz), :]], axis=0)
        s1[pl.ds(pl.multiple_of(b * p1, SUBLANES), 2 * na), :] = _dot(st1_ref[b], z.astype(BF16))
        return carry

    lax.fori_loop(0, m, stage1, 0, unroll=FFT_UNROLL)

    po = _pitch(na)

    def stage2(ka, carry):
        a = jnp.concatenate([s1[pl.ds(ka, m, stride=p1), :], s1[pl.ds(na + ka, m, stride=p1), :]], axis=0)
        op[pl.ds(ka, m, stride=po), :] = _dot(st2_ref[...], a.astype(BF16))
        return carry

    lax.fori_loop(0, na, stage2, 0, unroll=FFT_UNROLL)

    def unpad(j, r0):
        o_ref[0, pl.ds(r0, ROW_CHUNK), :] = _load_padded(op, j, na)

    _for_row_chunks(na * m, unpad)


def _fourier(zr, zi):
    B, L, C = zr.shape
    m = FFT_MINOR
    na = L // m
    ct = FFT_COLS
    st1, st2 = _fourier_tables(L)
    spec = pl.BlockSpec((1, L, ct), lambda j, b: (b, 0, j))
    pad = pltpu.VMEM((na * _pitch(m), ct), F32)
    return pl.pallas_call(
        functools.partial(_fourier_kernel, na),
        grid=(C // ct, B),
        in_specs=[spec, spec, _const_spec((m, 2 * na, 2 * na)), _const_spec((m, 2 * m))],
        out_specs=spec,
        out_shape=jax.ShapeDtypeStruct((B, L, C), F32),
        scratch_shapes=[pad, pad, pltpu.VMEM((m * _pitch(2 * na), ct), F32),
                        pltpu.VMEM((m * _pitch(na), ct), F32)],
        compiler_params=_params(("parallel", "parallel")),
        name="fourier",
    )(zr, zi, _mxu_table(st1), _mxu_table(st2))


def _hyena_kernel(conv_z, L, zin_ref, gin_ref, cwz_ref, cbz_ref, cwg_ref, cbg_ref, skip_ref, kf_ref,
                  fwd1_ref, inv1_ref, st2f_ref, st2i_ref, sign_ref, cosb_ref, sinb_ref, out_ref, zp, yp, s1, g2, hh):
    m = FFT_MINOR
    nh = L // m
    rows1 = 2 * nh + SUBLANES
    pz, p1, p2 = _pitch(m), _pitch(rows1), _pitch(2 * m)

    def fill(j, r0):
        if conv_z:
            _store_padded(zp, j, _short_conv_chunk(zin_ref, cwz_ref, cbz_ref, L, j, r0))
        else:
            _store_padded(zp, j, zin_ref[0, pl.ds(r0, ROW_CHUNK), :])

    _for_row_chunks(L, fill)

    def stage1(b, carry):
        slab = zp[pl.ds(b, nh, stride=pz), :].astype(BF16)
        s1[pl.ds(pl.multiple_of(b * p1, SUBLANES), rows1), :] = _dot(fwd1_ref[b], slab)
        return carry

    lax.fori_loop(0, m, stage1, 0, unroll=FFT_UNROLL)

    def spectrum_row(ka, re_row, im_row):
        a = jnp.concatenate([s1[pl.ds(re_row, m, stride=p1), :], s1[pl.ds(im_row, m, stride=p1), :]], axis=0)
        x = _dot(st2f_ref[...], a.astype(BF16))
        xr, xi = x[:m], x[m:]
        k = kf_ref[ka]
        kr, ki = k[:m], k[m:]
        y = jnp.concatenate([xr * kr - xi * ki, xr * ki + xi * kr], axis=0).astype(BF16)
        return _dot(st2i_ref[...], y)

    def stage2(ka, carry):
        g2[pl.ds(pl.multiple_of(ka * p2, SUBLANES), 2 * m), :] = spectrum_row(ka, ka, nh + ka)
        return carry

    lax.fori_loop(0, nh, stage2, 0, unroll=FFT_UNROLL)
    g_mid = spectrum_row(nh, 2 * nh, 2 * nh + 1)
    hh[...] = cosb_ref[...] * g_mid[:m] - sinb_ref[...] * g_mid[m:]
    sign = sign_ref[...]

    def stage3(b, carry):
        g = jnp.concatenate([g2[pl.ds(b, nh, stride=p2), :], g2[pl.ds(m + b, nh, stride=p2), :]], axis=0)
        yp[pl.ds(b, nh, stride=pz), :] = _dot(inv1_ref[b], g.astype(BF16)) + sign * hh[pl.ds(b, 1), :]
        return carry

    lax.fori_loop(0, m, stage3, 0, unroll=FFT_UNROLL)
    skip = skip_ref[...]

    def finish(j, r0):
        gate = _short_conv_chunk(gin_ref, cwg_ref, cbg_ref, L, j, r0)
        out_ref[0, pl.ds(r0, ROW_CHUNK), :] = gate * (_load_padded(yp, j) + skip * _load_padded(zp, j))

    _for_row_chunks(L, finish)


def _hyena_order(order, zin, z_col0, uh, g_col0, conv_w, conv_b, skip, kf, L):
    B = uh.shape[0]
    m = FFT_MINOR
    ct = FFT_COLS
    ncol = D_HYENA // ct
    nh = L // m
    rows1 = 2 * nh + SUBLANES
    conv_z = order == 0
    fwd_half, inv_half, _, sign, cos_b, sin_b = _hyena_tables(L)
    st2f, st2i, _, _ = _stage2_tables()
    z_cols = (lambda j, b: (b, 0, z_col0 + j))
    g_cols = (lambda j, b: (b, 0, g_col0 + j))
    zw_col0 = z_col0 if conv_z else 0
    pad = pltpu.VMEM((nh * _pitch(m), ct), F32)
    return pl.pallas_call(
        functools.partial(_hyena_kernel, conv_z, L),
        grid=(ncol, B),
        in_specs=[pl.BlockSpec((1, L, ct), z_cols),
                  pl.BlockSpec((1, L, ct), g_cols),
                  pl.BlockSpec((3, ct), lambda j, b: (0, zw_col0 + j)),
                  pl.BlockSpec((1, ct), lambda j, b: (0, zw_col0 + j)),
                  pl.BlockSpec((3, ct), lambda j, b: (0, g_col0 + j)),
                  pl.BlockSpec((1, ct), lambda j, b: (0, g_col0 + j)),
                  pl.BlockSpec((1, ct), lambda j, b: (0, j)),
                  pl.BlockSpec((nh + 1, 2 * m, ct), lambda j, b: (0, 0, order * ncol + j),
                               pipeline_mode=pl.Buffered(1)),
                  _const_spec((m, rows1, nh)),
                  _const_spec((m, nh, 2 * nh)),
                  _const_spec((2 * m, 2 * m)),
                  _const_spec((2 * m, 2 * m)),
                  _const_spec((nh, LANES)), _const_spec((m, LANES)), _const_spec((m, LANES))],
        out_specs=pl.BlockSpec((1, L, ct), lambda j, b: (b, 0, j)),
        out_shape=jax.ShapeDtypeStruct((B, L, D_HYENA), F32),
        scratch_shapes=[pad, pad,
                        pltpu.VMEM((m * _pitch(rows1), ct), F32),
                        pltpu.VMEM((nh * _pitch(2 * m), ct), F32),
                        pltpu.VMEM((m, ct), F32)],
        compiler_params=_params(("parallel", "arbitrary")),
        name=f"hyena{order}",
    )(zin, uh, conv_w, conv_b, conv_w, conv_b, skip, kf,
      _mxu_table(fwd_half), _mxu_table(inv_half), _mxu_table(st2f), _mxu_table(st2i),
      jnp.asarray(sign, F32), jnp.asarray(cos_b, F32), jnp.asarray(sin_b, F32))


def _merge_kernel(n_first, f_ref, z_ref, sg_ref, xa_ref, xb_ref, gt_ref, sc_ref, sh_ref, wf_ref, wh_ref, wo_ref,
                  rw_ref, rb_ref, tri_ref, x1_ref, h2_ref, route_ref, rt_ref, cnt_ref):
    i = pl.program_id(0)
    tm, D = xa_ref.shape
    sub = MERGE_SUBTILE
    lane = lax.broadcasted_iota(jnp.int32, (sub, LANES), 1).astype(F32)
    neg = -1e30

    def first_max(v):
        mx = jnp.max(v, axis=1, keepdims=True)
        return mx, jnp.min(jnp.where(v == mx, lane, float(LANES)), axis=1, keepdims=True)

    @pl.when(i == 0)
    def _():
        cnt_ref[...] = jnp.zeros_like(cnt_ref)

    def sub_tile(r0, count):
        rows = pl.ds(r0, sub)
        ya = _dot(f_ref[rows, :].astype(BF16), wf_ref[...])
        yb = _dot(z_ref[rows, :].astype(BF16), wh_ref[...])
        merged = sg_ref[rows, :D].astype(F32) * ya + sg_ref[rows, D:].astype(F32) * yb
        x = jnp.where(i < n_first, xa_ref[rows, :], xb_ref[rows, :])
        x1 = x + gt_ref[0] * _dot(merged.astype(BF16), wo_ref[...])
        x1_ref[rows, :] = x1
        h2 = _rms_mod(x1, sc_ref[0], sh_ref[0])
        _to_token_tiles(h2_ref, h2, r0 * SUBLANES)

        h_hi, h_lo = _split_bf16(h2)
        logits = (_dot(h_hi, rw_hi) + (_dot(h_lo, rw_hi) + _dot(h_hi, rw_lo))) + rb_ref[...]
        gl = jnp.where(lane < N_GROUPS, logits, neg)
        gmax, g = first_max(gl)
        p_g = 1.0 / jnp.sum(jnp.exp(gl - gmax), axis=1, keepdims=True)
        lo = N_GROUPS + EXPERTS_PER_GROUP * g
        el = jnp.where((lane >= lo) & (lane < lo + EXPERTS_PER_GROUP), logits, neg)
        m1, i1 = first_max(el)
        m2, i2 = first_max(jnp.where(lane == i1, neg, el))
        r = jnp.exp(m2 - m1)
        wt1 = p_g / (1.0 + r)
        wt2 = p_g * r / (1.0 + r)
        e1 = i1 - N_GROUPS
        e2 = i2 - N_GROUPS

        onehot = ((lane == e1) | (lane == e2)).astype(BF16)
        before = _dot(tri_ref[...], onehot) + count
        r1 = jnp.sum(jnp.where(lane == e1, before, 0.0), axis=1, keepdims=True)
        r2 = jnp.sum(jnp.where(lane == e2, before, 0.0), axis=1, keepdims=True)

        packed = jnp.zeros((sub, LANES), F32)
        for slot, v in enumerate((e1, e2, wt1, wt2, r1, r2)):
            packed = jnp.where(lane == slot, v, packed)
        route_ref[rows, :] = packed
        rt_ref[:, rows] = packed.T[0:SUBLANES, :]
        return count + jnp.sum(onehot.astype(F32), axis=0, keepdims=True)

    rw_hi, rw_lo = _split_bf16(rw_ref[...])
    count = cnt_ref[...]
    for h in range(tm // sub):
        count = sub_tile(h * sub, count)
    cnt_ref[...] = count


def _merge(f2d, z2d, sg, xa, xb, gt1, sc2, sh2, w_four, w_hyena, w_out, rw, rb, L):
    D = xa.shape[1]
    T = xa.shape[0] + xb.shape[0]
    assert D == SUBLANES * LANES
    tm = TOKEN_TILE
    per_b = L // tm
    n_first = xa.shape[0] // tm
    d_f = f2d.shape[1]
    d_h = z2d.shape[1]
    sub = MERGE_SUBTILE
    tri = jnp.asarray(np.tril(np.ones((sub, sub)), -1), BF16)
    mod_spec = pl.BlockSpec((1, 1, D), lambda i: (i // per_b, 0, 0))
    row = lambda w: pl.BlockSpec((tm, w), lambda i: (i, 0))
    return pl.pallas_call(
        functools.partial(_merge_kernel, n_first),
        grid=(T // tm,),
        in_specs=[row(d_f), row(d_h), row(2 * D)] + _group_specs(n_first, tm, D) + [
            mod_spec, mod_spec, mod_spec,
            _const_spec((d_f, D)), _const_spec((d_h, D)), _const_spec((D, D)),
            _const_spec((D, LANES)), _const_spec((1, LANES)), _const_spec((sub, sub))],
        out_specs=[row(D), pl.BlockSpec((tm * SUBLANES, LANES), lambda i: (i, 0)), row(LANES),
                   pl.BlockSpec((SUBLANES, tm), lambda i: (0, i)), pl.BlockSpec((1, LANES), lambda i: (0, 0))],
        out_shape=[jax.ShapeDtypeStruct((T, D), F32), jax.ShapeDtypeStruct((T * SUBLANES, LANES), F32),
                   jax.ShapeDtypeStruct((T, LANES), F32), jax.ShapeDtypeStruct((SUBLANES, T), F32),
                   jax.ShapeDtypeStruct((1, LANES), F32)],
        compiler_params=_params(("arbitrary",)),
        name="merge",
    )(f2d, z2d, sg, xa, xb, gt1, sc2, sh2, w_four, w_hyena, w_out, rw, rb, tri)


def _to_token_tiles(ref, val, row0=0):
    n = val.shape[0]
    for s in range(SUBLANES):
        ref[pl.ds(row0 + s, n, stride=SUBLANES), :] = val[:, s * LANES:(s + 1) * LANES]


def _from_token_tiles(ref, row0, n):
    return jnp.concatenate([ref[pl.ds(row0 + s, n, stride=SUBLANES), :] for s in range(SUBLANES)], axis=1)


def _gather_start(idx_ref, src_hbm, dst, row0, n_tok, sem):
    def issue(g, carry):
        for u in range(DMA_GROUP):
            r = g * DMA_GROUP + u
            src = pl.multiple_of(idx_ref[0, 0, r] * SUBLANES, SUBLANES)
            row = pl.multiple_of(row0 + r * SUBLANES, SUBLANES)
            copy = pltpu.make_async_copy(src_hbm.at[pl.ds(src, SUBLANES)], dst.at[pl.ds(row, SUBLANES)], sem)
            copy.start(priority=u % 2)
        return carry

    lax.fori_loop(0, n_tok // DMA_GROUP, issue, 0)


def _gather_wait(src_hbm, dst, row0, n_tok, sem):
    rows = n_tok * SUBLANES
    pltpu.make_async_copy(src_hbm.at[pl.ds(0, rows)], dst.at[pl.ds(row0, rows)], sem).wait()


def _dispatch_kernel(pend_ref, dest_ref, h_ref, x_hbm, zero, sem, zsem):
    i = pl.program_id(0)
    tm = dest_ref.shape[2] // TOP_K
    bm = MOE_BLOCK

    def token_rows(ref, tok):
        return ref.at[pl.ds(pl.multiple_of(tok * SUBLANES, SUBLANES), SUBLANES)]

    def drain(n_tok, s):
        rows = n_tok * SUBLANES
        pltpu.make_async_copy(zero.at[pl.ds(0, rows)], x_hbm.at[pl.ds(0, rows)], s).wait()

    @pl.when(i == 0)
    def _():
        zero[...] = jnp.zeros_like(zero)
        n_blocks = x_hbm.shape[0] // (bm * SUBLANES)
        used = pend_ref[N_EXPERTS - 1] // bm

        def zero_block(first_tok):
            row = pl.multiple_of(first_tok * SUBLANES, bm * SUBLANES)
            pltpu.make_async_copy(zero, x_hbm.at[pl.ds(row, bm * SUBLANES)], zsem).start()

        def per_expert(e, carry):
            zero_block(jnp.maximum(pend_ref[e] - bm, 0))
            return carry

        def per_unused(blk, carry):
            zero_block(blk * bm)
            return carry

        def drain_block(blk, carry):
            drain(bm, zsem)
            return carry

        lax.fori_loop(0, N_EXPERTS, per_expert, 0)
        lax.fori_loop(used, n_blocks, per_unused, 0)
        lax.fori_loop(0, N_EXPERTS + n_blocks - used, drain_block, 0)

    for k in range(TOP_K):
        def issue(g, carry, k=k):
            for u in range(DMA_GROUP):
                r = g * DMA_GROUP + u
                dst = token_rows(x_hbm, dest_ref[0, 0, k * tm + r])
                pltpu.make_async_copy(token_rows(h_ref, r), dst, sem).start(priority=u % 2)
            return carry

        lax.fori_loop(0, tm // DMA_GROUP, issue, 0)

    for k in range(TOP_K):
        pltpu.make_async_copy(h_ref, x_hbm.at[pl.ds(0, tm * SUBLANES)], sem).wait()


def _dispatch(pend, dest_tiles, h2, n_rows):
    n, _, tm2 = dest_tiles.shape
    tm = tm2 // TOP_K
    grid_spec = pltpu.PrefetchScalarGridSpec(
        num_scalar_prefetch=1,
        grid=(n,),
        in_specs=[pl.BlockSpec((1, 1, tm2), lambda i, pe: (i, 0, 0), memory_space=pltpu.SMEM),
                  pl.BlockSpec((tm * SUBLANES, LANES), lambda i, pe: (i, 0))],
        out_specs=pl.BlockSpec(memory_space=pl.ANY),
        scratch_shapes=[pltpu.VMEM((MOE_BLOCK * SUBLANES, LANES), F32), pltpu.SemaphoreType.DMA(()),
                        pltpu.SemaphoreType.DMA(())],
    )
    return pl.pallas_call(
        _dispatch_kernel,
        grid_spec=grid_spec,
        out_shape=jax.ShapeDtypeStruct((n_rows * SUBLANES, LANES), F32),
        compiler_params=_params(("arbitrary",)),
        name="dispatch",
    )(pend, dest_tiles, h2)


def _expert_kernel(blk_e_ref, nused_ref, x_ref, wg_ref, wu_ref, wd_ref, y_ref, wg16, wu16, wd16):
    i = pl.program_id(0)
    nused = nused_ref[0]
    bm = MOE_BLOCK

    @pl.when((i == 0) | (blk_e_ref[i] != blk_e_ref[jnp.maximum(i - 1, 0)]))
    def _():
        wg16[...] = wg_ref[0].astype(BF16)
        wu16[...] = wu_ref[0].astype(BF16)
        wd16[...] = wd_ref[0].astype(BF16)

    @pl.when(i < nused)
    def _():
        x = _from_token_tiles(x_ref, 0, bm).astype(BF16)
        g = _dot(x, wg16[...])
        u = _dot(x, wu16[...])
        a = (g * jax.nn.sigmoid(g) * u).astype(BF16)
        _to_token_tiles(y_ref, _dot(a, wd16[...]))

    @pl.when(i >= nused)
    def _():
        y_ref[...] = jnp.zeros_like(y_ref)


def _experts(blk_e, nused, xb, wg, wu, wd):
    nb = blk_e.shape[0]
    bm = MOE_BLOCK
    D, de = wg.shape[1:]
    assert D == SUBLANES * LANES and xb.shape == (nb * bm * SUBLANES, LANES)
    grid_spec = pltpu.PrefetchScalarGridSpec(
        num_scalar_prefetch=2,
        grid=(nb,),
        in_specs=[pl.BlockSpec((bm * SUBLANES, LANES), lambda i, be, nu: (jnp.minimum(i, nu[0] - 1), 0)),
                  pl.BlockSpec((1, D, de), lambda i, be, nu: (be[i], 0, 0)),
                  pl.BlockSpec((1, D, de), lambda i, be, nu: (be[i], 0, 0)),
                  pl.BlockSpec((1, de, D), lambda i, be, nu: (be[i], 0, 0))],
        out_specs=pl.BlockSpec((bm * SUBLANES, LANES), lambda i, be, nu: (i, 0)),
        scratch_shapes=[pltpu.VMEM((D, de), BF16), pltpu.VMEM((D, de), BF16), pltpu.VMEM((de, D), BF16)],
    )
    return pl.pallas_call(
        _expert_kernel,
        grid_spec=grid_spec,
        out_shape=jax.ShapeDtypeStruct((nb * bm * SUBLANES, LANES), F32),
        compiler_params=_params(("arbitrary",)),
        name="experts",
    )(blk_e, nused, xb, wg, wu, wd)


def _combine_kernel(n_steps, cur_ref, nxt_ref, y_hbm, x1_ref, route_ref, gt_ref, gf_ref, o_ref, buf, sem):
    i = pl.program_id(0)
    tm = x1_ref.shape[0]
    slot = i % 2
    rows = 2 * tm * SUBLANES

    @pl.when(i == 0)
    def _():
        _gather_start(cur_ref, y_hbm, buf, 0, 2 * tm, sem.at[0])

    @pl.when(i + 1 < n_steps)
    def _():
        _gather_start(nxt_ref, y_hbm, buf, (1 - slot) * rows, 2 * tm, sem.at[1 - slot])

    _gather_wait(y_hbm, buf, slot * rows, 2 * tm, sem.at[slot])
    route = route_ref[...]
    y1 = _from_token_tiles(buf, slot * rows, tm)
    y2 = _from_token_tiles(buf, slot * rows + tm * SUBLANES, tm)
    x = x1_ref[...] + gt_ref[0] * (route[:, 2:3] * y1 + route[:, 3:4] * y2)
    o_ref[...] = x * lax.rsqrt(jnp.mean(x * x, axis=-1, keepdims=True) + EPS) * gf_ref[...]


def _combine(tile0, n_tok, dest_tiles, yb, x1, route, gt2, g_final, L):
    T, D = x1.shape
    tm = TOKEN_TILE
    per_b = L // tm
    n = n_tok // tm
    last = tile0 + n - 1
    smem = lambda f: pl.BlockSpec((1, 1, 2 * tm), lambda i: (f(i), 0, 0), memory_space=pltpu.SMEM)
    return pl.pallas_call(
        functools.partial(_combine_kernel, n),
        grid=(n,),
        in_specs=[smem(lambda i: tile0 + i), smem(lambda i: jnp.minimum(tile0 + i + 1, last)),
                  pl.BlockSpec(memory_space=pl.ANY),
                  pl.BlockSpec((tm, D), lambda i: (tile0 + i, 0)),
                  pl.BlockSpec((tm, LANES), lambda i: (tile0 + i, 0)),
                  pl.BlockSpec((1, 1, D), lambda i: ((tile0 + i) // per_b, 0, 0)),
                  pl.BlockSpec((1, D), lambda i: (0, 0))],
        out_specs=pl.BlockSpec((tm, D), lambda i: (i, 0)),
        out_shape=jax.ShapeDtypeStruct((n_tok, D), F32),
        scratch_shapes=[pltpu.VMEM((2 * 2 * tm * SUBLANES, LANES), F32), pltpu.SemaphoreType.DMA((2,))],
        compiler_params=_params(("arbitrary",)),
        name="combine",
    )(dest_tiles, dest_tiles, yb, x1, route, gt2, g_final.reshape(1, D))


def _encoder(xp, xs, c, p, g_final):
    n_prompt, L, D = xp.shape
    B = n_prompt + xs.shape[0]
    T = B * L
    xa, xb = xp.reshape(-1, D), xs.reshape(-1, D)
    d_f = N_FOURIER_GROUPS * FOURIER_GROUP_DIM
    d_h = (HYENA_ORDER + 1) * D_HYENA
    d_g = 2 * D

    mod = _ada(c, p["w_ada"], p["b_ada"])
    sh1, sc1, gt1, sh2, sc2, gt2 = [mod[:, k * D:(k + 1) * D].reshape(B, 1, D) for k in range(6)]

    wb = jnp.concatenate([p["w_in"][:, :d_f], jnp.broadcast_to(p["b_in"][None, :d_f], (8, d_f))], axis=0)
    fre, fim = _fold_channel_dft(wb)
    w_all = jnp.concatenate([fre[:D], fim[:D], p["w_in"][:, d_f:]], axis=1).astype(BF16)
    b_all = jnp.concatenate([fre[D], fim[D], p["b_in"][d_f:]])[None, :]
    zr, zi, uh, sg = _inproj(xa, xb, sc1, sh1, w_all, b_all, L, d_f, d_h, d_g)

    f = _fourier(zr.reshape(B, L, d_f), zi.reshape(B, L, d_f))

    ktime, ksum = _filt_time(L, p["filt_w1"], p["filt_b1"], p["filt_w2"], p["filt_b2"],
                             p["filt_w3"], p["filt_b3"], p["filt_freq"], p["filt_wout"])
    kf = _filt_fft(L, ktime, ksum)
    uh3 = uh.reshape(B, L, d_h)
    conv_b = p["conv_b"][None, :]
    ncol = D_HYENA // FFT_COLS
    skip = p["hyena_skip"]
    z = _hyena_order(0, uh3, 0, uh3, ncol, p["conv_w"], conv_b, skip[0:1], kf, L)
    z = _hyena_order(1, z, 0, uh3, 2 * ncol, p["conv_w"], conv_b, skip[1:2], kf, L)

    rw = jnp.zeros((D, LANES), F32).at[:, :N_GROUPS].set(p["router_w1"])
    rw = rw.at[:, N_GROUPS:N_GROUPS + N_EXPERTS].set(p["router_w2"])
    rb = jnp.zeros((1, LANES), F32).at[0, :N_GROUPS].set(p["router_b1"])
    rb = rb.at[0, N_GROUPS:N_GROUPS + N_EXPERTS].set(p["router_b2"])
    x1, h2, route, route_t, counts = _merge(f.reshape(T, d_f), z.reshape(T, D_HYENA), sg, xa, xb,
                                            gt1, sc2, sh2, p["w_four"].astype(BF16), p["w_hyena"].astype(BF16),
                                            p["w_out"].astype(BF16), rw, rb, L)

    bm = MOE_BLOCK
    tm = TOKEN_TILE
    nb = (T * TOP_K) // bm + N_EXPERTS
    e = route_t[0:2].astype(jnp.int32)
    rank = route_t[4:6].astype(jnp.int32)
    cnt = counts[0, :N_EXPERTS].astype(jnp.int32)
    pcnt = (cnt + bm - 1) // bm * bm
    pend = jnp.cumsum(pcnt)
    experts = jnp.arange(N_EXPERTS, dtype=jnp.int32)
    dest = rank + jnp.sum(jnp.where(e[..., None] == experts, pend - pcnt, 0), axis=-1)
    blk_row0 = jnp.arange(nb, dtype=jnp.int32)[:, None] * bm
    blk_e = jnp.minimum(jnp.sum((pend[None, :] <= blk_row0).astype(jnp.int32), axis=1), N_EXPERTS - 1)
    nused = (pend[-1] // bm).astype(jnp.int32).reshape(1)
    dest_tiles = dest.reshape(TOP_K, T // tm, tm).transpose(1, 0, 2).reshape(T // tm, 1, TOP_K * tm)
    xb = _dispatch(pend.astype(jnp.int32), dest_tiles, h2, nb * bm)
    yb = _experts(blk_e, nused, xb, p["exp_w_gate"], p["exp_w_up"], p["exp_w_down"])

    t_prompt = n_prompt * L
    outs = []
    for tile0, n_tok in ((0, t_prompt), (t_prompt // tm, T - t_prompt)):
        outs.append(_combine(tile0, n_tok, dest_tiles, yb, x1, route, gt2, g_final, L))
    return outs[0].reshape(xp.shape), outs[1].reshape(xs.shape)


def kernel(x_prompt, x_sample, c_prompt, c_sample, w_ada, b_ada, w_in, b_in, conv_w, conv_b, filt_w1, filt_b1, filt_w2, filt_b2, filt_w3, filt_b3, filt_freq, filt_wout, hyena_skip, w_four, w_hyena, w_out, router_w1, router_b1, router_w2, router_b2, exp_w_gate, exp_w_up, exp_w_down, g_final):
    assert w_ada.shape[0] == 1, "single-layer block"
    assert x_prompt.shape[1:] == x_sample.shape[1:], "both request groups share sequence length and width"
    p = dict(w_ada=w_ada[0], b_ada=b_ada[0], w_in=w_in[0], b_in=b_in[0], conv_w=conv_w[0], conv_b=conv_b[0],
             filt_w1=filt_w1[0], filt_b1=filt_b1[0], filt_w2=filt_w2[0], filt_b2=filt_b2[0],
             filt_w3=filt_w3[0], filt_b3=filt_b3[0], filt_freq=filt_freq[0], filt_wout=filt_wout[0],
             hyena_skip=hyena_skip[0], w_four=w_four[0], w_hyena=w_hyena[0], w_out=w_out[0],
             router_w1=router_w1[0], router_b1=router_b1[0], router_w2=router_w2[0], router_b2=router_b2[0],
             exp_w_gate=exp_w_gate[0], exp_w_up=exp_w_up[0], exp_w_down=exp_w_down[0])
    c = jnp.concatenate([c_prompt, c_sample], axis=0)
    return _encoder(x_prompt, x_sample, c, p, g_final)
```

```python
import functools
import math

import numpy as np
import jax
import jax.numpy as jnp
from jax import lax
from jax.experimental import pallas as pl
from jax.experimental.pallas import tpu as pltpu

F32 = jnp.float32
BF16 = jnp.bfloat16
HI = lax.Precision.HIGHEST

EPS = 1e-6
FFT_MINOR = 64
N_FOURIER_GROUPS = 4
FOURIER_GROUP_DIM = 128
D_HYENA = 512
HYENA_ORDER = 2
POS_BANDS = 16
N_GROUPS = 4
EXPERTS_PER_GROUP = 8
N_EXPERTS = 32
TOP_K = 2
SHORT_DECAY_PCT = 0.3
LONG_DECAY_PCT = 1.5
DECAY_TARGET = 1e-2
LANES = 128
SUBLANES = 8
VMEM_LIMIT = 56 * 1024 * 1024

TOKEN_TILE = 512
MERGE_SUBTILE = 128
MOE_BLOCK = 512
FFT_COLS = LANES
ROW_CHUNK = 256
FFT_UNROLL = 32
DMA_GROUP = 8


def _dot(a, b):
    return jnp.dot(a, b, preferred_element_type=F32)


def _dot_hi(a, b):
    return jnp.dot(a, b, preferred_element_type=F32, precision=HI)


def _split_bf16(x):
    hi = x.astype(BF16)
    return hi, (x - hi.astype(F32)).astype(BF16)


def _dot3(a, b):
    a_hi, a_lo = _split_bf16(a)
    b_hi, b_lo = _split_bf16(b)
    return _dot(a_hi, b_hi) + (_dot(a_lo, b_hi) + _dot(a_hi, b_lo))


def _params(sem=None):
    return pltpu.CompilerParams(dimension_semantics=sem, vmem_limit_bytes=VMEM_LIMIT)


def _const_spec(shape):
    nd = len(shape)
    return pl.BlockSpec(shape, lambda *_: (0,) * nd, pipeline_mode=pl.Buffered(1))


def _pitch(rows):
    p = -(-rows // SUBLANES)
    return SUBLANES * (p if p % 2 else p + 1)


@functools.lru_cache(maxsize=None)
def _stage2_tables():
    i = np.arange(FFT_MINOR)
    ph = 2.0 * np.pi * np.outer(i, i) / FFT_MINOR
    c, s = np.cos(ph), np.sin(ph)
    fwd = np.block([[c, s], [-s, c]])
    inv = np.block([[c, -s], [s, c]])
    return fwd, inv, c, s


@functools.lru_cache(maxsize=None)
def _hyena_tables(L):
    n_fft = 2 * L
    m = FFT_MINOR
    na = n_fft // m
    nh = na // 2
    b = np.arange(m)[:, None, None]
    ka = np.arange(nh + 1)[None, :, None]

    def forward(a_count):
        a = np.arange(a_count)[None, None, :]
        th = 2.0 * np.pi * ((ka * (m * a + b)) % n_fft) / n_fft
        c, s = np.cos(th), -np.sin(th)
        pad = np.zeros((m, SUBLANES - 2, a_count))
        return np.concatenate([c[:, :nh], s[:, :nh], c[:, nh:], s[:, nh:], pad], axis=1)

    fwd_half = forward(nh)
    fwd_full = forward(na)
    weight = np.where(np.arange(nh) == 0, 1.0, 2.0)[None, :, None]
    inv_half = np.transpose(fwd_half[:, :2 * nh] * np.concatenate([weight, weight], axis=1), (0, 2, 1)) / n_fft
    sign = np.broadcast_to(((-1.0) ** np.arange(nh))[:, None] / n_fft, (nh, LANES))
    ph = np.pi * np.arange(m) / m
    cos_b = np.broadcast_to(np.cos(ph)[:, None], (m, LANES))
    sin_b = np.broadcast_to(np.sin(ph)[:, None], (m, LANES))
    return fwd_half, inv_half, fwd_full, sign, cos_b, sin_b


@functools.lru_cache(maxsize=None)
def _fourier_tables(L):
    na = L // FFT_MINOR
    b = np.arange(FFT_MINOR)[:, None, None]
    ka = np.arange(na)[None, :, None]
    a = np.arange(na)[None, None, :]
    th = 2.0 * np.pi * ((ka * (FFT_MINOR * a + b)) % L) / L
    c, s = np.cos(th), np.sin(th)
    st1 = np.concatenate([np.concatenate([c, s], axis=2),
                          np.concatenate([-s, c], axis=2)], axis=1)
    _, _, c2, s2 = _stage2_tables()
    st2 = np.concatenate([c2, s2], axis=1) / math.sqrt(L)
    return st1, st2


def _mxu_table(table):
    return jnp.asarray(table, F32).astype(BF16)


@functools.lru_cache(maxsize=None)
def _channel_dft():
    i = np.arange(FOURIER_GROUP_DIM)
    ph = 2.0 * np.pi * np.outer(i, i) / FOURIER_GROUP_DIM
    return np.concatenate([np.cos(ph), -np.sin(ph)], axis=1) / math.sqrt(FOURIER_GROUP_DIM)


def _ada_kernel(c_ref, w_ref, b_ref, o_ref):
    c = c_ref[...]
    o_ref[...] = _dot_hi(c * jax.nn.sigmoid(c), w_ref[...]) + b_ref[...]


def _ada(c, w_ada, b_ada):
    nb, d = c.shape
    n = w_ada.shape[1]
    tn = 1536
    return pl.pallas_call(
        _ada_kernel,
        grid=(n // tn,),
        in_specs=[pl.BlockSpec((nb, d), lambda j: (0, 0)),
                  pl.BlockSpec((d, tn), lambda j: (0, j)),
                  pl.BlockSpec((1, tn), lambda j: (0, j))],
        out_specs=pl.BlockSpec((nb, tn), lambda j: (0, j)),
        out_shape=jax.ShapeDtypeStruct((nb, n), F32),
        compiler_params=_params(("arbitrary",)),
        name="ada",
    )(c, w_ada, b_ada.reshape(1, n))


def _fold_kernel(w_ref, f_ref, re_ref, im_ref):
    z = _dot_hi(w_ref[...], f_ref[...])
    re_ref[...] = z[:, :FOURIER_GROUP_DIM]
    im_ref[...] = z[:, FOURIER_GROUP_DIM:]


def _fold_channel_dft(wb):
    rows = wb.shape[0]
    gd = FOURIER_GROUP_DIM
    fmat = jnp.asarray(_channel_dft(), F32)
    return pl.pallas_call(
        _fold_kernel,
        grid=(N_FOURIER_GROUPS,),
        in_specs=[pl.BlockSpec((rows, gd), lambda g: (0, g)),
                  pl.BlockSpec((gd, 2 * gd), lambda g: (0, 0))],
        out_specs=[pl.BlockSpec((rows, gd), lambda g: (0, g)),
                   pl.BlockSpec((rows, gd), lambda g: (0, g))],
        out_shape=[jax.ShapeDtypeStruct((rows, N_FOURIER_GROUPS * gd), F32)] * 2,
        compiler_params=_params(("arbitrary",)),
        name="fold",
    )(wb, fmat)


def _for_row_chunks(n_rows, body):
    def step(j, carry):
        body(j, pl.multiple_of(j * ROW_CHUNK, ROW_CHUNK))
        return carry

    lax.fori_loop(0, n_rows // ROW_CHUNK, step, 0)


def _store_padded(dst, j, val, run=FFT_MINOR):
    pz = _pitch(run)
    per = ROW_CHUNK // run
    for i in range(per):
        dst[pl.ds(pl.multiple_of((j * per + i) * pz, SUBLANES), run), :] = val[i * run:(i + 1) * run]


def _load_padded(src, j, run=FFT_MINOR):
    pz = _pitch(run)
    per = ROW_CHUNK // run
    return jnp.concatenate([src[pl.ds(pl.multiple_of((j * per + i) * pz, SUBLANES), run), :] for i in range(per)],
                           axis=0)


def _short_conv_chunk(src_ref, w_ref, b_ref, L, j, r0):
    R = ROW_CHUNK
    ct = src_ref.shape[-1]
    row = lax.broadcasted_iota(jnp.int32, (R, ct), 0)
    cur = src_ref[0, pl.ds(r0, R), :]
    before = src_ref[0, pl.ds(jnp.maximum(r0 - 1, 0), 1), :] * jnp.where(j > 0, 1.0, 0.0)
    after = src_ref[0, pl.ds(jnp.minimum(r0 + R, L - 1), 1), :] * jnp.where(j < L // R - 1, 1.0, 0.0)
    up = jnp.where(row == 0, before, pltpu.roll(cur, 1, 0))
    dn = jnp.where(row == R - 1, after, pltpu.roll(cur, R - 1, 0))
    return w_ref[0:1, :] * up + w_ref[1:2, :] * cur + w_ref[2:3, :] * dn + b_ref[...]


def _filt_time_kernel(L, rows, bands_ref, delt_ref, w1_ref, b1_ref, w2_ref, b2_ref, w3_ref, b3_ref,
                      fr_ref, wo_ref, k_ref, sum_ref):
    i = pl.program_id(0)
    n = i * rows + lax.broadcasted_iota(jnp.int32, (rows, 1), 0)
    fwd = n < L
    pos = jnp.where(fwd, n, 2 * L - n).astype(F32)
    t = pos * (1.0 / (L - 1))
    ang = (2.0 * math.pi / L) * pos * bands_ref[...]
    w1 = w1_ref[...]
    pre = (t * w1[0:1, :] + _dot3(jnp.cos(ang), w1[1:1 + POS_BANDS, :])
           + _dot3(-jnp.sin(ang), w1[1 + POS_BANDS:, :]) + b1_ref[...])
    fr = fr_ref[...]
    h = jnp.sin(fr[0:1, :] * pre)
    h = jnp.sin(fr[1:2, :] * (_dot3(h, w2_ref[...]) + b2_ref[...]))
    h = jnp.sin(fr[2:3, :] * (_dot3(h, w3_ref[...]) + b3_ref[...]))
    window = jnp.exp(-t * delt_ref[...])
    live = n != L
    tile_fwd = i < (L // rows)
    parts = []
    for o in range(HYENA_ORDER):
        base = o * 2 * D_HYENA
        w = jnp.where(tile_fwd, wo_ref[:, base:base + D_HYENA], wo_ref[:, base + D_HYENA:base + 2 * D_HYENA])
        parts.append(jnp.where(live, _dot3(h, w) * window, 0.0))
    k = jnp.concatenate(parts, axis=1)
    k_ref[...] = k

    @pl.when(i == 0)
    def _():
        sum_ref[...] = jnp.zeros_like(sum_ref)

    sum_ref[...] += jnp.sum(jnp.abs(k), axis=0, keepdims=True)


def _filt_time(L, w1, b1, w2, b2, w3, b3, freq, wout):
    rows = 512
    assert L % rows == 0, "each row tile must hold one filter direction only"
    n_fft = 2 * L
    cols = HYENA_ORDER * D_HYENA
    bands = jnp.linspace(1e-4, POS_BANDS - 1, POS_BANDS, dtype=F32).reshape(1, POS_BANDS)
    max_decay = math.log(DECAY_TARGET) / SHORT_DECAY_PCT
    min_decay = math.log(DECAY_TARGET) / LONG_DECAY_PCT
    deltas = jnp.abs(jnp.linspace(min_decay, max_decay, D_HYENA, dtype=F32)).reshape(1, D_HYENA)
    args = (bands, deltas, w1, b1.reshape(1, -1), w2, b2.reshape(1, -1), w3, b3.reshape(1, -1), freq, wout)
    return pl.pallas_call(
        functools.partial(_filt_time_kernel, L, rows),
        grid=(n_fft // rows,),
        in_specs=[pl.BlockSpec(a.shape, lambda i: (0, 0)) for a in args],
        out_specs=[pl.BlockSpec((rows, cols), lambda i: (i, 0)),
                   pl.BlockSpec((1, cols), lambda i: (0, 0))],
        out_shape=[jax.ShapeDtypeStruct((n_fft, cols), F32),
                   jax.ShapeDtypeStruct((1, cols), F32)],
        compiler_params=_params(("arbitrary",)),
        name="filt_time",
    )(*args)


def _filt_fft_kernel(na, k_ref, sum_ref, st1_ref, st2_ref, o_ref, kp, s1):
    m = FFT_MINOR
    nh = na // 2
    rows1 = 2 * nh + SUBLANES
    pz, p1 = _pitch(m), _pitch(rows1)
    _for_row_chunks(na * m, lambda j, r0: _store_padded(kp, j, k_ref[pl.ds(r0, ROW_CHUNK), :]))

    def stage1(b, carry):
        slab = kp[pl.ds(b, na, stride=pz), :]
        s1[pl.ds(pl.multiple_of(b * p1, SUBLANES), rows1), :] = _dot3(st1_ref[b], slab)
        return carry

    lax.fori_loop(0, m, stage1, 0, unroll=4)
    inv_norm = 1.0 / sum_ref[...]

    def spectrum_row(re_row, im_row):
        a = jnp.concatenate([s1[pl.ds(re_row, m, stride=p1), :], s1[pl.ds(im_row, m, stride=p1), :]], axis=0)
        return _dot3(st2_ref[...], a) * inv_norm

    def stage2(ka, carry):
        o_ref[ka] = spectrum_row(ka, nh + ka)
        return carry

    lax.fori_loop(0, nh, stage2, 0, unroll=4)
    o_ref[nh] = spectrum_row(2 * nh, 2 * nh + 1)


def _filt_fft(L, ktime, ksum):
    n_fft, cols = ktime.shape
    m = FFT_MINOR
    na = n_fft // m
    nh = na // 2
    rows1 = 2 * nh + SUBLANES
    ct = FFT_COLS
    fwd_full = _hyena_tables(L)[2]
    st2f = _stage2_tables()[0]
    return pl.pallas_call(
        functools.partial(_filt_fft_kernel, na),
        grid=(cols // ct,),
        in_specs=[pl.BlockSpec((n_fft, ct), lambda j: (0, j)),
                  pl.BlockSpec((1, ct), lambda j: (0, j)),
                  _const_spec((m, rows1, na)),
                  _const_spec((2 * m, 2 * m))],
        out_specs=pl.BlockSpec((nh + 1, 2 * m, ct), lambda j: (0, 0, j)),
        out_shape=jax.ShapeDtypeStruct((nh + 1, 2 * m, cols), F32),
        scratch_shapes=[pltpu.VMEM((na * _pitch(m), ct), F32),
                        pltpu.VMEM((m * _pitch(rows1), ct), F32)],
        compiler_params=_params(("arbitrary",)),
        name="filt_fft",
    )(ktime, ksum, jnp.asarray(fwd_full, F32), jnp.asarray(st2f, F32))


def _rms_mod(x, scale, shift):
    y = x * lax.rsqrt(jnp.mean(x * x, axis=-1, keepdims=True) + EPS)
    return y * (1.0 + scale) + shift


def _group_tile(n_first, a_ref, b_ref):
    return jnp.where(pl.program_id(0) < n_first, a_ref[...], b_ref[...])


def _group_specs(n_first, tm, D):
    return [pl.BlockSpec((tm, D), lambda i: (jnp.minimum(i, n_first - 1), 0)),
            pl.BlockSpec((tm, D), lambda i: (jnp.maximum(i - n_first, 0), 0))]


def _inproj_kernel(d_f, d_h, n_first, xa_ref, xb_ref, sc_ref, sh_ref, w_ref, b_ref, zr_ref, zi_ref, uh_ref, sg_ref):
    h = _rms_mod(_group_tile(n_first, xa_ref, xb_ref), sc_ref[0], sh_ref[0]).astype(BF16)

    def proj(c0, width):
        return _dot(h, w_ref[:, c0:c0 + width]) + b_ref[:, c0:c0 + width]

    zr_ref[...] = proj(0, d_f)
    zi_ref[...] = proj(d_f, d_f)
    chunk = 512
    for c in range(0, d_h, chunk):
        uh_ref[:, c:c + chunk] = proj(2 * d_f + c, chunk)
    d_g = sg_ref.shape[1]
    for c in range(0, d_g, chunk):
        sg_ref[:, c:c + chunk] = jax.nn.sigmoid(proj(2 * d_f + d_h + c, chunk)).astype(BF16)


def _inproj(xa, xb, sc, sh, w, b, L, d_f, d_h, d_g):
    D = xa.shape[1]
    T = xa.shape[0] + xb.shape[0]
    tm = TOKEN_TILE
    per_b = L // tm
    n_first = xa.shape[0] // tm
    n = w.shape[1]
    mod_spec = pl.BlockSpec((1, 1, D), lambda i: (i // per_b, 0, 0))
    return pl.pallas_call(
        functools.partial(_inproj_kernel, d_f, d_h, n_first),
        grid=(T // tm,),
        in_specs=_group_specs(n_first, tm, D) + [mod_spec, mod_spec, _const_spec((D, n)), _const_spec((1, n))],
        out_specs=[pl.BlockSpec((tm, d_f), lambda i: (i, 0)),
                   pl.BlockSpec((tm, d_f), lambda i: (i, 0)),
                   pl.BlockSpec((tm, d_h), lambda i: (i, 0)),
                   pl.BlockSpec((tm, d_g), lambda i: (i, 0))],
        out_shape=[jax.ShapeDtypeStruct((T, d_f), F32), jax.ShapeDtypeStruct((T, d_f), F32),
                   jax.ShapeDtypeStruct((T, d_h), F32), jax.ShapeDtypeStruct((T, d_g), BF16)],
        compiler_params=_params(("parallel",)),
        name="inproj",
    )(xa, xb, sc, sh, w, b)


def _fourier_kernel(na, zr_ref, zi_ref, st1_ref, st2_ref, o_ref, zpr, zpi, s1, op):
    m = FFT_MINOR
    pz, p1 = _pitch(m), _pitch(2 * na)

    def fill(j, r0):
        _store_padded(zpr, j, zr_ref[0, pl.ds(r0, ROW_CHUNK), :])
        _store_padded(zpi, j, zi_ref[0, pl.ds(r0, ROW_CHUNK), :])

    _for_row_chunks(na * m, fill)

    def stage1(b, carry):
        z = jnp.concatenate([zpr[pl.ds(b, na, stride=pz), :], zpi[pl.ds(b, na, stride=pz), :]], axis=0)
        s1[pl.ds(pl.multiple_of(b * p1, SUBLANES), 2 * na), :] = _dot(st1_ref[b], z.astype(BF16))
        return carry

    lax.fori_loop(0, m, stage1, 0, unroll=FFT_UNROLL)

    po = _pitch(na)

    def stage2(ka, carry):
        a = jnp.concatenate([s1[pl.ds(ka, m, stride=p1), :], s1[pl.ds(na + ka, m, stride=p1), :]], axis=0)
        op[pl.ds(ka, m, stride=po), :] = _dot(st2_ref[...], a.astype(BF16))
        return carry

    lax.fori_loop(0, na, stage2, 0, unroll=FFT_UNROLL)

    def unpad(j, r0):
        o_ref[0, pl.ds(r0, ROW_CHUNK), :] = _load_padded(op, j, na)

    _for_row_chunks(na * m, unpad)


def _fourier(zr, zi):
    B, L, C = zr.shape
    m = FFT_MINOR
    na = L // m
    ct = FFT_COLS
    st1, st2 = _fourier_tables(L)
    spec = pl.BlockSpec((1, L, ct), lambda j, b: (b, 0, j))
    pad = pltpu.VMEM((na * _pitch(m), ct), F32)
    return pl.pallas_call(
        functools.partial(_fourier_kernel, na),
        grid=(C // ct, B),
        in_specs=[spec, spec, _const_spec((m, 2 * na, 2 * na)), _const_spec((m, 2 * m))],
        out_specs=spec,
        out_shape=jax.ShapeDtypeStruct((B, L, C), F32),
        scratch_shapes=[pad, pad, pltpu.VMEM((m * _pitch(2 * na), ct), F32),
                        pltpu.VMEM((m * _pitch(na), ct), F32)],
        compiler_params=_params(("parallel", "parallel")),
        name="fourier",
    )(zr, zi, _mxu_table(st1), _mxu_table(st2))


def _hyena_kernel(conv_z, L, zin_ref, gin_ref, cwz_ref, cbz_ref, cwg_ref, cbg_ref, skip_ref, kf_ref,
                  fwd1_ref, inv1_ref, st2f_ref, st2i_ref, sign_ref, cosb_ref, sinb_ref, out_ref, zp, yp, s1, g2, hh):
    m = FFT_MINOR
    nh = L // m
    rows1 = 2 * nh + SUBLANES
    pz, p1, p2 = _pitch(m), _pitch(rows1), _pitch(2 * m)

    def fill(j, r0):
        if conv_z:
            _store_padded(zp, j, _short_conv_chunk(zin_ref, cwz_ref, cbz_ref, L, j, r0))
        else:
            _store_padded(zp, j, zin_ref[0, pl.ds(r0, ROW_CHUNK), :])

    _for_row_chunks(L, fill)

    def stage1(b, carry):
        slab = zp[pl.ds(b, nh, stride=pz), :].astype(BF16)
        s1[pl.ds(pl.multiple_of(b * p1, SUBLANES), rows1), :] = _dot(fwd1_ref[b], slab)
        return carry

    lax.fori_loop(0, m, stage1, 0, unroll=FFT_UNROLL)

    def spectrum_row(ka, re_row, im_row):
        a = jnp.concatenate([s1[pl.ds(re_row, m, stride=p1), :], s1[pl.ds(im_row, m, stride=p1), :]], axis=0)
        x = _dot(st2f_ref[...], a.astype(BF16))
        xr, xi = x[:m], x[m:]
        k = kf_ref[ka]
        kr, ki = k[:m], k[m:]
        y = jnp.concatenate([xr * kr - xi * ki, xr * ki + xi * kr], axis=0).astype(BF16)
        return _dot(st2i_ref[...], y)

    def stage2(ka, carry):
        g2[pl.ds(pl.multiple_of(ka * p2, SUBLANES), 2 * m), :] = spectrum_row(ka, ka, nh + ka)
        return carry

    lax.fori_loop(0, nh, stage2, 0, unroll=FFT_UNROLL)
    g_mid = spectrum_row(nh, 2 * nh, 2 * nh + 1)
    hh[...] = cosb_ref[...] * g_mid[:m] - sinb_ref[...] * g_mid[m:]
    sign = sign_ref[...]

    def stage3(b, carry):
        g = jnp.concatenate([g2[pl.ds(b, nh, stride=p2), :], g2[pl.ds(m + b, nh, stride=p2), :]], axis=0)
        yp[pl.ds(b, nh, stride=pz), :] = _dot(inv1_ref[b], g.astype(BF16)) + sign * hh[pl.ds(b, 1), :]
        return carry

    lax.fori_loop(0, m, stage3, 0, unroll=FFT_UNROLL)
    skip = skip_ref[...]

    def finish(j, r0):
        gate = _short_conv_chunk(gin_ref, cwg_ref, cbg_ref, L, j, r0)
        out_ref[0, pl.ds(r0, ROW_CHUNK), :] = gate * (_load_padded(yp, j) + skip * _load_padded(zp, j))

    _for_row_chunks(L, finish)


def _hyena_order(order, zin, z_col0, uh, g_col0, conv_w, conv_b, skip, kf, L):
    B = uh.shape[0]
    m = FFT_MINOR
    ct = FFT_COLS
    ncol = D_HYENA // ct
    nh = L // m
    rows1 = 2 * nh + SUBLANES
    conv_z = order == 0
    fwd_half, inv_half, _, sign, cos_b, sin_b = _hyena_tables(L)
    st2f, st2i, _, _ = _stage2_tables()
    z_cols = (lambda j, b: (b, 0, z_col0 + j))
    g_cols = (lambda j, b: (b, 0, g_col0 + j))
    zw_col0 = z_col0 if conv_z else 0
    pad = pltpu.VMEM((nh * _pitch(m), ct), F32)
    return pl.pallas_call(
        functools.partial(_hyena_kernel, conv_z, L),
        grid=(ncol, B),
        in_specs=[pl.BlockSpec((1, L, ct), z_cols),
                  pl.BlockSpec((1, L, ct), g_cols),
                  pl.BlockSpec((3, ct), lambda j, b: (0, zw_col0 + j)),
                  pl.BlockSpec((1, ct), lambda j, b: (0, zw_col0 + j)),
                  pl.BlockSpec((3, ct), lambda j, b: (0, g_col0 + j)),
                  pl.BlockSpec((1, ct), lambda j, b: (0, g_col0 + j)),
                  pl.BlockSpec((1, ct), lambda j, b: (0, j)),
                  pl.BlockSpec((nh + 1, 2 * m, ct), lambda j, b: (0, 0, order * ncol + j),
                               pipeline_mode=pl.Buffered(1)),
                  _const_spec((m, rows1, nh)),
                  _const_spec((m, nh, 2 * nh)),
                  _const_spec((2 * m, 2 * m)),
                  _const_spec((2 * m, 2 * m)),
                  _const_spec((nh, LANES)), _const_spec((m, LANES)), _const_spec((m, LANES))],
        out_specs=pl.BlockSpec((1, L, ct), lambda j, b: (b, 0, j)),
        out_shape=jax.ShapeDtypeStruct((B, L, D_HYENA), F32),
        scratch_shapes=[pad, pad,
                        pltpu.VMEM((m * _pitch(rows1), ct), F32),
                        pltpu.VMEM((nh * _pitch(2 * m), ct), F32),
                        pltpu.VMEM((m, ct), F32)],
        compiler_params=_params(("parallel", "arbitrary")),
        name=f"hyena{order}",
    )(zin, uh, conv_w, conv_b, conv_w, conv_b, skip, kf,
      _mxu_table(fwd_half), _mxu_table(inv_half), _mxu_table(st2f), _mxu_table(st2i),
      jnp.asarray(sign, F32), jnp.asarray(cos_b, F32), jnp.asarray(sin_b, F32))


def _merge_kernel(n_first, f_ref, z_ref, sg_ref, xa_ref, xb_ref, gt_ref, sc_ref, sh_ref, wf_ref, wh_ref, wo_ref,
                  rw_ref, rb_ref, tri_ref, x1_ref, h2_ref, route_ref, rt_ref, cnt_ref):
    i = pl.program_id(0)
    tm, D = xa_ref.shape
    sub = MERGE_SUBTILE
    lane = lax.broadcasted_iota(jnp.int32, (sub, LANES), 1).astype(F32)
    neg = -1e30

    def first_max(v):
        mx = jnp.max(v, axis=1, keepdims=True)
        return mx, jnp.min(jnp.where(v == mx, lane, float(LANES)), axis=1, keepdims=True)

    @pl.when(i == 0)
    def _():
        cnt_ref[...] = jnp.zeros_like(cnt_ref)

    def sub_tile(r0, count):
        rows = pl.ds(r0, sub)
        ya = _dot(f_ref[rows, :].astype(BF16), wf_ref[...])
        yb = _dot(z_ref[rows, :].astype(BF16), wh_ref[...])
        merged = sg_ref[rows, :D].astype(F32) * ya + sg_ref[rows, D:].astype(F32) * yb
        x = jnp.where(i < n_first, xa_ref[rows, :], xb_ref[rows, :])
        x1 = x + gt_ref[0] * _dot(merged.astype(BF16), wo_ref[...])
        x1_ref[rows, :] = x1
        h2 = _rms_mod(x1, sc_ref[0], sh_ref[0])
        _to_token_tiles(h2_ref, h2, r0 * SUBLANES)

        h_hi, h_lo = _split_bf16(h2)
        logits = (_dot(h_hi, rw_hi) + (_dot(h_lo, rw_hi) + _dot(h_hi, rw_lo))) + rb_ref[...]
        gl = jnp.where(lane < N_GROUPS, logits, neg)
        gmax, g = first_max(gl)
        p_g = 1.0 / jnp.sum(jnp.exp(gl - gmax), axis=1, keepdims=True)
        lo = N_GROUPS + EXPERTS_PER_GROUP * g
        el = jnp.where((lane >= lo) & (lane < lo + EXPERTS_PER_GROUP), logits, neg)
        m1, i1 = first_max(el)
        m2, i2 = first_max(jnp.where(lane == i1, neg, el))
        r = jnp.exp(m2 - m1)
        wt1 = p_g / (1.0 + r)
        wt2 = p_g * r / (1.0 + r)
        e1 = i1 - N_GROUPS
        e2 = i2 - N_GROUPS

        onehot = ((lane == e1) | (lane == e2)).astype(BF16)
        before = _dot(tri_ref[...], onehot) + count
        r1 = jnp.sum(jnp.where(lane == e1, before, 0.0), axis=1, keepdims=True)
        r2 = jnp.sum(jnp.where(lane == e2, before, 0.0), axis=1, keepdims=True)

        packed = jnp.zeros((sub, LANES), F32)
        for slot, v in enumerate((e1, e2, wt1, wt2, r1, r2)):
            packed = jnp.where(lane == slot, v, packed)
        route_ref[rows, :] = packed
        rt_ref[:, rows] = packed.T[0:SUBLANES, :]
        return count + jnp.sum(onehot.astype(F32), axis=0, keepdims=True)

    rw_hi, rw_lo = _split_bf16(rw_ref[...])
    count = cnt_ref[...]
    for h in range(tm // sub):
        count = sub_tile(h * sub, count)
    cnt_ref[...] = count


def _merge(f2d, z2d, sg, xa, xb, gt1, sc2, sh2, w_four, w_hyena, w_out, rw, rb, L):
    D = xa.shape[1]
    T = xa.shape[0] + xb.shape[0]
    assert D == SUBLANES * LANES
    tm = TOKEN_TILE
    per_b = L // tm
    n_first = xa.shape[0] // tm
    d_f = f2d.shape[1]
    d_h = z2d.shape[1]
    sub = MERGE_SUBTILE
    tri = jnp.asarray(np.tril(np.ones((sub, sub)), -1), BF16)
    mod_spec = pl.BlockSpec((1, 1, D), lambda i: (i // per_b, 0, 0))
    row = lambda w: pl.BlockSpec((tm, w), lambda i: (i, 0))
    return pl.pallas_call(
        functools.partial(_merge_kernel, n_first),
        grid=(T // tm,),
        in_specs=[row(d_f), row(d_h), row(2 * D)] + _group_specs(n_first, tm, D) + [
            mod_spec, mod_spec, mod_spec,
            _const_spec((d_f, D)), _const_spec((d_h, D)), _const_spec((D, D)),
            _const_spec((D, LANES)), _const_spec((1, LANES)), _const_spec((sub, sub))],
        out_specs=[row(D), pl.BlockSpec((tm * SUBLANES, LANES), lambda i: (i, 0)), row(LANES),
                   pl.BlockSpec((SUBLANES, tm), lambda i: (0, i)), pl.BlockSpec((1, LANES), lambda i: (0, 0))],
        out_shape=[jax.ShapeDtypeStruct((T, D), F32), jax.ShapeDtypeStruct((T * SUBLANES, LANES), F32),
                   jax.ShapeDtypeStruct((T, LANES), F32), jax.ShapeDtypeStruct((SUBLANES, T), F32),
                   jax.ShapeDtypeStruct((1, LANES), F32)],
        compiler_params=_params(("arbitrary",)),
        name="merge",
    )(f2d, z2d, sg, xa, xb, gt1, sc2, sh2, w_four, w_hyena, w_out, rw, rb, tri)


def _to_token_tiles(ref, val, row0=0):
    n = val.shape[0]
    for s in range(SUBLANES):
        ref[pl.ds(row0 + s, n, stride=SUBLANES), :] = val[:, s * LANES:(s + 1) * LANES]


def _from_token_tiles(ref, row0, n):
    return jnp.concatenate([ref[pl.ds(row0 + s, n, stride=SUBLANES), :] for s in range(SUBLANES)], axis=1)


def _gather_start(idx_ref, src_hbm, dst, row0, n_tok, sem):
    def issue(g, carry):
        for u in range(DMA_GROUP):
            r = g * DMA_GROUP + u
            src = pl.multiple_of(idx_ref[0, 0, r] * SUBLANES, SUBLANES)
            row = pl.multiple_of(row0 + r * SUBLANES, SUBLANES)
            copy = pltpu.make_async_copy(src_hbm.at[pl.ds(src, SUBLANES)], dst.at[pl.ds(row, SUBLANES)], sem)
            copy.start(priority=u % 2)
        return carry

    lax.fori_loop(0, n_tok // DMA_GROUP, issue, 0)


def _gather_wait(src_hbm, dst, row0, n_tok, sem):
    rows = n_tok * SUBLANES
    pltpu.make_async_copy(src_hbm.at[pl.ds(0, rows)], dst.at[pl.ds(row0, rows)], sem).wait()


def _dispatch_kernel(pend_ref, dest_ref, h_ref, x_hbm, zero, sem, zsem):
    i = pl.program_id(0)
    tm = dest_ref.shape[2] // TOP_K
    bm = MOE_BLOCK

    def token_rows(ref, tok):
        return ref.at[pl.ds(pl.multiple_of(tok * SUBLANES, SUBLANES), SUBLANES)]

    def drain(n_tok, s):
        rows = n_tok * SUBLANES
        pltpu.make_async_copy(zero.at[pl.ds(0, rows)], x_hbm.at[pl.ds(0, rows)], s).wait()

    @pl.when(i == 0)
    def _():
        zero[...] = jnp.zeros_like(zero)
        n_blocks = x_hbm.shape[0] // (bm * SUBLANES)
        used = pend_ref[N_EXPERTS - 1] // bm

        def zero_block(first_tok):
            row = pl.multiple_of(first_tok * SUBLANES, bm * SUBLANES)
            pltpu.make_async_copy(zero, x_hbm.at[pl.ds(row, bm * SUBLANES)], zsem).start()

        def per_expert(e, carry):
            zero_block(jnp.maximum(pend_ref[e] - bm, 0))
            return carry

        def per_unused(blk, carry):
            zero_block(blk * bm)
            return carry

        def drain_block(blk, carry):
            drain(bm, zsem)
            return carry

        lax.fori_loop(0, N_EXPERTS, per_expert, 0)
        lax.fori_loop(used, n_blocks, per_unused, 0)
        lax.fori_loop(0, N_EXPERTS + n_blocks - used, drain_block, 0)

    for k in range(TOP_K):
        def issue(g, carry, k=k):
            for u in range(DMA_GROUP):
                r = g * DMA_GROUP + u
                dst = token_rows(x_hbm, dest_ref[0, 0, k * tm + r])
                pltpu.make_async_copy(token_rows(h_ref, r), dst, sem).start(priority=u % 2)
            return carry

        lax.fori_loop(0, tm // DMA_GROUP, issue, 0)

    for k in range(TOP_K):
        pltpu.make_async_copy(h_ref, x_hbm.at[pl.ds(0, tm * SUBLANES)], sem).wait()


def _dispatch(pend, dest_tiles, h2, n_rows):
    n, _, tm2 = dest_tiles.shape
    tm = tm2 // TOP_K
    grid_spec = pltpu.PrefetchScalarGridSpec(
        num_scalar_prefetch=1,
        grid=(n,),
        in_specs=[pl.BlockSpec((1, 1, tm2), lambda i, pe: (i, 0, 0), memory_space=pltpu.SMEM),
                  pl.BlockSpec((tm * SUBLANES, LANES), lambda i, pe: (i, 0))],
        out_specs=pl.BlockSpec(memory_space=pl.ANY),
        scratch_shapes=[pltpu.VMEM((MOE_BLOCK * SUBLANES, LANES), F32), pltpu.SemaphoreType.DMA(()),
                        pltpu.SemaphoreType.DMA(())],
    )
    return pl.pallas_call(
        _dispatch_kernel,
        grid_spec=grid_spec,
        out_shape=jax.ShapeDtypeStruct((n_rows * SUBLANES, LANES), F32),
        compiler_params=_params(("arbitrary",)),
        name="dispatch",
    )(pend, dest_tiles, h2)


def _expert_kernel(blk_e_ref, nused_ref, x_ref, wg_ref, wu_ref, wd_ref, y_ref, wg16, wu16, wd16):
    i = pl.program_id(0)
    nused = nused_ref[0]
    bm = MOE_BLOCK

    @pl.when((i == 0) | (blk_e_ref[i] != blk_e_ref[jnp.maximum(i - 1, 0)]))
    def _():
        wg16[...] = wg_ref[0].astype(BF16)
        wu16[...] = wu_ref[0].astype(BF16)
        wd16[...] = wd_ref[0].astype(BF16)

    @pl.when(i < nused)
    def _():
        x = _from_token_tiles(x_ref, 0, bm).astype(BF16)
        g = _dot(x, wg16[...])
        u = _dot(x, wu16[...])
        a = (g * jax.nn.sigmoid(g) * u).astype(BF16)
        _to_token_tiles(y_ref, _dot(a, wd16[...]))

    @pl.when(i >= nused)
    def _():
        y_ref[...] = jnp.zeros_like(y_ref)


def _experts(blk_e, nused, xb, wg, wu, wd):
    nb = blk_e.shape[0]
    bm = MOE_BLOCK
    D, de = wg.shape[1:]
    assert D == SUBLANES * LANES and xb.shape == (nb * bm * SUBLANES, LANES)
    grid_spec = pltpu.PrefetchScalarGridSpec(
        num_scalar_prefetch=2,
        grid=(nb,),
        in_specs=[pl.BlockSpec((bm * SUBLANES, LANES), lambda i, be, nu: (jnp.minimum(i, nu[0] - 1), 0)),
                  pl.BlockSpec((1, D, de), lambda i, be, nu: (be[i], 0, 0)),
                  pl.BlockSpec((1, D, de), lambda i, be, nu: (be[i], 0, 0)),
                  pl.BlockSpec((1, de, D), lambda i, be, nu: (be[i], 0, 0))],
        out_specs=pl.BlockSpec((bm * SUBLANES, LANES), lambda i, be, nu: (i, 0)),
        scratch_shapes=[pltpu.VMEM((D, de), BF16), pltpu.VMEM((D, de), BF16), pltpu.VMEM((de, D), BF16)],
    )
    return pl.pallas_call(
        _expert_kernel,
        grid_spec=grid_spec,
        out_shape=jax.ShapeDtypeStruct((nb * bm * SUBLANES, LANES), F32),
        compiler_params=_params(("arbitrary",)),
        name="experts",
    )(blk_e, nused, xb, wg, wu, wd)


def _combine_kernel(n_steps, cur_ref, nxt_ref, y_hbm, x1_ref, route_ref, gt_ref, gf_ref, o_ref, buf, sem):
    i = pl.program_id(0)
    tm = x1_ref.shape[0]
    slot = i % 2
    rows = 2 * tm * SUBLANES

    @pl.when(i == 0)
    def _():
        _gather_start(cur_ref, y_hbm, buf, 0, 2 * tm, sem.at[0])

    @pl.when(i + 1 < n_steps)
    def _():
        _gather_start(nxt_ref, y_hbm, buf, (1 - slot) * rows, 2 * tm, sem.at[1 - slot])

    _gather_wait(y_hbm, buf, slot * rows, 2 * tm, sem.at[slot])
    route = route_ref[...]
    y1 = _from_token_tiles(buf, slot * rows, tm)
    y2 = _from_token_tiles(buf, slot * rows + tm * SUBLANES, tm)
    x = x1_ref[...] + gt_ref[0] * (route[:, 2:3] * y1 + route[:, 3:4] * y2)
    o_ref[...] = x * lax.rsqrt(jnp.mean(x * x, axis=-1, keepdims=True) + EPS) * gf_ref[...]


def _combine(tile0, n_tok, dest_tiles, yb, x1, route, gt2, g_final, L):
    T, D = x1.shape
    tm = TOKEN_TILE
    per_b = L // tm
    n = n_tok // tm
    last = tile0 + n - 1
    smem = lambda f: pl.BlockSpec((1, 1, 2 * tm), lambda i: (f(i), 0, 0), memory_space=pltpu.SMEM)
    return pl.pallas_call(
        functools.partial(_combine_kernel, n),
        grid=(n,),
        in_specs=[smem(lambda i: tile0 + i), smem(lambda i: jnp.minimum(tile0 + i + 1, last)),
                  pl.BlockSpec(memory_space=pl.ANY),
                  pl.BlockSpec((tm, D), lambda i: (tile0 + i, 0)),
                  pl.BlockSpec((tm, LANES), lambda i: (tile0 + i, 0)),
                  pl.BlockSpec((1, 1, D), lambda i: ((tile0 + i) // per_b, 0, 0)),
                  pl.BlockSpec((1, D), lambda i: (0, 0))],
        out_specs=pl.BlockSpec((tm, D), lambda i: (i, 0)),
        out_shape=jax.ShapeDtypeStruct((n_tok, D), F32),
        scratch_shapes=[pltpu.VMEM((2 * 2 * tm * SUBLANES, LANES), F32), pltpu.SemaphoreType.DMA((2,))],
        compiler_params=_params(("arbitrary",)),
        name="combine",
    )(dest_tiles, dest_tiles, yb, x1, route, gt2, g_final.reshape(1, D))


def _encoder(xp, xs, c, p, g_final):
    n_prompt, L, D = xp.shape
    B = n_prompt + xs.shape[0]
    T = B * L
    xa, xb = xp.reshape(-1, D), xs.reshape(-1, D)
    d_f = N_FOURIER_GROUPS * FOURIER_GROUP_DIM
    d_h = (HYENA_ORDER + 1) * D_HYENA
    d_g = 2 * D

    mod = _ada(c, p["w_ada"], p["b_ada"])
    sh1, sc1, gt1, sh2, sc2, gt2 = [mod[:, k * D:(k + 1) * D].reshape(B, 1, D) for k in range(6)]

    wb = jnp.concatenate([p["w_in"][:, :d_f], jnp.broadcast_to(p["b_in"][None, :d_f], (8, d_f))], axis=0)
    fre, fim = _fold_channel_dft(wb)
    w_all = jnp.concatenate([fre[:D], fim[:D], p["w_in"][:, d_f:]], axis=1).astype(BF16)
    b_all = jnp.concatenate([fre[D], fim[D], p["b_in"][d_f:]])[None, :]
    zr, zi, uh, sg = _inproj(xa, xb, sc1, sh1, w_all, b_all, L, d_f, d_h, d_g)

    f = _fourier(zr.reshape(B, L, d_f), zi.reshape(B, L, d_f))

    ktime, ksum = _filt_time(L, p["filt_w1"], p["filt_b1"], p["filt_w2"], p["filt_b2"],
                             p["filt_w3"], p["filt_b3"], p["filt_freq"], p["filt_wout"])
    kf = _filt_fft(L, ktime, ksum)
    uh3 = uh.reshape(B, L, d_h)
    conv_b = p["conv_b"][None, :]
    ncol = D_HYENA // FFT_COLS
    skip = p["hyena_skip"]
    z = _hyena_order(0, uh3, 0, uh3, ncol, p["conv_w"], conv_b, skip[0:1], kf, L)
    z = _hyena_order(1, z, 0, uh3, 2 * ncol, p["conv_w"], conv_b, skip[1:2], kf, L)

    rw = jnp.zeros((D, LANES), F32).at[:, :N_GROUPS].set(p["router_w1"])
    rw = rw.at[:, N_GROUPS:N_GROUPS + N_EXPERTS].set(p["router_w2"])
    rb = jnp.zeros((1, LANES), F32).at[0, :N_GROUPS].set(p["router_b1"])
    rb = rb.at[0, N_GROUPS:N_GROUPS + N_EXPERTS].set(p["router_b2"])
    x1, h2, route, route_t, counts = _merge(f.reshape(T, d_f), z.reshape(T, D_HYENA), sg, xa, xb,
                                            gt1, sc2, sh2, p["w_four"].astype(BF16), p["w_hyena"].astype(BF16),
                                            p["w_out"].astype(BF16), rw, rb, L)

    bm = MOE_BLOCK
    tm = TOKEN_TILE
    nb = (T * TOP_K) // bm + N_EXPERTS
    e = route_t[0:2].astype(jnp.int32)
    rank = route_t[4:6].astype(jnp.int32)
    cnt = counts[0, :N_EXPERTS].astype(jnp.int32)
    pcnt = (cnt + bm - 1) // bm * bm
    pend = jnp.cumsum(pcnt)
    experts = jnp.arange(N_EXPERTS, dtype=jnp.int32)
    dest = rank + jnp.sum(jnp.where(e[..., None] == experts, pend - pcnt, 0), axis=-1)
    blk_row0 = jnp.arange(nb, dtype=jnp.int32)[:, None] * bm
    blk_e = jnp.minimum(jnp.sum((pend[None, :] <= blk_row0).astype(jnp.int32), axis=1), N_EXPERTS - 1)
    nused = (pend[-1] // bm).astype(jnp.int32).reshape(1)
    dest_tiles = dest.reshape(TOP_K, T // tm, tm).transpose(1, 0, 2).reshape(T // tm, 1, TOP_K * tm)
    xb = _dispatch(pend.astype(jnp.int32), dest_tiles, h2, nb * bm)
    yb = _experts(blk_e, nused, xb, p["exp_w_gate"], p["exp_w_up"], p["exp_w_down"])

    t_prompt = n_prompt * L
    outs = []
    for tile0, n_tok in ((0, t_prompt), (t_prompt // tm, T - t_prompt)):
        outs.append(_combine(tile0, n_tok, dest_tiles, yb, x1, route, gt2, g_final, L))
    return outs[0].reshape(xp.shape), outs[1].reshape(xs.shape)


def kernel(x_prompt, x_sample, c_prompt, c_sample, w_ada, b_ada, w_in, b_in, conv_w, conv_b, filt_w1, filt_b1, filt_w2, filt_b2, filt_w3, filt_b3, filt_freq, filt_wout, hyena_skip, w_four, w_hyena, w_out, router_w1, router_b1, router_w2, router_b2, exp_w_gate, exp_w_up, exp_w_down, g_final):
    assert w_ada.shape[0] == 1, "single-layer block"
    assert x_prompt.shape[1:] == x_sample.shape[1:], "both request groups share sequence length and width"
    p = dict(w_ada=w_ada[0], b_ada=b_ada[0], w_in=w_in[0], b_in=b_in[0], conv_w=conv_w[0], conv_b=conv_b[0],
             filt_w1=filt_w1[0], filt_b1=filt_b1[0], filt_w2=filt_w2[0], filt_b2=filt_b2[0],
             filt_w3=filt_w3[0], filt_b3=filt_b3[0], filt_freq=filt_freq[0], filt_wout=filt_wout[0],
             hyena_skip=hyena_skip[0], w_four=w_four[0], w_hyena=w_hyena[0], w_out=w_out[0],
             router_w1=router_w1[0], router_b1=router_b1[0], router_w2=router_w2[0], router_b2=router_b2[0],
             exp_w_gate=exp_w_gate[0], exp_w_up=exp_w_up[0], exp_w_down=exp_w_down[0])
    c = jnp.concatenate([c_prompt, c_sample], axis=0)
    return _encoder(x_prompt, x_sample, c, p, g_final)
```

```python
import functools
import math

import numpy as np
import jax
import jax.numpy as jnp
from jax import lax
from jax.experimental import pallas as pl
from jax.experimental.pallas import tpu as pltpu

F32 = jnp.float32
BF16 = jnp.bfloat16
HI = lax.Precision.HIGHEST

EPS = 1e-6
FFT_MINOR = 64
N_FOURIER_GROUPS = 4
FOURIER_GROUP_DIM = 128
D_HYENA = 512
HYENA_ORDER = 2
POS_BANDS = 16
N_GROUPS = 4
EXPERTS_PER_GROUP = 8
N_EXPERTS = 32
TOP_K = 2
SHORT_DECAY_PCT = 0.3
LONG_DECAY_PCT = 1.5
DECAY_TARGET = 1e-2
LANES = 128
SUBLANES = 8
VMEM_LIMIT = 56 * 1024 * 1024

TOKEN_TILE = 512
MERGE_SUBTILE = 128
MOE_BLOCK = 512
FFT_COLS = LANES
ROW_CHUNK = 256
FFT_UNROLL = 64
DMA_GROUP = 8


def _dot(a, b):
    return jnp.dot(a, b, preferred_element_type=F32)


def _dot_hi(a, b):
    return jnp.dot(a, b, preferred_element_type=F32, precision=HI)


def _split_bf16(x):
    hi = x.astype(BF16)
    return hi, (x - hi.astype(F32)).astype(BF16)


def _dot3(a, b):
    a_hi, a_lo = _split_bf16(a)
    b_hi, b_lo = _split_bf16(b)
    return _dot(a_hi, b_hi) + (_dot(a_lo, b_hi) + _dot(a_hi, b_lo))


def _params(sem=None):
    return pltpu.CompilerParams(dimension_semantics=sem, vmem_limit_bytes=VMEM_LIMIT)


def _const_spec(shape):
    nd = len(shape)
    return pl.BlockSpec(shape, lambda *_: (0,) * nd, pipeline_mode=pl.Buffered(1))


def _pitch(rows):
    p = -(-rows // SUBLANES)
    return SUBLANES * (p if p % 2 else p + 1)


@functools.lru_cache(maxsize=None)
def _stage2_tables():
    i = np.arange(FFT_MINOR)
    ph = 2.0 * np.pi * np.outer(i, i) / FFT_MINOR
    c, s = np.cos(ph), np.sin(ph)
    fwd = np.block([[c, s], [-s, c]])
    inv = np.block([[c, -s], [s, c]])
    return fwd, inv, c, s


@functools.lru_cache(maxsize=None)
def _hyena_tables(L):
    n_fft = 2 * L
    m = FFT_MINOR
    na = n_fft // m
    nh = na // 2
    b = np.arange(m)[:, None, None]
    ka = np.arange(nh + 1)[None, :, None]

    def forward(a_count):
        a = np.arange(a_count)[None, None, :]
        th = 2.0 * np.pi * ((ka * (m * a + b)) % n_fft) / n_fft
        c, s = np.cos(th), -np.sin(th)
        pad = np.zeros((m, SUBLANES - 2, a_count))
        return np.concatenate([c[:, :nh], s[:, :nh], c[:, nh:], s[:, nh:], pad], axis=1)

    fwd_half = forward(nh)
    fwd_full = forward(na)
    weight = np.where(np.arange(nh) == 0, 1.0, 2.0)[None, :, None]
    inv_half = np.transpose(fwd_half[:, :2 * nh] * np.concatenate([weight, weight], axis=1), (0, 2, 1)) / n_fft
    sign = np.broadcast_to(((-1.0) ** np.arange(nh))[:, None] / n_fft, (nh, LANES))
    ph = np.pi * np.arange(m) / m
    cos_b = np.broadcast_to(np.cos(ph)[:, None], (m, LANES))
    sin_b = np.broadcast_to(np.sin(ph)[:, None], (m, LANES))
    return fwd_half, inv_half, fwd_full, sign, cos_b, sin_b


@functools.lru_cache(maxsize=None)
def _fourier_tables(L):
    na = L // FFT_MINOR
    b = np.arange(FFT_MINOR)[:, None, None]
    ka = np.arange(na)[None, :, None]
    a = np.arange(na)[None, None, :]
    th = 2.0 * np.pi * ((ka * (FFT_MINOR * a + b)) % L) / L
    c, s = np.cos(th), np.sin(th)
    st1 = np.concatenate([np.concatenate([c, s], axis=2),
                          np.concatenate([-s, c], axis=2)], axis=1)
    _, _, c2, s2 = _stage2_tables()
    st2 = np.concatenate([c2, s2], axis=1) / math.sqrt(L)
    return st1, st2


def _mxu_table(table):
    return jnp.asarray(table, F32).astype(BF16)


@functools.lru_cache(maxsize=None)
def _channel_dft():
    i = np.arange(FOURIER_GROUP_DIM)
    ph = 2.0 * np.pi * np.outer(i, i) / FOURIER_GROUP_DIM
    return np.concatenate([np.cos(ph), -np.sin(ph)], axis=1) / math.sqrt(FOURIER_GROUP_DIM)


def _ada_kernel(c_ref, w_ref, b_ref, o_ref):
    c = c_ref[...]
    o_ref[...] = _dot_hi(c * jax.nn.sigmoid(c), w_ref[...]) + b_ref[...]


def _ada(c, w_ada, b_ada):
    nb, d = c.shape
    n = w_ada.shape[1]
    tn = 1536
    return pl.pallas_call(
        _ada_kernel,
        grid=(n // tn,),
        in_specs=[pl.BlockSpec((nb, d), lambda j: (0, 0)),
                  pl.BlockSpec((d, tn), lambda j: (0, j)),
                  pl.BlockSpec((1, tn), lambda j: (0, j))],
        out_specs=pl.BlockSpec((nb, tn), lambda j: (0, j)),
        out_shape=jax.ShapeDtypeStruct((nb, n), F32),
        compiler_params=_params(("arbitrary",)),
        name="ada",
    )(c, w_ada, b_ada.reshape(1, n))


def _fold_kernel(w_ref, f_ref, re_ref, im_ref):
    z = _dot_hi(w_ref[...], f_ref[...])
    re_ref[...] = z[:, :FOURIER_GROUP_DIM]
    im_ref[...] = z[:, FOURIER_GROUP_DIM:]


def _fold_channel_dft(wb):
    rows = wb.shape[0]
    gd = FOURIER_GROUP_DIM
    fmat = jnp.asarray(_channel_dft(), F32)
    return pl.pallas_call(
        _fold_kernel,
        grid=(N_FOURIER_GROUPS,),
        in_specs=[pl.BlockSpec((rows, gd), lambda g: (0, g)),
                  pl.BlockSpec((gd, 2 * gd), lambda g: (0, 0))],
        out_specs=[pl.BlockSpec((rows, gd), lambda g: (0, g)),
                   pl.BlockSpec((rows, gd), lambda g: (0, g))],
        out_shape=[jax.ShapeDtypeStruct((rows, N_FOURIER_GROUPS * gd), F32)] * 2,
        compiler_params=_params(("arbitrary",)),
        name="fold",
    )(wb, fmat)


def _for_row_chunks(n_rows, body):
    def step(j, carry):
        body(j, pl.multiple_of(j * ROW_CHUNK, ROW_CHUNK))
        return carry

    lax.fori_loop(0, n_rows // ROW_CHUNK, step, 0)


def _store_padded(dst, j, val, run=FFT_MINOR):
    pz = _pitch(run)
    per = ROW_CHUNK // run
    for i in range(per):
        dst[pl.ds(pl.multiple_of((j * per + i) * pz, SUBLANES), run), :] = val[i * run:(i + 1) * run]


def _load_padded(src, j, run=FFT_MINOR):
    pz = _pitch(run)
    per = ROW_CHUNK // run
    return jnp.concatenate([src[pl.ds(pl.multiple_of((j * per + i) * pz, SUBLANES), run), :] for i in range(per)],
                           axis=0)


def _short_conv_chunk(src_ref, w_ref, b_ref, L, j, r0):
    R = ROW_CHUNK
    ct = src_ref.shape[-1]
    row = lax.broadcasted_iota(jnp.int32, (R, ct), 0)
    cur = src_ref[0, pl.ds(r0, R), :]
    before = src_ref[0, pl.ds(jnp.maximum(r0 - 1, 0), 1), :] * jnp.where(j > 0, 1.0, 0.0)
    after = src_ref[0, pl.ds(jnp.minimum(r0 + R, L - 1), 1), :] * jnp.where(j < L // R - 1, 1.0, 0.0)
    up = jnp.where(row == 0, before, pltpu.roll(cur, 1, 0))
    dn = jnp.where(row == R - 1, after, pltpu.roll(cur, R - 1, 0))
    return w_ref[0:1, :] * up + w_ref[1:2, :] * cur + w_ref[2:3, :] * dn + b_ref[...]


def _filt_time_kernel(L, rows, bands_ref, delt_ref, w1_ref, b1_ref, w2_ref, b2_ref, w3_ref, b3_ref,
                      fr_ref, wo_ref, k_ref, sum_ref):
    i = pl.program_id(0)
    n = i * rows + lax.broadcasted_iota(jnp.int32, (rows, 1), 0)
    fwd = n < L
    pos = jnp.where(fwd, n, 2 * L - n).astype(F32)
    t = pos * (1.0 / (L - 1))
    ang = (2.0 * math.pi / L) * pos * bands_ref[...]
    w1 = w1_ref[...]
    pre = (t * w1[0:1, :] + _dot3(jnp.cos(ang), w1[1:1 + POS_BANDS, :])
           + _dot3(-jnp.sin(ang), w1[1 + POS_BANDS:, :]) + b1_ref[...])
    fr = fr_ref[...]
    h = jnp.sin(fr[0:1, :] * pre)
    h = jnp.sin(fr[1:2, :] * (_dot3(h, w2_ref[...]) + b2_ref[...]))
    h = jnp.sin(fr[2:3, :] * (_dot3(h, w3_ref[...]) + b3_ref[...]))
    window = jnp.exp(-t * delt_ref[...])
    live = n != L
    tile_fwd = i < (L // rows)
    parts = []
    for o in range(HYENA_ORDER):
        base = o * 2 * D_HYENA
        w = jnp.where(tile_fwd, wo_ref[:, base:base + D_HYENA], wo_ref[:, base + D_HYENA:base + 2 * D_HYENA])
        parts.append(jnp.where(live, _dot3(h, w) * window, 0.0))
    k = jnp.concatenate(parts, axis=1)
    k_ref[...] = k

    @pl.when(i == 0)
    def _():
        sum_ref[...] = jnp.zeros_like(sum_ref)

    sum_ref[...] += jnp.sum(jnp.abs(k), axis=0, keepdims=True)


def _filt_time(L, w1, b1, w2, b2, w3, b3, freq, wout):
    rows = 512
    assert L % rows == 0, "each row tile must hold one filter direction only"
    n_fft = 2 * L
    cols = HYENA_ORDER * D_HYENA
    bands = jnp.linspace(1e-4, POS_BANDS - 1, POS_BANDS, dtype=F32).reshape(1, POS_BANDS)
    max_decay = math.log(DECAY_TARGET) / SHORT_DECAY_PCT
    min_decay = math.log(DECAY_TARGET) / LONG_DECAY_PCT
    deltas = jnp.abs(jnp.linspace(min_decay, max_decay, D_HYENA, dtype=F32)).reshape(1, D_HYENA)
    args = (bands, deltas, w1, b1.reshape(1, -1), w2, b2.reshape(1, -1), w3, b3.reshape(1, -1), freq, wout)
    return pl.pallas_call(
        functools.partial(_filt_time_kernel, L, rows),
        grid=(n_fft // rows,),
        in_specs=[pl.BlockSpec(a.shape, lambda i: (0, 0)) for a in args],
        out_specs=[pl.BlockSpec((rows, cols), lambda i: (i, 0)),
                   pl.BlockSpec((1, cols), lambda i: (0, 0))],
        out_shape=[jax.ShapeDtypeStruct((n_fft, cols), F32),
                   jax.ShapeDtypeStruct((1, cols), F32)],
        compiler_params=_params(("arbitrary",)),
        name="filt_time",
    )(*args)


def _filt_fft_kernel(na, k_ref, sum_ref, st1_ref, st2_ref, o_ref, kp, s1):
    m = FFT_MINOR
    nh = na // 2
    rows1 = 2 * nh + SUBLANES
    pz, p1 = _pitch(m), _pitch(rows1)
    _for_row_chunks(na * m, lambda j, r0: _store_padded(kp, j, k_ref[pl.ds(r0, ROW_CHUNK), :]))

    def stage1(b, carry):
        slab = kp[pl.ds(b, na, stride=pz), :]
        s1[pl.ds(pl.multiple_of(b * p1, SUBLANES), rows1), :] = _dot3(st1_ref[b], slab)
        return carry

    lax.fori_loop(0, m, stage1, 0, unroll=4)
    inv_norm = 1.0 / sum_ref[...]

    def spectrum_row(re_row, im_row):
        a = jnp.concatenate([s1[pl.ds(re_row, m, stride=p1), :], s1[pl.ds(im_row, m, stride=p1), :]], axis=0)
        return _dot3(st2_ref[...], a) * inv_norm

    def stage2(ka, carry):
        o_ref[ka] = spectrum_row(ka, nh + ka)
        return carry

    lax.fori_loop(0, nh, stage2, 0, unroll=4)
    o_ref[nh] = spectrum_row(2 * nh, 2 * nh + 1)


def _filt_fft(L, ktime, ksum):
    n_fft, cols = ktime.shape
    m = FFT_MINOR
    na = n_fft // m
    nh = na // 2
    rows1 = 2 * nh + SUBLANES
    ct = FFT_COLS
    fwd_full = _hyena_tables(L)[2]
    st2f = _stage2_tables()[0]
    return pl.pallas_call(
        functools.partial(_filt_fft_kernel, na),
        grid=(cols // ct,),
        in_specs=[pl.BlockSpec((n_fft, ct), lambda j: (0, j)),
                  pl.BlockSpec((1, ct), lambda j: (0, j)),
                  _const_spec((m, rows1, na)),
                  _const_spec((2 * m, 2 * m))],
        out_specs=pl.BlockSpec((nh + 1, 2 * m, ct), lambda j: (0, 0, j)),
        out_shape=jax.ShapeDtypeStruct((nh + 1, 2 * m, cols), F32),
        scratch_shapes=[pltpu.VMEM((na * _pitch(m), ct), F32),
                        pltpu.VMEM((m * _pitch(rows1), ct), F32)],
        compiler_params=_params(("arbitrary",)),
        name="filt_fft",
    )(ktime, ksum, jnp.asarray(fwd_full, F32), jnp.asarray(st2f, F32))


def _rms_mod(x, scale, shift):
    y = x * lax.rsqrt(jnp.mean(x * x, axis=-1, keepdims=True) + EPS)
    return y * (1.0 + scale) + shift


def _group_tile(n_first, a_ref, b_ref):
    return jnp.where(pl.program_id(0) < n_first, a_ref[...], b_ref[...])


def _group_specs(n_first, tm, D):
    return [pl.BlockSpec((tm, D), lambda i: (jnp.minimum(i, n_first - 1), 0)),
            pl.BlockSpec((tm, D), lambda i: (jnp.maximum(i - n_first, 0), 0))]


def _inproj_kernel(d_f, d_h, n_first, xa_ref, xb_ref, sc_ref, sh_ref, w_ref, b_ref, zr_ref, zi_ref, uh_ref, sg_ref):
    h = _rms_mod(_group_tile(n_first, xa_ref, xb_ref), sc_ref[0], sh_ref[0]).astype(BF16)

    def proj(c0, width):
        return _dot(h, w_ref[:, c0:c0 + width]) + b_ref[:, c0:c0 + width]

    zr_ref[...] = proj(0, d_f)
    zi_ref[...] = proj(d_f, d_f)
    chunk = 512
    for c in range(0, d_h, chunk):
        uh_ref[:, c:c + chunk] = proj(2 * d_f + c, chunk)
    d_g = sg_ref.shape[1]
    for c in range(0, d_g, chunk):
        sg_ref[:, c:c + chunk] = jax.nn.sigmoid(proj(2 * d_f + d_h + c, chunk)).astype(BF16)


def _inproj(xa, xb, sc, sh, w, b, L, d_f, d_h, d_g):
    D = xa.shape[1]
    T = xa.shape[0] + xb.shape[0]
    tm = TOKEN_TILE
    per_b = L // tm
    n_first = xa.shape[0] // tm
    n = w.shape[1]
    mod_spec = pl.BlockSpec((1, 1, D), lambda i: (i // per_b, 0, 0))
    return pl.pallas_call(
        functools.partial(_inproj_kernel, d_f, d_h, n_first),
        grid=(T // tm,),
        in_specs=_group_specs(n_first, tm, D) + [mod_spec, mod_spec, _const_spec((D, n)), _const_spec((1, n))],
        out_specs=[pl.BlockSpec((tm, d_f), lambda i: (i, 0)),
                   pl.BlockSpec((tm, d_f), lambda i: (i, 0)),
                   pl.BlockSpec((tm, d_h), lambda i: (i, 0)),
                   pl.BlockSpec((tm, d_g), lambda i: (i, 0))],
        out_shape=[jax.ShapeDtypeStruct((T, d_f), F32), jax.ShapeDtypeStruct((T, d_f), F32),
                   jax.ShapeDtypeStruct((T, d_h), F32), jax.ShapeDtypeStruct((T, d_g), BF16)],
        compiler_params=_params(("parallel",)),
        name="inproj",
    )(xa, xb, sc, sh, w, b)


def _fourier_kernel(na, zr_ref, zi_ref, st1_ref, st2_ref, o_ref, zpr, zpi, s1, op):
    m = FFT_MINOR
    pz, p1 = _pitch(m), _pitch(2 * na)

    def fill(j, r0):
        _store_padded(zpr, j, zr_ref[0, pl.ds(r0, ROW_CHUNK), :])
        _store_padded(zpi, j, zi_ref[0, pl.ds(r0, ROW_CHUNK), :])

    _for_row_chunks(na * m, fill)

    def stage1(b, carry):
        z = jnp.concatenate([zpr[pl.ds(b, na, stride=pz), :], zpi[pl.ds(b, na, stride=pz), :]], axis=0)
        s1[pl.ds(pl.multiple_of(b * p1, SUBLANES), 2 * na), :] = _dot(st1_ref[b], z.astype(BF16))
        return carry

    lax.fori_loop(0, m, stage1, 0, unroll=FFT_UNROLL)

    po = _pitch(na)

    def stage2(ka, carry):
        a = jnp.concatenate([s1[pl.ds(ka, m, stride=p1), :], s1[pl.ds(na + ka, m, stride=p1), :]], axis=0)
        op[pl.ds(ka, m, stride=po), :] = _dot(st2_ref[...], a.astype(BF16))
        return carry

    lax.fori_loop(0, na, stage2, 0, unroll=FFT_UNROLL)

    def unpad(j, r0):
        o_ref[0, pl.ds(r0, ROW_CHUNK), :] = _load_padded(op, j, na)

    _for_row_chunks(na * m, unpad)


def _fourier(zr, zi):
    B, L, C = zr.shape
    m = FFT_MINOR
    na = L // m
    ct = FFT_COLS
    st1, st2 = _fourier_tables(L)
    spec = pl.BlockSpec((1, L, ct), lambda j, b: (b, 0, j))
    pad = pltpu.VMEM((na * _pitch(m), ct), F32)
    return pl.pallas_call(
        functools.partial(_fourier_kernel, na),
        grid=(C // ct, B),
        in_specs=[spec, spec, _const_spec((m, 2 * na, 2 * na)), _const_spec((m, 2 * m))],
        out_specs=spec,
        out_shape=jax.ShapeDtypeStruct((B, L, C), F32),
        scratch_shapes=[pad, pad, pltpu.VMEM((m * _pitch(2 * na), ct), F32),
                        pltpu.VMEM((m * _pitch(na), ct), F32)],
        compiler_params=_params(("parallel", "parallel")),
        name="fourier",
    )(zr, zi, _mxu_table(st1), _mxu_table(st2))


def _hyena_kernel(conv_z, L, zin_ref, gin_ref, cwz_ref, cbz_ref, cwg_ref, cbg_ref, skip_ref, kf_ref,
                  fwd1_ref, inv1_ref, st2f_ref, st2i_ref, sign_ref, cosb_ref, sinb_ref, out_ref, zp, yp, s1, g2, hh):
    m = FFT_MINOR
    nh = L // m
    rows1 = 2 * nh + SUBLANES
    pz, p1, p2 = _pitch(m), _pitch(rows1), _pitch(2 * m)

    def fill(j, r0):
        if conv_z:
            _store_padded(zp, j, _short_conv_chunk(zin_ref, cwz_ref, cbz_ref, L, j, r0))
        else:
            _store_padded(zp, j, zin_ref[0, pl.ds(r0, ROW_CHUNK), :])

    _for_row_chunks(L, fill)

    def stage1(b, carry):
        slab = zp[pl.ds(b, nh, stride=pz), :].astype(BF16)
        s1[pl.ds(pl.multiple_of(b * p1, SUBLANES), rows1), :] = _dot(fwd1_ref[b], slab)
        return carry

    lax.fori_loop(0, m, stage1, 0, unroll=FFT_UNROLL)

    def spectrum_row(ka, re_row, im_row):
        a = jnp.concatenate([s1[pl.ds(re_row, m, stride=p1), :], s1[pl.ds(im_row, m, stride=p1), :]], axis=0)
        x = _dot(st2f_ref[...], a.astype(BF16))
        xr, xi = x[:m], x[m:]
        k = kf_ref[ka]
        kr, ki = k[:m], k[m:]
        y = jnp.concatenate([xr * kr - xi * ki, xr * ki + xi * kr], axis=0).astype(BF16)
        return _dot(st2i_ref[...], y)

    def stage2(ka, carry):
        g2[pl.ds(pl.multiple_of(ka * p2, SUBLANES), 2 * m), :] = spectrum_row(ka, ka, nh + ka)
        return carry

    lax.fori_loop(0, nh, stage2, 0, unroll=FFT_UNROLL)
    g_mid = spectrum_row(nh, 2 * nh, 2 * nh + 1)
    hh[...] = cosb_ref[...] * g_mid[:m] - sinb_ref[...] * g_mid[m:]
    sign = sign_ref[...]

    def stage3(b, carry):
        g = jnp.concatenate([g2[pl.ds(b, nh, stride=p2), :], g2[pl.ds(m + b, nh, stride=p2), :]], axis=0)
        yp[pl.ds(b, nh, stride=pz), :] = _dot(inv1_ref[b], g.astype(BF16)) + sign * hh[pl.ds(b, 1), :]
        return carry

    lax.fori_loop(0, m, stage3, 0, unroll=FFT_UNROLL)
    skip = skip_ref[...]

    def finish(j, r0):
        gate = _short_conv_chunk(gin_ref, cwg_ref, cbg_ref, L, j, r0)
        out_ref[0, pl.ds(r0, ROW_CHUNK), :] = gate * (_load_padded(yp, j) + skip * _load_padded(zp, j))

    _for_row_chunks(L, finish)


def _hyena_order(order, zin, z_col0, uh, g_col0, conv_w, conv_b, skip, kf, L):
    B = uh.shape[0]
    m = FFT_MINOR
    ct = FFT_COLS
    ncol = D_HYENA // ct
    nh = L // m
    rows1 = 2 * nh + SUBLANES
    conv_z = order == 0
    fwd_half, inv_half, _, sign, cos_b, sin_b = _hyena_tables(L)
    st2f, st2i, _, _ = _stage2_tables()
    z_cols = (lambda j, b: (b, 0, z_col0 + j))
    g_cols = (lambda j, b: (b, 0, g_col0 + j))
    zw_col0 = z_col0 if conv_z else 0
    pad = pltpu.VMEM((nh * _pitch(m), ct), F32)
    return pl.pallas_call(
        functools.partial(_hyena_kernel, conv_z, L),
        grid=(ncol, B),
        in_specs=[pl.BlockSpec((1, L, ct), z_cols),
                  pl.BlockSpec((1, L, ct), g_cols),
                  pl.BlockSpec((3, ct), lambda j, b: (0, zw_col0 + j)),
                  pl.BlockSpec((1, ct), lambda j, b: (0, zw_col0 + j)),
                  pl.BlockSpec((3, ct), lambda j, b: (0, g_col0 + j)),
                  pl.BlockSpec((1, ct), lambda j, b: (0, g_col0 + j)),
                  pl.BlockSpec((1, ct), lambda j, b: (0, j)),
                  pl.BlockSpec((nh + 1, 2 * m, ct), lambda j, b: (0, 0, order * ncol + j),
                               pipeline_mode=pl.Buffered(1)),
                  _const_spec((m, rows1, nh)),
                  _const_spec((m, nh, 2 * nh)),
                  _const_spec((2 * m, 2 * m)),
                  _const_spec((2 * m, 2 * m)),
                  _const_spec((nh, LANES)), _const_spec((m, LANES)), _const_spec((m, LANES))],
        out_specs=pl.BlockSpec((1, L, ct), lambda j, b: (b, 0, j)),
        out_shape=jax.ShapeDtypeStruct((B, L, D_HYENA), F32),
        scratch_shapes=[pad, pad,
                        pltpu.VMEM((m * _pitch(rows1), ct), F32),
                        pltpu.VMEM((nh * _pitch(2 * m), ct), F32),
                        pltpu.VMEM((m, ct), F32)],
        compiler_params=_params(("parallel", "arbitrary")),
        name=f"hyena{order}",
    )(zin, uh, conv_w, conv_b, conv_w, conv_b, skip, kf,
      _mxu_table(fwd_half), _mxu_table(inv_half), _mxu_table(st2f), _mxu_table(st2i),
      jnp.asarray(sign, F32), jnp.asarray(cos_b, F32), jnp.asarray(sin_b, F32))


def _merge_kernel(n_first, f_ref, z_ref, sg_ref, xa_ref, xb_ref, gt_ref, sc_ref, sh_ref, wf_ref, wh_ref, wo_ref,
                  rw_ref, rb_ref, tri_ref, x1_ref, h2_ref, route_ref, rt_ref, cnt_ref):
    i = pl.program_id(0)
    tm, D = xa_ref.shape
    sub = MERGE_SUBTILE
    lane = lax.broadcasted_iota(jnp.int32, (sub, LANES), 1).astype(F32)
    neg = -1e30

    def first_max(v):
        mx = jnp.max(v, axis=1, keepdims=True)
        return mx, jnp.min(jnp.where(v == mx, lane, float(LANES)), axis=1, keepdims=True)

    @pl.when(i == 0)
    def _():
        cnt_ref[...] = jnp.zeros_like(cnt_ref)

    def sub_tile(r0, count):
        rows = pl.ds(r0, sub)
        ya = _dot(f_ref[rows, :].astype(BF16), wf_ref[...])
        yb = _dot(z_ref[rows, :].astype(BF16), wh_ref[...])
        merged = sg_ref[rows, :D].astype(F32) * ya + sg_ref[rows, D:].astype(F32) * yb
        x = jnp.where(i < n_first, xa_ref[rows, :], xb_ref[rows, :])
        x1 = x + gt_ref[0] * _dot(merged.astype(BF16), wo_ref[...])
        x1_ref[rows, :] = x1
        h2 = _rms_mod(x1, sc_ref[0], sh_ref[0])
        _to_token_tiles(h2_ref, h2, r0 * SUBLANES)

        h_hi, h_lo = _split_bf16(h2)
        logits = (_dot(h_hi, rw_hi) + (_dot(h_lo, rw_hi) + _dot(h_hi, rw_lo))) + rb_ref[...]
        gl = jnp.where(lane < N_GROUPS, logits, neg)
        gmax, g = first_max(gl)
        p_g = 1.0 / jnp.sum(jnp.exp(gl - gmax), axis=1, keepdims=True)
        lo = N_GROUPS + EXPERTS_PER_GROUP * g
        el = jnp.where((lane >= lo) & (lane < lo + EXPERTS_PER_GROUP), logits, neg)
        m1, i1 = first_max(el)
        m2, i2 = first_max(jnp.where(lane == i1, neg, el))
        r = jnp.exp(m2 - m1)
        wt1 = p_g / (1.0 + r)
        wt2 = p_g * r / (1.0 + r)
        e1 = i1 - N_GROUPS
        e2 = i2 - N_GROUPS

        onehot = ((lane == e1) | (lane == e2)).astype(BF16)
        before = _dot(tri_ref[...], onehot) + count
        r1 = jnp.sum(jnp.where(lane == e1, before, 0.0), axis=1, keepdims=True)
        r2 = jnp.sum(jnp.where(lane == e2, before, 0.0), axis=1, keepdims=True)

        packed = jnp.zeros((sub, LANES), F32)
        for slot, v in enumerate((e1, e2, wt1, wt2, r1, r2)):
            packed = jnp.where(lane == slot, v, packed)
        route_ref[rows, :] = packed
        rt_ref[:, rows] = packed.T[0:SUBLANES, :]
        return count + jnp.sum(onehot.astype(F32), axis=0, keepdims=True)

    rw_hi, rw_lo = _split_bf16(rw_ref[...])
    count = cnt_ref[...]
    for h in range(tm // sub):
        count = sub_tile(h * sub, count)
    cnt_ref[...] = count


def _merge(f2d, z2d, sg, xa, xb, gt1, sc2, sh2, w_four, w_hyena, w_out, rw, rb, L):
    D = xa.shape[1]
    T = xa.shape[0] + xb.shape[0]
    assert D == SUBLANES * LANES
    tm = TOKEN_TILE
    per_b = L // tm
    n_first = xa.shape[0] // tm
    d_f = f2d.shape[1]
    d_h = z2d.shape[1]
    sub = MERGE_SUBTILE
    tri = jnp.asarray(np.tril(np.ones((sub, sub)), -1), BF16)
    mod_spec = pl.BlockSpec((1, 1, D), lambda i: (i // per_b, 0, 0))
    row = lambda w: pl.BlockSpec((tm, w), lambda i: (i, 0))
    return pl.pallas_call(
        functools.partial(_merge_kernel, n_first),
        grid=(T // tm,),
        in_specs=[row(d_f), row(d_h), row(2 * D)] + _group_specs(n_first, tm, D) + [
            mod_spec, mod_spec, mod_spec,
            _const_spec((d_f, D)), _const_spec((d_h, D)), _const_spec((D, D)),
            _const_spec((D, LANES)), _const_spec((1, LANES)), _const_spec((sub, sub))],
        out_specs=[row(D), pl.BlockSpec((tm * SUBLANES, LANES), lambda i: (i, 0)), row(LANES),
                   pl.BlockSpec((SUBLANES, tm), lambda i: (0, i)), pl.BlockSpec((1, LANES), lambda i: (0, 0))],
        out_shape=[jax.ShapeDtypeStruct((T, D), F32), jax.ShapeDtypeStruct((T * SUBLANES, LANES), F32),
                   jax.ShapeDtypeStruct((T, LANES), F32), jax.ShapeDtypeStruct((SUBLANES, T), F32),
                   jax.ShapeDtypeStruct((1, LANES), F32)],
        compiler_params=_params(("arbitrary",)),
        name="merge",
    )(f2d, z2d, sg, xa, xb, gt1, sc2, sh2, w_four, w_hyena, w_out, rw, rb, tri)


def _to_token_tiles(ref, val, row0=0):
    n = val.shape[0]
    for s in range(SUBLANES):
        ref[pl.ds(row0 + s, n, stride=SUBLANES), :] = val[:, s * LANES:(s + 1) * LANES]


def _from_token_tiles(ref, row0, n):
    return jnp.concatenate([ref[pl.ds(row0 + s, n, stride=SUBLANES), :] for s in range(SUBLANES)], axis=1)


def _gather_start(idx_ref, src_hbm, dst, row0, n_tok, sem):
    def issue(g, carry):
        for u in range(DMA_GROUP):
            r = g * DMA_GROUP + u
            src = pl.multiple_of(idx_ref[0, 0, r] * SUBLANES, SUBLANES)
            row = pl.multiple_of(row0 + r * SUBLANES, SUBLANES)
            copy = pltpu.make_async_copy(src_hbm.at[pl.ds(src, SUBLANES)], dst.at[pl.ds(row, SUBLANES)], sem)
            copy.start(priority=u % 2)
        return carry

    lax.fori_loop(0, n_tok // DMA_GROUP, issue, 0)


def _gather_wait(src_hbm, dst, row0, n_tok, sem):
    rows = n_tok * SUBLANES
    pltpu.make_async_copy(src_hbm.at[pl.ds(0, rows)], dst.at[pl.ds(row0, rows)], sem).wait()


def _dispatch_kernel(pend_ref, dest_ref, h_ref, x_hbm, zero, sem, zsem):
    i = pl.program_id(0)
    tm = dest_ref.shape[2] // TOP_K
    bm = MOE_BLOCK

    def token_rows(ref, tok):
        return ref.at[pl.ds(pl.multiple_of(tok * SUBLANES, SUBLANES), SUBLANES)]

    def drain(n_tok, s):
        rows = n_tok * SUBLANES
        pltpu.make_async_copy(zero.at[pl.ds(0, rows)], x_hbm.at[pl.ds(0, rows)], s).wait()

    @pl.when(i == 0)
    def _():
        zero[...] = jnp.zeros_like(zero)
        n_blocks = x_hbm.shape[0] // (bm * SUBLANES)
        used = pend_ref[N_EXPERTS - 1] // bm

        def zero_block(first_tok):
            row = pl.multiple_of(first_tok * SUBLANES, bm * SUBLANES)
            pltpu.make_async_copy(zero, x_hbm.at[pl.ds(row, bm * SUBLANES)], zsem).start()

        def per_expert(e, carry):
            zero_block(jnp.maximum(pend_ref[e] - bm, 0))
            return carry

        def per_unused(blk, carry):
            zero_block(blk * bm)
            return carry

        def drain_block(blk, carry):
            drain(bm, zsem)
            return carry

        lax.fori_loop(0, N_EXPERTS, per_expert, 0)
        lax.fori_loop(used, n_blocks, per_unused, 0)
        lax.fori_loop(0, N_EXPERTS + n_blocks - used, drain_block, 0)

    for k in range(TOP_K):
        def issue(g, carry, k=k):
            for u in range(DMA_GROUP):
                r = g * DMA_GROUP + u
                dst = token_rows(x_hbm, dest_ref[0, 0, k * tm + r])
                pltpu.make_async_copy(token_rows(h_ref, r), dst, sem).start(priority=u % 2)
            return carry

        lax.fori_loop(0, tm // DMA_GROUP, issue, 0)

    for k in range(TOP_K):
        pltpu.make_async_copy(h_ref, x_hbm.at[pl.ds(0, tm * SUBLANES)], sem).wait()


def _dispatch(pend, dest_tiles, h2, n_rows):
    n, _, tm2 = dest_tiles.shape
    tm = tm2 // TOP_K
    grid_spec = pltpu.PrefetchScalarGridSpec(
        num_scalar_prefetch=1,
        grid=(n,),
        in_specs=[pl.BlockSpec((1, 1, tm2), lambda i, pe: (i, 0, 0), memory_space=pltpu.SMEM),
                  pl.BlockSpec((tm * SUBLANES, LANES), lambda i, pe: (i, 0))],
        out_specs=pl.BlockSpec(memory_space=pl.ANY),
        scratch_shapes=[pltpu.VMEM((MOE_BLOCK * SUBLANES, LANES), F32), pltpu.SemaphoreType.DMA(()),
                        pltpu.SemaphoreType.DMA(())],
    )
    return pl.pallas_call(
        _dispatch_kernel,
        grid_spec=grid_spec,
        out_shape=jax.ShapeDtypeStruct((n_rows * SUBLANES, LANES), F32),
        compiler_params=_params(("arbitrary",)),
        name="dispatch",
    )(pend, dest_tiles, h2)


def _expert_kernel(blk_e_ref, nused_ref, x_ref, wg_ref, wu_ref, wd_ref, y_ref, wg16, wu16, wd16):
    i = pl.program_id(0)
    nused = nused_ref[0]
    bm = MOE_BLOCK

    @pl.when((i == 0) | (blk_e_ref[i] != blk_e_ref[jnp.maximum(i - 1, 0)]))
    def _():
        wg16[...] = wg_ref[0].astype(BF16)
        wu16[...] = wu_ref[0].astype(BF16)
        wd16[...] = wd_ref[0].astype(BF16)

    @pl.when(i < nused)
    def _():
        x = _from_token_tiles(x_ref, 0, bm).astype(BF16)
        g = _dot(x, wg16[...])
        u = _dot(x, wu16[...])
        a = (g * jax.nn.sigmoid(g) * u).astype(BF16)
        _to_token_tiles(y_ref, _dot(a, wd16[...]))

    @pl.when(i >= nused)
    def _():
        y_ref[...] = jnp.zeros_like(y_ref)


def _experts(blk_e, nused, xb, wg, wu, wd):
    nb = blk_e.shape[0]
    bm = MOE_BLOCK
    D, de = wg.shape[1:]
    assert D == SUBLANES * LANES and xb.shape == (nb * bm * SUBLANES, LANES)
    grid_spec = pltpu.PrefetchScalarGridSpec(
        num_scalar_prefetch=2,
        grid=(nb,),
        in_specs=[pl.BlockSpec((bm * SUBLANES, LANES), lambda i, be, nu: (jnp.minimum(i, nu[0] - 1), 0)),
                  pl.BlockSpec((1, D, de), lambda i, be, nu: (be[i], 0, 0)),
                  pl.BlockSpec((1, D, de), lambda i, be, nu: (be[i], 0, 0)),
                  pl.BlockSpec((1, de, D), lambda i, be, nu: (be[i], 0, 0))],
        out_specs=pl.BlockSpec((bm * SUBLANES, LANES), lambda i, be, nu: (i, 0)),
        scratch_shapes=[pltpu.VMEM((D, de), BF16), pltpu.VMEM((D, de), BF16), pltpu.VMEM((de, D), BF16)],
    )
    return pl.pallas_call(
        _expert_kernel,
        grid_spec=grid_spec,
        out_shape=jax.ShapeDtypeStruct((nb * bm * SUBLANES, LANES), F32),
        compiler_params=_params(("arbitrary",)),
        name="experts",
    )(blk_e, nused, xb, wg, wu, wd)


def _combine_kernel(n_steps, cur_ref, nxt_ref, y_hbm, x1_ref, route_ref, gt_ref, gf_ref, o_ref, buf, sem):
    i = pl.program_id(0)
    tm, D = x1_ref.shape
    slot = i % 2
    rows = 2 * tm * SUBLANES
    cur0 = slot * rows
    nxt0 = (1 - slot) * rows

    @pl.when(i == 0)
    def _():
        _gather_start(cur_ref, y_hbm, buf, 0, 2 * tm, sem.at[0])

    _gather_wait(y_hbm, buf, cur0, 2 * tm, sem.at[slot])
    route = route_ref[...]
    w1, w2 = route[:, 2:3], route[:, 3:4]
    gate = gt_ref[0]
    per = 2 * tm // SUBLANES
    sumsq = jnp.zeros((tm, 1), F32)
    for s in range(SUBLANES):
        cols = slice(s * LANES, (s + 1) * LANES)
        y1 = buf[pl.ds(cur0 + s, tm, stride=SUBLANES), :]
        y2 = buf[pl.ds(cur0 + tm * SUBLANES + s, tm, stride=SUBLANES), :]
        x = x1_ref[:, cols] + gate[:, cols] * (w1 * y1 + w2 * y2)
        o_ref[:, cols] = x
        sumsq = sumsq + jnp.sum(x * x, axis=1, keepdims=True)
        for u in range(per):
            r = s * per + u
            src = pl.multiple_of(nxt_ref[0, 0, r] * SUBLANES, SUBLANES)
            row = pl.multiple_of(nxt0 + r * SUBLANES, SUBLANES)
            copy = pltpu.make_async_copy(y_hbm.at[pl.ds(src, SUBLANES)], buf.at[pl.ds(row, SUBLANES)],
                                         sem.at[1 - slot])
            copy.start(priority=u % 2)
    o_ref[...] = o_ref[...] * lax.rsqrt(sumsq * (1.0 / D) + EPS) * gf_ref[...]

    @pl.when(i == n_steps - 1)
    def _():
        _gather_wait(y_hbm, buf, nxt0, 2 * tm, sem.at[1 - slot])


def _combine(tile0, n_tok, dest_tiles, yb, x1, route, gt2, g_final, L):
    T, D = x1.shape
    tm = TOKEN_TILE
    per_b = L // tm
    n = n_tok // tm
    last = tile0 + n - 1
    smem = lambda f: pl.BlockSpec((1, 1, 2 * tm), lambda i: (f(i), 0, 0), memory_space=pltpu.SMEM)
    return pl.pallas_call(
        functools.partial(_combine_kernel, n),
        grid=(n,),
        in_specs=[smem(lambda i: tile0 + i), smem(lambda i: jnp.minimum(tile0 + i + 1, last)),
                  pl.BlockSpec(memory_space=pl.ANY),
                  pl.BlockSpec((tm, D), lambda i: (tile0 + i, 0)),
                  pl.BlockSpec((tm, LANES), lambda i: (tile0 + i, 0)),
                  pl.BlockSpec((1, 1, D), lambda i: ((tile0 + i) // per_b, 0, 0)),
                  pl.BlockSpec((1, D), lambda i: (0, 0))],
        out_specs=pl.BlockSpec((tm, D), lambda i: (i, 0)),
        out_shape=jax.ShapeDtypeStruct((n_tok, D), F32),
        scratch_shapes=[pltpu.VMEM((2 * 2 * tm * SUBLANES, LANES), F32), pltpu.SemaphoreType.DMA((2,))],
        compiler_params=_params(("arbitrary",)),
        name="combine",
    )(dest_tiles, dest_tiles, yb, x1, route, gt2, g_final.reshape(1, D))


def _encoder(xp, xs, c, p, g_final):
    n_prompt, L, D = xp.shape
    B = n_prompt + xs.shape[0]
    T = B * L
    xa, xb = xp.reshape(-1, D), xs.reshape(-1, D)
    d_f = N_FOURIER_GROUPS * FOURIER_GROUP_DIM
    d_h = (HYENA_ORDER + 1) * D_HYENA
    d_g = 2 * D

    mod = _ada(c, p["w_ada"], p["b_ada"])
    sh1, sc1, gt1, sh2, sc2, gt2 = [mod[:, k * D:(k + 1) * D].reshape(B, 1, D) for k in range(6)]

    wb = jnp.concatenate([p["w_in"][:, :d_f], jnp.broadcast_to(p["b_in"][None, :d_f], (8, d_f))], axis=0)
    fre, fim = _fold_channel_dft(wb)
    w_all = jnp.concatenate([fre[:D], fim[:D], p["w_in"][:, d_f:]], axis=1).astype(BF16)
    b_all = jnp.concatenate([fre[D], fim[D], p["b_in"][d_f:]])[None, :]
    zr, zi, uh, sg = _inproj(xa, xb, sc1, sh1, w_all, b_all, L, d_f, d_h, d_g)

    f = _fourier(zr.reshape(B, L, d_f), zi.reshape(B, L, d_f))

    ktime, ksum = _filt_time(L, p["filt_w1"], p["filt_b1"], p["filt_w2"], p["filt_b2"],
                             p["filt_w3"], p["filt_b3"], p["filt_freq"], p["filt_wout"])
    kf = _filt_fft(L, ktime, ksum)
    uh3 = uh.reshape(B, L, d_h)
    conv_b = p["conv_b"][None, :]
    ncol = D_HYENA // FFT_COLS
    skip = p["hyena_skip"]
    z = _hyena_order(0, uh3, 0, uh3, ncol, p["conv_w"], conv_b, skip[0:1], kf, L)
    z = _hyena_order(1, z, 0, uh3, 2 * ncol, p["conv_w"], conv_b, skip[1:2], kf, L)

    rw = jnp.zeros((D, LANES), F32).at[:, :N_GROUPS].set(p["router_w1"])
    rw = rw.at[:, N_GROUPS:N_GROUPS + N_EXPERTS].set(p["router_w2"])
    rb = jnp.zeros((1, LANES), F32).at[0, :N_GROUPS].set(p["router_b1"])
    rb = rb.at[0, N_GROUPS:N_GROUPS + N_EXPERTS].set(p["router_b2"])
    x1, h2, route, route_t, counts = _merge(f.reshape(T, d_f), z.reshape(T, D_HYENA), sg, xa, xb,
                                            gt1, sc2, sh2, p["w_four"].astype(BF16), p["w_hyena"].astype(BF16),
                                            p["w_out"].astype(BF16), rw, rb, L)

    bm = MOE_BLOCK
    tm = TOKEN_TILE
    nb = (T * TOP_K) // bm + N_EXPERTS
    e = route_t[0:2].astype(jnp.int32)
    rank = route_t[4:6].astype(jnp.int32)
    cnt = counts[0, :N_EXPERTS].astype(jnp.int32)
    pcnt = (cnt + bm - 1) // bm * bm
    pend = jnp.cumsum(pcnt)
    experts = jnp.arange(N_EXPERTS, dtype=jnp.int32)
    dest = rank + jnp.sum(jnp.where(e[..., None] == experts, pend - pcnt, 0), axis=-1)
    blk_row0 = jnp.arange(nb, dtype=jnp.int32)[:, None] * bm
    blk_e = jnp.minimum(jnp.sum((pend[None, :] <= blk_row0).astype(jnp.int32), axis=1), N_EXPERTS - 1)
    nused = (pend[-1] // bm).astype(jnp.int32).reshape(1)
    dest_tiles = dest.reshape(TOP_K, T // tm, tm).transpose(1, 0, 2).reshape(T // tm, 1, TOP_K * tm)
    xb = _dispatch(pend.astype(jnp.int32), dest_tiles, h2, nb * bm)
    yb = _experts(blk_e, nused, xb, p["exp_w_gate"], p["exp_w_up"], p["exp_w_down"])

    t_prompt = n_prompt * L
    outs = []
    for tile0, n_tok in ((0, t_prompt), (t_prompt // tm, T - t_prompt)):
        outs.append(_combine(tile0, n_tok, dest_tiles, yb, x1, route, gt2, g_final, L))
    return outs[0].reshape(xp.shape), outs[1].reshape(xs.shape)


def kernel(x_prompt, x_sample, c_prompt, c_sample, w_ada, b_ada, w_in, b_in, conv_w, conv_b, filt_w1, filt_b1, filt_w2, filt_b2, filt_w3, filt_b3, filt_freq, filt_wout, hyena_skip, w_four, w_hyena, w_out, router_w1, router_b1, router_w2, router_b2, exp_w_gate, exp_w_up, exp_w_down, g_final):
    assert w_ada.shape[0] == 1, "single-layer block"
    assert x_prompt.shape[1:] == x_sample.shape[1:], "both request groups share sequence length and width"
    p = dict(w_ada=w_ada[0], b_ada=b_ada[0], w_in=w_in[0], b_in=b_in[0], conv_w=conv_w[0], conv_b=conv_b[0],
             filt_w1=filt_w1[0], filt_b1=filt_b1[0], filt_w2=filt_w2[0], filt_b2=filt_b2[0],
             filt_w3=filt_w3[0], filt_b3=filt_b3[0], filt_freq=filt_freq[0], filt_wout=filt_wout[0],
             hyena_skip=hyena_skip[0], w_four=w_four[0], w_hyena=w_hyena[0], w_out=w_out[0],
             router_w1=router_w1[0], router_b1=router_b1[0], router_w2=router_w2[0], router_b2=router_b2[0],
             exp_w_gate=exp_w_gate[0], exp_w_up=exp_w_up[0], exp_w_down=exp_w_down[0])
    c = jnp.concatenate([c_prompt, c_sample], axis=0)
    return _encoder(x_prompt, x_sample, c, p, g_final)
```

```python
import functools
import math

import numpy as np
import jax
import jax.numpy as jnp
from jax import lax
from jax.experimental import pallas as pl
from jax.experimental.pallas import tpu as pltpu

F32 = jnp.float32
BF16 = jnp.bfloat16
HI = lax.Precision.HIGHEST

EPS = 1e-6
FFT_MINOR = 64
N_FOURIER_GROUPS = 4
FOURIER_GROUP_DIM = 128
D_HYENA = 512
HYENA_ORDER = 2
POS_BANDS = 16
N_GROUPS = 4
EXPERTS_PER_GROUP = 8
N_EXPERTS = 32
TOP_K = 2
SHORT_DECAY_PCT = 0.3
LONG_DECAY_PCT = 1.5
DECAY_TARGET = 1e-2
LANES = 128
SUBLANES = 8
VMEM_LIMIT = 56 * 1024 * 1024

TOKEN_TILE = 512
MERGE_SUBTILE = 128
MOE_BLOCK = 512
FFT_COLS = LANES
ROW_CHUNK = 256
FFT_UNROLL = 64
DMA_GROUP = 8


def _dot(a, b):
    return jnp.dot(a, b, preferred_element_type=F32)


def _dot_hi(a, b):
    return jnp.dot(a, b, preferred_element_type=F32, precision=HI)


def _split_bf16(x):
    hi = x.astype(BF16)
    return hi, (x - hi.astype(F32)).astype(BF16)


def _dot3(a, b):
    a_hi, a_lo = _split_bf16(a)
    b_hi, b_lo = _split_bf16(b)
    return _dot(a_hi, b_hi) + (_dot(a_lo, b_hi) + _dot(a_hi, b_lo))


def _params(sem=None):
    return pltpu.CompilerParams(dimension_semantics=sem, vmem_limit_bytes=VMEM_LIMIT)


def _const_spec(shape):
    nd = len(shape)
    return pl.BlockSpec(shape, lambda *_: (0,) * nd, pipeline_mode=pl.Buffered(1))


def _pitch(rows):
    p = -(-rows // SUBLANES)
    return SUBLANES * (p if p % 2 else p + 1)


@functools.lru_cache(maxsize=None)
def _stage2_tables():
    i = np.arange(FFT_MINOR)
    ph = 2.0 * np.pi * np.outer(i, i) / FFT_MINOR
    c, s = np.cos(ph), np.sin(ph)
    fwd = np.block([[c, s], [-s, c]])
    inv = np.block([[c, -s], [s, c]])
    return fwd, inv, c, s


@functools.lru_cache(maxsize=None)
def _hyena_tables(L):
    n_fft = 2 * L
    m = FFT_MINOR
    na = n_fft // m
    nh = na // 2
    b = np.arange(m)[:, None, None]
    ka = np.arange(nh + 1)[None, :, None]

    def forward(a_count):
        a = np.arange(a_count)[None, None, :]
        th = 2.0 * np.pi * ((ka * (m * a + b)) % n_fft) / n_fft
        c, s = np.cos(th), -np.sin(th)
        pad = np.zeros((m, SUBLANES - 2, a_count))
        return np.concatenate([c[:, :nh], s[:, :nh], c[:, nh:], s[:, nh:], pad], axis=1)

    fwd_half = forward(nh)
    fwd_full = forward(na)
    weight = np.where(np.arange(nh) == 0, 1.0, 2.0)[None, :, None]
    inv_half = np.transpose(fwd_half[:, :2 * nh] * np.concatenate([weight, weight], axis=1), (0, 2, 1)) / n_fft
    sign = np.broadcast_to(((-1.0) ** np.arange(nh))[:, None] / n_fft, (nh, LANES))
    ph = np.pi * np.arange(m) / m
    cos_b = np.broadcast_to(np.cos(ph)[:, None], (m, LANES))
    sin_b = np.broadcast_to(np.sin(ph)[:, None], (m, LANES))
    return fwd_half, inv_half, fwd_full, sign, cos_b, sin_b


@functools.lru_cache(maxsize=None)
def _fourier_tables(L):
    na = L // FFT_MINOR
    b = np.arange(FFT_MINOR)[:, None, None]
    ka = np.arange(na)[None, :, None]
    a = np.arange(na)[None, None, :]
    th = 2.0 * np.pi * ((ka * (FFT_MINOR * a + b)) % L) / L
    c, s = np.cos(th), np.sin(th)
    st1 = np.concatenate([np.concatenate([c, s], axis=2),
                          np.concatenate([-s, c], axis=2)], axis=1)
    _, _, c2, s2 = _stage2_tables()
    st2 = np.concatenate([c2, s2], axis=1) / math.sqrt(L)
    return st1, st2


def _mxu_table(table):
    return jnp.asarray(table, F32).astype(BF16)


@functools.lru_cache(maxsize=None)
def _channel_dft():
    i = np.arange(FOURIER_GROUP_DIM)
    ph = 2.0 * np.pi * np.outer(i, i) / FOURIER_GROUP_DIM
    return np.concatenate([np.cos(ph), -np.sin(ph)], axis=1) / math.sqrt(FOURIER_GROUP_DIM)


def _ada_kernel(c_ref, w_ref, b_ref, o_ref):
    c = c_ref[...]
    o_ref[...] = _dot_hi(c * jax.nn.sigmoid(c), w_ref[...]) + b_ref[...]


def _ada(c, w_ada, b_ada):
    nb, d = c.shape
    n = w_ada.shape[1]
    tn = 1536
    return pl.pallas_call(
        _ada_kernel,
        grid=(n // tn,),
        in_specs=[pl.BlockSpec((nb, d), lambda j: (0, 0)),
                  pl.BlockSpec((d, tn), lambda j: (0, j)),
                  pl.BlockSpec((1, tn), lambda j: (0, j))],
        out_specs=pl.BlockSpec((nb, tn), lambda j: (0, j)),
        out_shape=jax.ShapeDtypeStruct((nb, n), F32),
        compiler_params=_params(("arbitrary",)),
        name="ada",
    )(c, w_ada, b_ada.reshape(1, n))


def _for_row_chunks(n_rows, body):
    def step(j, carry):
        body(j, pl.multiple_of(j * ROW_CHUNK, ROW_CHUNK))
        return carry

    lax.fori_loop(0, n_rows // ROW_CHUNK, step, 0)


def _store_padded(dst, j, val, run=FFT_MINOR):
    pz = _pitch(run)
    per = ROW_CHUNK // run
    for i in range(per):
        dst[pl.ds(pl.multiple_of((j * per + i) * pz, SUBLANES), run), :] = val[i * run:(i + 1) * run]


def _load_padded(src, j, run=FFT_MINOR):
    pz = _pitch(run)
    per = ROW_CHUNK // run
    return jnp.concatenate([src[pl.ds(pl.multiple_of((j * per + i) * pz, SUBLANES), run), :] for i in range(per)],
                           axis=0)


def _short_conv_chunk(src_ref, w_ref, b_ref, L, j, r0):
    R = ROW_CHUNK
    ct = src_ref.shape[-1]
    row = lax.broadcasted_iota(jnp.int32, (R, ct), 0)
    cur = src_ref[0, pl.ds(r0, R), :]
    before = src_ref[0, pl.ds(jnp.maximum(r0 - 1, 0), 1), :] * jnp.where(j > 0, 1.0, 0.0)
    after = src_ref[0, pl.ds(jnp.minimum(r0 + R, L - 1), 1), :] * jnp.where(j < L // R - 1, 1.0, 0.0)
    up = jnp.where(row == 0, before, pltpu.roll(cur, 1, 0))
    dn = jnp.where(row == R - 1, after, pltpu.roll(cur, R - 1, 0))
    return w_ref[0:1, :] * up + w_ref[1:2, :] * cur + w_ref[2:3, :] * dn + b_ref[...]


def _filt_time_kernel(L, rows, bands_ref, delt_ref, w1_ref, b1_ref, w2_ref, b2_ref, w3_ref, b3_ref,
                      fr_ref, wo_ref, k_ref, sum_ref):
    i = pl.program_id(0)
    n = i * rows + lax.broadcasted_iota(jnp.int32, (rows, 1), 0)
    fwd = n < L
    pos = jnp.where(fwd, n, 2 * L - n).astype(F32)
    t = pos * (1.0 / (L - 1))
    ang = (2.0 * math.pi / L) * pos * bands_ref[...]
    w1 = w1_ref[...]
    pre = (t * w1[0:1, :] + _dot3(jnp.cos(ang), w1[1:1 + POS_BANDS, :])
           + _dot3(-jnp.sin(ang), w1[1 + POS_BANDS:, :]) + b1_ref[...])
    fr = fr_ref[...]
    h = jnp.sin(fr[0:1, :] * pre)
    h = jnp.sin(fr[1:2, :] * (_dot3(h, w2_ref[...]) + b2_ref[...]))
    h = jnp.sin(fr[2:3, :] * (_dot3(h, w3_ref[...]) + b3_ref[...]))
    window = jnp.exp(-t * delt_ref[...])
    live = n != L
    tile_fwd = i < (L // rows)
    parts = []
    for o in range(HYENA_ORDER):
        base = o * 2 * D_HYENA
        w = jnp.where(tile_fwd, wo_ref[:, base:base + D_HYENA], wo_ref[:, base + D_HYENA:base + 2 * D_HYENA])
        parts.append(jnp.where(live, _dot3(h, w) * window, 0.0))
    k = jnp.concatenate(parts, axis=1)
    k_ref[...] = k

    @pl.when(i == 0)
    def _():
        sum_ref[...] = jnp.zeros_like(sum_ref)

    sum_ref[...] += jnp.sum(jnp.abs(k), axis=0, keepdims=True)


def _filt_time(L, w1, b1, w2, b2, w3, b3, freq, wout):
    rows = 512
    assert L % rows == 0, "each row tile must hold one filter direction only"
    n_fft = 2 * L
    cols = HYENA_ORDER * D_HYENA
    bands = jnp.linspace(1e-4, POS_BANDS - 1, POS_BANDS, dtype=F32).reshape(1, POS_BANDS)
    max_decay = math.log(DECAY_TARGET) / SHORT_DECAY_PCT
    min_decay = math.log(DECAY_TARGET) / LONG_DECAY_PCT
    deltas = jnp.abs(jnp.linspace(min_decay, max_decay, D_HYENA, dtype=F32)).reshape(1, D_HYENA)
    args = (bands, deltas, w1, b1.reshape(1, -1), w2, b2.reshape(1, -1), w3, b3.reshape(1, -1), freq, wout)
    return pl.pallas_call(
        functools.partial(_filt_time_kernel, L, rows),
        grid=(n_fft // rows,),
        in_specs=[pl.BlockSpec(a.shape, lambda i: (0, 0)) for a in args],
        out_specs=[pl.BlockSpec((rows, cols), lambda i: (i, 0)),
                   pl.BlockSpec((1, cols), lambda i: (0, 0))],
        out_shape=[jax.ShapeDtypeStruct((n_fft, cols), F32),
                   jax.ShapeDtypeStruct((1, cols), F32)],
        compiler_params=_params(("arbitrary",)),
        name="filt_time",
    )(*args)


def _filt_fft_kernel(na, k_ref, sum_ref, st1_ref, st2_ref, o_ref, kp, s1):
    m = FFT_MINOR
    nh = na // 2
    rows1 = 2 * nh + SUBLANES
    pz, p1 = _pitch(m), _pitch(rows1)
    _for_row_chunks(na * m, lambda j, r0: _store_padded(kp, j, k_ref[pl.ds(r0, ROW_CHUNK), :]))

    def stage1(b, carry):
        slab = kp[pl.ds(b, na, stride=pz), :]
        s1[pl.ds(pl.multiple_of(b * p1, SUBLANES), rows1), :] = _dot3(st1_ref[b], slab)
        return carry

    lax.fori_loop(0, m, stage1, 0, unroll=4)
    inv_norm = 1.0 / sum_ref[...]

    def spectrum_row(re_row, im_row):
        a = jnp.concatenate([s1[pl.ds(re_row, m, stride=p1), :], s1[pl.ds(im_row, m, stride=p1), :]], axis=0)
        return _dot3(st2_ref[...], a) * inv_norm

    def stage2(ka, carry):
        o_ref[ka] = spectrum_row(ka, nh + ka)
        return carry

    lax.fori_loop(0, nh, stage2, 0, unroll=4)
    o_ref[nh] = spectrum_row(2 * nh, 2 * nh + 1)


def _filt_fft(L, ktime, ksum):
    n_fft, cols = ktime.shape
    m = FFT_MINOR
    na = n_fft // m
    nh = na // 2
    rows1 = 2 * nh + SUBLANES
    ct = FFT_COLS
    fwd_full = _hyena_tables(L)[2]
    st2f = _stage2_tables()[0]
    return pl.pallas_call(
        functools.partial(_filt_fft_kernel, na),
        grid=(cols // ct,),
        in_specs=[pl.BlockSpec((n_fft, ct), lambda j: (0, j)),
                  pl.BlockSpec((1, ct), lambda j: (0, j)),
                  _const_spec((m, rows1, na)),
                  _const_spec((2 * m, 2 * m))],
        out_specs=pl.BlockSpec((nh + 1, 2 * m, ct), lambda j: (0, 0, j)),
        out_shape=jax.ShapeDtypeStruct((nh + 1, 2 * m, cols), F32),
        scratch_shapes=[pltpu.VMEM((na * _pitch(m), ct), F32),
                        pltpu.VMEM((m * _pitch(rows1), ct), F32)],
        compiler_params=_params(("arbitrary",)),
        name="filt_fft",
    )(ktime, ksum, jnp.asarray(fwd_full, F32), jnp.asarray(st2f, F32))


def _rms_mod(x, scale, shift):
    y = x * lax.rsqrt(jnp.mean(x * x, axis=-1, keepdims=True) + EPS)
    return y * (1.0 + scale) + shift


def _group_tile(n_first, a_ref, b_ref):
    return jnp.where(pl.program_id(0) < n_first, a_ref[...], b_ref[...])


def _group_specs(n_first, tm, D):
    return [pl.BlockSpec((tm, D), lambda i: (jnp.minimum(i, n_first - 1), 0)),
            pl.BlockSpec((tm, D), lambda i: (jnp.maximum(i - n_first, 0), 0))]


def _inproj_kernel(d_f, d_h, n_first, xa_ref, xb_ref, sc_ref, sh_ref, w_ref, b_ref, uf_ref, uh_ref, sg_ref):
    h = _rms_mod(_group_tile(n_first, xa_ref, xb_ref), sc_ref[0], sh_ref[0]).astype(BF16)

    def proj(c0, width):
        return _dot(h, w_ref[:, c0:c0 + width]) + b_ref[:, c0:c0 + width]

    uf_ref[...] = proj(0, d_f)
    chunk = 512
    for c in range(0, d_h, chunk):
        uh_ref[:, c:c + chunk] = proj(d_f + c, chunk)
    d_g = sg_ref.shape[1]
    for c in range(0, d_g, chunk):
        sg_ref[:, c:c + chunk] = jax.nn.sigmoid(proj(d_f + d_h + c, chunk)).astype(BF16)


def _inproj(xa, xb, sc, sh, w, b, L, d_f, d_h, d_g):
    D = xa.shape[1]
    T = xa.shape[0] + xb.shape[0]
    tm = TOKEN_TILE
    per_b = L // tm
    n_first = xa.shape[0] // tm
    n = w.shape[1]
    mod_spec = pl.BlockSpec((1, 1, D), lambda i: (i // per_b, 0, 0))
    return pl.pallas_call(
        functools.partial(_inproj_kernel, d_f, d_h, n_first),
        grid=(T // tm,),
        in_specs=_group_specs(n_first, tm, D) + [mod_spec, mod_spec, _const_spec((D, n)), _const_spec((1, n))],
        out_specs=[pl.BlockSpec((tm, d_f), lambda i: (i, 0)),
                   pl.BlockSpec((tm, d_h), lambda i: (i, 0)),
                   pl.BlockSpec((tm, d_g), lambda i: (i, 0))],
        out_shape=[jax.ShapeDtypeStruct((T, d_f), F32),
                   jax.ShapeDtypeStruct((T, d_h), F32), jax.ShapeDtypeStruct((T, d_g), BF16)],
        compiler_params=_params(("parallel",)),
        name="inproj",
    )(xa, xb, sc, sh, w, b)


def _fourier_kernel(na, u_ref, cdft_ref, st1_ref, st2_ref, o_ref, zpr, zpi, s1, op):
    m = FFT_MINOR
    pz, p1 = _pitch(m), _pitch(2 * na)

    def fill(j, r0):
        z = _dot(u_ref[0, pl.ds(r0, ROW_CHUNK), :].astype(BF16), cdft_ref[...])
        _store_padded(zpr, j, z[:, :FOURIER_GROUP_DIM])
        _store_padded(zpi, j, z[:, FOURIER_GROUP_DIM:])

    _for_row_chunks(na * m, fill)

    def stage1(b, carry):
        z = jnp.concatenate([zpr[pl.ds(b, na, stride=pz), :], zpi[pl.ds(b, na, stride=pz), :]], axis=0)
        s1[pl.ds(pl.multiple_of(b * p1, SUBLANES), 2 * na), :] = _dot(st1_ref[b], z.astype(BF16))
        return carry

    lax.fori_loop(0, m, stage1, 0, unroll=FFT_UNROLL)

    po = _pitch(na)

    def stage2(ka, carry):
        a = jnp.concatenate([s1[pl.ds(ka, m, stride=p1), :], s1[pl.ds(na + ka, m, stride=p1), :]], axis=0)
        op[pl.ds(ka, m, stride=po), :] = _dot(st2_ref[...], a.astype(BF16))
        return carry

    lax.fori_loop(0, na, stage2, 0, unroll=FFT_UNROLL)

    def unpad(j, r0):
        o_ref[0, pl.ds(r0, ROW_CHUNK), :] = _load_padded(op, j, na)

    _for_row_chunks(na * m, unpad)


def _fourier(u):
    B, L, C = u.shape
    m = FFT_MINOR
    na = L // m
    ct = FOURIER_GROUP_DIM
    assert ct == FFT_COLS
    st1, st2 = _fourier_tables(L)
    spec = pl.BlockSpec((1, L, ct), lambda j, b: (b, 0, j))
    pad = pltpu.VMEM((na * _pitch(m), ct), F32)
    return pl.pallas_call(
        functools.partial(_fourier_kernel, na),
        grid=(C // ct, B),
        in_specs=[spec, _const_spec((ct, 2 * ct)), _const_spec((m, 2 * na, 2 * na)), _const_spec((m, 2 * m))],
        out_specs=spec,
        out_shape=jax.ShapeDtypeStruct((B, L, C), F32),
        scratch_shapes=[pad, pad, pltpu.VMEM((m * _pitch(2 * na), ct), F32),
                        pltpu.VMEM((m * _pitch(na), ct), F32)],
        compiler_params=_params(("parallel", "parallel")),
        name="fourier",
    )(u, _mxu_table(_channel_dft()), _mxu_table(st1), _mxu_table(st2))


def _hyena_kernel(conv_z, L, zin_ref, gin_ref, cwz_ref, cbz_ref, cwg_ref, cbg_ref, skip_ref, kf_ref,
                  fwd1_ref, inv1_ref, st2f_ref, st2i_ref, sign_ref, cosb_ref, sinb_ref, out_ref, zp, yp, s1, g2, hh):
    m = FFT_MINOR
    nh = L // m
    rows1 = 2 * nh + SUBLANES
    pz, p1, p2 = _pitch(m), _pitch(rows1), _pitch(2 * m)

    def fill(j, r0):
        if conv_z:
            _store_padded(zp, j, _short_conv_chunk(zin_ref, cwz_ref, cbz_ref, L, j, r0))
        else:
            _store_padded(zp, j, zin_ref[0, pl.ds(r0, ROW_CHUNK), :])

    _for_row_chunks(L, fill)

    def stage1(b, carry):
        slab = zp[pl.ds(b, nh, stride=pz), :].astype(BF16)
        s1[pl.ds(pl.multiple_of(b * p1, SUBLANES), rows1), :] = _dot(fwd1_ref[b], slab)
        return carry

    lax.fori_loop(0, m, stage1, 0, unroll=FFT_UNROLL)

    def spectrum_row(ka, re_row, im_row):
        a = jnp.concatenate([s1[pl.ds(re_row, m, stride=p1), :], s1[pl.ds(im_row, m, stride=p1), :]], axis=0)
        x = _dot(st2f_ref[...], a.astype(BF16))
        xr, xi = x[:m], x[m:]
        k = kf_ref[ka]
        kr, ki = k[:m], k[m:]
        y = jnp.concatenate([xr * kr - xi * ki, xr * ki + xi * kr], axis=0).astype(BF16)
        return _dot(st2i_ref[...], y)

    def stage2(ka, carry):
        g2[pl.ds(pl.multiple_of(ka * p2, SUBLANES), 2 * m), :] = spectrum_row(ka, ka, nh + ka)
        return carry

    lax.fori_loop(0, nh, stage2, 0, unroll=FFT_UNROLL)
    g_mid = spectrum_row(nh, 2 * nh, 2 * nh + 1)
    hh[...] = cosb_ref[...] * g_mid[:m] - sinb_ref[...] * g_mid[m:]
    sign = sign_ref[...]

    def stage3(b, carry):
        g = jnp.concatenate([g2[pl.ds(b, nh, stride=p2), :], g2[pl.ds(m + b, nh, stride=p2), :]], axis=0)
        yp[pl.ds(b, nh, stride=pz), :] = _dot(inv1_ref[b], g.astype(BF16)) + sign * hh[pl.ds(b, 1), :]
        return carry

    lax.fori_loop(0, m, stage3, 0, unroll=FFT_UNROLL)
    skip = skip_ref[...]

    def finish(j, r0):
        gate = _short_conv_chunk(gin_ref, cwg_ref, cbg_ref, L, j, r0)
        out_ref[0, pl.ds(r0, ROW_CHUNK), :] = gate * (_load_padded(yp, j) + skip * _load_padded(zp, j))

    _for_row_chunks(L, finish)


def _hyena_order(order, zin, z_col0, uh, g_col0, conv_w, conv_b, skip, kf, L):
    B = uh.shape[0]
    m = FFT_MINOR
    ct = FFT_COLS
    ncol = D_HYENA // ct
    nh = L // m
    rows1 = 2 * nh + SUBLANES
    conv_z = order == 0
    fwd_half, inv_half, _, sign, cos_b, sin_b = _hyena_tables(L)
    st2f, st2i, _, _ = _stage2_tables()
    z_cols = (lambda j, b: (b, 0, z_col0 + j))
    g_cols = (lambda j, b: (b, 0, g_col0 + j))
    zw_col0 = z_col0 if conv_z else 0
    pad = pltpu.VMEM((nh * _pitch(m), ct), F32)
    return pl.pallas_call(
        functools.partial(_hyena_kernel, conv_z, L),
        grid=(ncol, B),
        in_specs=[pl.BlockSpec((1, L, ct), z_cols),
                  pl.BlockSpec((1, L, ct), g_cols),
                  pl.BlockSpec((3, ct), lambda j, b: (0, zw_col0 + j)),
                  pl.BlockSpec((1, ct), lambda j, b: (0, zw_col0 + j)),
                  pl.BlockSpec((3, ct), lambda j, b: (0, g_col0 + j)),
                  pl.BlockSpec((1, ct), lambda j, b: (0, g_col0 + j)),
                  pl.BlockSpec((1, ct), lambda j, b: (0, j)),
                  pl.BlockSpec((nh + 1, 2 * m, ct), lambda j, b: (0, 0, order * ncol + j),
                               pipeline_mode=pl.Buffered(1)),
                  _const_spec((m, rows1, nh)),
                  _const_spec((m, nh, 2 * nh)),
                  _const_spec((2 * m, 2 * m)),
                  _const_spec((2 * m, 2 * m)),
                  _const_spec((nh, LANES)), _const_spec((m, LANES)), _const_spec((m, LANES))],
        out_specs=pl.BlockSpec((1, L, ct), lambda j, b: (b, 0, j)),
        out_shape=jax.ShapeDtypeStruct((B, L, D_HYENA), F32),
        scratch_shapes=[pad, pad,
                        pltpu.VMEM((m * _pitch(rows1), ct), F32),
                        pltpu.VMEM((nh * _pitch(2 * m), ct), F32),
                        pltpu.VMEM((m, ct), F32)],
        compiler_params=_params(("parallel", "arbitrary")),
        name=f"hyena{order}",
    )(zin, uh, conv_w, conv_b, conv_w, conv_b, skip, kf,
      _mxu_table(fwd_half), _mxu_table(inv_half), _mxu_table(st2f), _mxu_table(st2i),
      jnp.asarray(sign, F32), jnp.asarray(cos_b, F32), jnp.asarray(sin_b, F32))


def _merge_kernel(n_first, f_ref, z_ref, sg_ref, xa_ref, xb_ref, gt_ref, sc_ref, sh_ref, wf_ref, wh_ref, wo_ref,
                  rw_ref, rb_ref, tri_ref, x1_ref, h2_ref, route_ref, rt_ref, cnt_ref):
    i = pl.program_id(0)
    tm, D = xa_ref.shape
    sub = MERGE_SUBTILE
    lane = lax.broadcasted_iota(jnp.int32, (sub, LANES), 1).astype(F32)
    neg = -1e30

    def first_max(v):
        mx = jnp.max(v, axis=1, keepdims=True)
        return mx, jnp.min(jnp.where(v == mx, lane, float(LANES)), axis=1, keepdims=True)

    @pl.when(i == 0)
    def _():
        cnt_ref[...] = jnp.zeros_like(cnt_ref)

    def sub_tile(r0, count):
        rows = pl.ds(r0, sub)
        ya = _dot(f_ref[rows, :].astype(BF16), wf_ref[...])
        yb = _dot(z_ref[rows, :].astype(BF16), wh_ref[...])
        merged = sg_ref[rows, :D].astype(F32) * ya + sg_ref[rows, D:].astype(F32) * yb
        x = jnp.where(i < n_first, xa_ref[rows, :], xb_ref[rows, :])
        x1 = x + gt_ref[0] * _dot(merged.astype(BF16), wo_ref[...])
        x1_ref[rows, :] = x1
        h2 = _rms_mod(x1, sc_ref[0], sh_ref[0])
        _to_token_tiles(h2_ref, h2, r0 * SUBLANES)

        h_hi, h_lo = _split_bf16(h2)
        logits = (_dot(h_hi, rw_hi) + (_dot(h_lo, rw_hi) + _dot(h_hi, rw_lo))) + rb_ref[...]
        gl = jnp.where(lane < N_GROUPS, logits, neg)
        gmax, g = first_max(gl)
        p_g = 1.0 / jnp.sum(jnp.exp(gl - gmax), axis=1, keepdims=True)
        lo = N_GROUPS + EXPERTS_PER_GROUP * g
        el = jnp.where((lane >= lo) & (lane < lo + EXPERTS_PER_GROUP), logits, neg)
        m1, i1 = first_max(el)
        m2, i2 = first_max(jnp.where(lane == i1, neg, el))
        r = jnp.exp(m2 - m1)
        wt1 = p_g / (1.0 + r)
        wt2 = p_g * r / (1.0 + r)
        e1 = i1 - N_GROUPS
        e2 = i2 - N_GROUPS

        onehot = ((lane == e1) | (lane == e2)).astype(BF16)
        before = _dot(tri_ref[...], onehot) + count
        r1 = jnp.sum(jnp.where(lane == e1, before, 0.0), axis=1, keepdims=True)
        r2 = jnp.sum(jnp.where(lane == e2, before, 0.0), axis=1, keepdims=True)

        packed = jnp.zeros((sub, LANES), F32)
        for slot, v in enumerate((e1, e2, wt1, wt2, r1, r2)):
            packed = jnp.where(lane == slot, v, packed)
        route_ref[rows, :] = packed
        rt_ref[:, rows] = packed.T[0:SUBLANES, :]
        return count + jnp.sum(onehot.astype(F32), axis=0, keepdims=True)

    rw_hi, rw_lo = _split_bf16(rw_ref[...])
    count = cnt_ref[...]
    for h in range(tm // sub):
        count = sub_tile(h * sub, count)
    cnt_ref[...] = count


def _merge(f2d, z2d, sg, xa, xb, gt1, sc2, sh2, w_four, w_hyena, w_out, rw, rb, L):
    D = xa.shape[1]
    T = xa.shape[0] + xb.shape[0]
    assert D == SUBLANES * LANES
    tm = TOKEN_TILE
    per_b = L // tm
    n_first = xa.shape[0] // tm
    d_f = f2d.shape[1]
    d_h = z2d.shape[1]
    sub = MERGE_SUBTILE
    tri = jnp.asarray(np.tril(np.ones((sub, sub)), -1), BF16)
    mod_spec = pl.BlockSpec((1, 1, D), lambda i: (i // per_b, 0, 0))
    row = lambda w: pl.BlockSpec((tm, w), lambda i: (i, 0))
    return pl.pallas_call(
        functools.partial(_merge_kernel, n_first),
        grid=(T // tm,),
        in_specs=[row(d_f), row(d_h), row(2 * D)] + _group_specs(n_first, tm, D) + [
            mod_spec, mod_spec, mod_spec,
            _const_spec((d_f, D)), _const_spec((d_h, D)), _const_spec((D, D)),
            _const_spec((D, LANES)), _const_spec((1, LANES)), _const_spec((sub, sub))],
        out_specs=[row(D), pl.BlockSpec((tm * SUBLANES, LANES), lambda i: (i, 0)), row(LANES),
                   pl.BlockSpec((SUBLANES, tm), lambda i: (0, i)), pl.BlockSpec((1, LANES), lambda i: (0, 0))],
        out_shape=[jax.ShapeDtypeStruct((T, D), F32), jax.ShapeDtypeStruct((T * SUBLANES, LANES), F32),
                   jax.ShapeDtypeStruct((T, LANES), F32), jax.ShapeDtypeStruct((SUBLANES, T), F32),
                   jax.ShapeDtypeStruct((1, LANES), F32)],
        compiler_params=_params(("arbitrary",)),
        name="merge",
    )(f2d, z2d, sg, xa, xb, gt1, sc2, sh2, w_four, w_hyena, w_out, rw, rb, tri)


def _to_token_tiles(ref, val, row0=0):
    n = val.shape[0]
    for s in range(SUBLANES):
        ref[pl.ds(row0 + s, n, stride=SUBLANES), :] = val[:, s * LANES:(s + 1) * LANES]


def _from_token_tiles(ref, row0, n):
    return jnp.concatenate([ref[pl.ds(row0 + s, n, stride=SUBLANES), :] for s in range(SUBLANES)], axis=1)


def _gather_start(idx_ref, src_hbm, dst, row0, n_tok, sem):
    def issue(g, carry):
        for u in range(DMA_GROUP):
            r = g * DMA_GROUP + u
            src = pl.multiple_of(idx_ref[0, 0, r] * SUBLANES, SUBLANES)
            row = pl.multiple_of(row0 + r * SUBLANES, SUBLANES)
            copy = pltpu.make_async_copy(src_hbm.at[pl.ds(src, SUBLANES)], dst.at[pl.ds(row, SUBLANES)], sem)
            copy.start(priority=u % 2)
        return carry

    lax.fori_loop(0, n_tok // DMA_GROUP, issue, 0)


def _gather_wait(src_hbm, dst, row0, n_tok, sem):
    rows = n_tok * SUBLANES
    pltpu.make_async_copy(src_hbm.at[pl.ds(0, rows)], dst.at[pl.ds(row0, rows)], sem).wait()


def _dispatch_kernel(pend_ref, dest_ref, h_ref, x_hbm, zero, sem, zsem):
    i = pl.program_id(0)
    tm = dest_ref.shape[2] // TOP_K
    bm = MOE_BLOCK

    def token_rows(ref, tok):
        return ref.at[pl.ds(pl.multiple_of(tok * SUBLANES, SUBLANES), SUBLANES)]

    def drain(n_tok, s):
        rows = n_tok * SUBLANES
        pltpu.make_async_copy(zero.at[pl.ds(0, rows)], x_hbm.at[pl.ds(0, rows)], s).wait()

    @pl.when(i == 0)
    def _():
        zero[...] = jnp.zeros_like(zero)
        n_blocks = x_hbm.shape[0] // (bm * SUBLANES)
        used = pend_ref[N_EXPERTS - 1] // bm

        def zero_block(first_tok):
            row = pl.multiple_of(first_tok * SUBLANES, bm * SUBLANES)
            pltpu.make_async_copy(zero, x_hbm.at[pl.ds(row, bm * SUBLANES)], zsem).start()

        def per_expert(e, carry):
            zero_block(jnp.maximum(pend_ref[e] - bm, 0))
            return carry

        def per_unused(blk, carry):
            zero_block(blk * bm)
            return carry

        def drain_block(blk, carry):
            drain(bm, zsem)
            return carry

        lax.fori_loop(0, N_EXPERTS, per_expert, 0)
        lax.fori_loop(used, n_blocks, per_unused, 0)
        lax.fori_loop(0, N_EXPERTS + n_blocks - used, drain_block, 0)

    for k in range(TOP_K):
        def issue(g, carry, k=k):
            for u in range(DMA_GROUP):
                r = g * DMA_GROUP + u
                dst = token_rows(x_hbm, dest_ref[0, 0, k * tm + r])
                pltpu.make_async_copy(token_rows(h_ref, r), dst, sem).start(priority=u % 2)
            return carry

        lax.fori_loop(0, tm // DMA_GROUP, issue, 0)

    for k in range(TOP_K):
        pltpu.make_async_copy(h_ref, x_hbm.at[pl.ds(0, tm * SUBLANES)], sem).wait()


def _dispatch(pend, dest_tiles, h2, n_rows):
    n, _, tm2 = dest_tiles.shape
    tm = tm2 // TOP_K
    grid_spec = pltpu.PrefetchScalarGridSpec(
        num_scalar_prefetch=1,
        grid=(n,),
        in_specs=[pl.BlockSpec((1, 1, tm2), lambda i, pe: (i, 0, 0), memory_space=pltpu.SMEM),
                  pl.BlockSpec((tm * SUBLANES, LANES), lambda i, pe: (i, 0))],
        out_specs=pl.BlockSpec(memory_space=pl.ANY),
        scratch_shapes=[pltpu.VMEM((MOE_BLOCK * SUBLANES, LANES), F32), pltpu.SemaphoreType.DMA(()),
                        pltpu.SemaphoreType.DMA(())],
    )
    return pl.pallas_call(
        _dispatch_kernel,
        grid_spec=grid_spec,
        out_shape=jax.ShapeDtypeStruct((n_rows * SUBLANES, LANES), F32),
        compiler_params=_params(("arbitrary",)),
        name="dispatch",
    )(pend, dest_tiles, h2)


def _expert_kernel(blk_e_ref, nused_ref, x_ref, wg_ref, wu_ref, wd_ref, y_ref, wg16, wu16, wd16):
    i = pl.program_id(0)
    nused = nused_ref[0]
    bm = MOE_BLOCK

    @pl.when((i == 0) | (blk_e_ref[i] != blk_e_ref[jnp.maximum(i - 1, 0)]))
    def _():
        wg16[...] = wg_ref[0].astype(BF16)
        wu16[...] = wu_ref[0].astype(BF16)
        wd16[...] = wd_ref[0].astype(BF16)

    @pl.when(i < nused)
    def _():
        x = _from_token_tiles(x_ref, 0, bm).astype(BF16)
        g = _dot(x, wg16[...])
        u = _dot(x, wu16[...])
        a = (g * jax.nn.sigmoid(g) * u).astype(BF16)
        _to_token_tiles(y_ref, _dot(a, wd16[...]))

    @pl.when(i >= nused)
    def _():
        y_ref[...] = jnp.zeros_like(y_ref)


def _experts(blk_e, nused, xb, wg, wu, wd):
    nb = blk_e.shape[0]
    bm = MOE_BLOCK
    D, de = wg.shape[1:]
    assert D == SUBLANES * LANES and xb.shape == (nb * bm * SUBLANES, LANES)
    grid_spec = pltpu.PrefetchScalarGridSpec(
        num_scalar_prefetch=2,
        grid=(nb,),
        in_specs=[pl.BlockSpec((bm * SUBLANES, LANES), lambda i, be, nu: (jnp.minimum(i, nu[0] - 1), 0)),
                  pl.BlockSpec((1, D, de), lambda i, be, nu: (be[i], 0, 0)),
                  pl.BlockSpec((1, D, de), lambda i, be, nu: (be[i], 0, 0)),
                  pl.BlockSpec((1, de, D), lambda i, be, nu: (be[i], 0, 0))],
        out_specs=pl.BlockSpec((bm * SUBLANES, LANES), lambda i, be, nu: (i, 0)),
        scratch_shapes=[pltpu.VMEM((D, de), BF16), pltpu.VMEM((D, de), BF16), pltpu.VMEM((de, D), BF16)],
    )
    return pl.pallas_call(
        _expert_kernel,
        grid_spec=grid_spec,
        out_shape=jax.ShapeDtypeStruct((nb * bm * SUBLANES, LANES), F32),
        compiler_params=_params(("arbitrary",)),
        name="experts",
    )(blk_e, nused, xb, wg, wu, wd)


def _combine_kernel(n_steps, cur_ref, nxt_ref, y_hbm, x1_ref, route_ref, gt_ref, gf_ref, o_ref, buf, sem):
    i = pl.program_id(0)
    tm = x1_ref.shape[0]
    slot = i % 2
    rows = 2 * tm * SUBLANES

    @pl.when(i == 0)
    def _():
        _gather_start(cur_ref, y_hbm, buf, 0, 2 * tm, sem.at[0])

    @pl.when(i + 1 < n_steps)
    def _():
        _gather_start(nxt_ref, y_hbm, buf, (1 - slot) * rows, 2 * tm, sem.at[1 - slot])

    _gather_wait(y_hbm, buf, slot * rows, 2 * tm, sem.at[slot])
    route = route_ref[...]
    y1 = _from_token_tiles(buf, slot * rows, tm)
    y2 = _from_token_tiles(buf, slot * rows + tm * SUBLANES, tm)
    x = x1_ref[...] + gt_ref[0] * (route[:, 2:3] * y1 + route[:, 3:4] * y2)
    o_ref[...] = x * lax.rsqrt(jnp.mean(x * x, axis=-1, keepdims=True) + EPS) * gf_ref[...]


def _combine(tile0, n_tok, dest_tiles, yb, x1, route, gt2, g_final, L):
    T, D = x1.shape
    tm = TOKEN_TILE
    per_b = L // tm
    n = n_tok // tm
    last = tile0 + n - 1
    smem = lambda f: pl.BlockSpec((1, 1, 2 * tm), lambda i: (f(i), 0, 0), memory_space=pltpu.SMEM)
    return pl.pallas_call(
        functools.partial(_combine_kernel, n),
        grid=(n,),
        in_specs=[smem(lambda i: tile0 + i), smem(lambda i: jnp.minimum(tile0 + i + 1, last)),
                  pl.BlockSpec(memory_space=pl.ANY),
                  pl.BlockSpec((tm, D), lambda i: (tile0 + i, 0)),
                  pl.BlockSpec((tm, LANES), lambda i: (tile0 + i, 0)),
                  pl.BlockSpec((1, 1, D), lambda i: ((tile0 + i) // per_b, 0, 0)),
                  pl.BlockSpec((1, D), lambda i: (0, 0))],
        out_specs=pl.BlockSpec((tm, D), lambda i: (i, 0)),
        out_shape=jax.ShapeDtypeStruct((n_tok, D), F32),
        scratch_shapes=[pltpu.VMEM((2 * 2 * tm * SUBLANES, LANES), F32), pltpu.SemaphoreType.DMA((2,))],
        compiler_params=_params(("arbitrary",)),
        name="combine",
    )(dest_tiles, dest_tiles, yb, x1, route, gt2, g_final.reshape(1, D))


def _encoder(xp, xs, c, p, g_final):
    n_prompt, L, D = xp.shape
    B = n_prompt + xs.shape[0]
    T = B * L
    xa, xb = xp.reshape(-1, D), xs.reshape(-1, D)
    d_f = N_FOURIER_GROUPS * FOURIER_GROUP_DIM
    d_h = (HYENA_ORDER + 1) * D_HYENA
    d_g = 2 * D

    mod = _ada(c, p["w_ada"], p["b_ada"])
    sh1, sc1, gt1, sh2, sc2, gt2 = [mod[:, k * D:(k + 1) * D].reshape(B, 1, D) for k in range(6)]

    uf, uh, sg = _inproj(xa, xb, sc1, sh1, p["w_in"].astype(BF16), p["b_in"][None, :], L, d_f, d_h, d_g)
    f = _fourier(uf.reshape(B, L, d_f))

    ktime, ksum = _filt_time(L, p["filt_w1"], p["filt_b1"], p["filt_w2"], p["filt_b2"],
                             p["filt_w3"], p["filt_b3"], p["filt_freq"], p["filt_wout"])
    kf = _filt_fft(L, ktime, ksum)
    uh3 = uh.reshape(B, L, d_h)
    conv_b = p["conv_b"][None, :]
    ncol = D_HYENA // FFT_COLS
    skip = p["hyena_skip"]
    z = _hyena_order(0, uh3, 0, uh3, ncol, p["conv_w"], conv_b, skip[0:1], kf, L)
    z = _hyena_order(1, z, 0, uh3, 2 * ncol, p["conv_w"], conv_b, skip[1:2], kf, L)

    rw = jnp.zeros((D, LANES), F32).at[:, :N_GROUPS].set(p["router_w1"])
    rw = rw.at[:, N_GROUPS:N_GROUPS + N_EXPERTS].set(p["router_w2"])
    rb = jnp.zeros((1, LANES), F32).at[0, :N_GROUPS].set(p["router_b1"])
    rb = rb.at[0, N_GROUPS:N_GROUPS + N_EXPERTS].set(p["router_b2"])
    x1, h2, route, route_t, counts = _merge(f.reshape(T, d_f), z.reshape(T, D_HYENA), sg, xa, xb,
                                            gt1, sc2, sh2, p["w_four"].astype(BF16), p["w_hyena"].astype(BF16),
                                            p["w_out"].astype(BF16), rw, rb, L)

    bm = MOE_BLOCK
    tm = TOKEN_TILE
    nb = (T * TOP_K) // bm + N_EXPERTS
    e = route_t[0:2].astype(jnp.int32)
    rank = route_t[4:6].astype(jnp.int32)
    cnt = counts[0, :N_EXPERTS].astype(jnp.int32)
    pcnt = (cnt + bm - 1) // bm * bm
    pend = jnp.cumsum(pcnt)
    experts = jnp.arange(N_EXPERTS, dtype=jnp.int32)
    dest = rank + jnp.sum(jnp.where(e[..., None] == experts, pend - pcnt, 0), axis=-1)
    blk_row0 = jnp.arange(nb, dtype=jnp.int32)[:, None] * bm
    blk_e = jnp.minimum(jnp.sum((pend[None, :] <= blk_row0).astype(jnp.int32), axis=1), N_EXPERTS - 1)
    nused = (pend[-1] // bm).astype(jnp.int32).reshape(1)
    dest_tiles = dest.reshape(TOP_K, T // tm, tm).transpose(1, 0, 2).reshape(T // tm, 1, TOP_K * tm)
    xb = _dispatch(pend.astype(jnp.int32), dest_tiles, h2, nb * bm)
    yb = _experts(blk_e, nused, xb, p["exp_w_gate"], p["exp_w_up"], p["exp_w_down"])

    t_prompt = n_prompt * L
    outs = []
    for tile0, n_tok in ((0, t_prompt), (t_prompt // tm, T - t_prompt)):
        outs.append(_combine(tile0, n_tok, dest_tiles, yb, x1, route, gt2, g_final, L))
    return outs[0].reshape(xp.shape), outs[1].reshape(xs.shape)


def kernel(x_prompt, x_sample, c_prompt, c_sample, w_ada, b_ada, w_in, b_in, conv_w, conv_b, filt_w1, filt_b1, filt_w2, filt_b2, filt_w3, filt_b3, filt_freq, filt_wout, hyena_skip, w_four, w_hyena, w_out, router_w1, router_b1, router_w2, router_b2, exp_w_gate, exp_w_up, exp_w_down, g_final):
    assert w_ada.shape[0] == 1, "single-layer block"
    assert x_prompt.shape[1:] == x_sample.shape[1:], "both request groups share sequence length and width"
    p = dict(w_ada=w_ada[0], b_ada=b_ada[0], w_in=w_in[0], b_in=b_in[0], conv_w=conv_w[0], conv_b=conv_b[0],
             filt_w1=filt_w1[0], filt_b1=filt_b1[0], filt_w2=filt_w2[0], filt_b2=filt_b2[0],
             filt_w3=filt_w3[0], filt_b3=filt_b3[0], filt_freq=filt_freq[0], filt_wout=filt_wout[0],
             hyena_skip=hyena_skip[0], w_four=w_four[0], w_hyena=w_hyena[0], w_out=w_out[0],
             router_w1=router_w1[0], router_b1=router_b1[0], router_w2=router_w2[0], router_b2=router_b2[0],
             exp_w_gate=exp_w_gate[0], exp_w_up=exp_w_up[0], exp_w_down=exp_w_down[0])
    c = jnp.concatenate([c_prompt, c_sample], axis=0)
    return _encoder(x_prompt, x_sample, c, p, g_final)
```

```python
import functools
import math

import numpy as np
import jax
import jax.numpy as jnp
from jax import lax
from jax.experimental import pallas as pl
from jax.experimental.pallas import tpu as pltpu

F32 = jnp.float32
BF16 = jnp.bfloat16
HI = lax.Precision.HIGHEST

EPS = 1e-6
FFT_MINOR = 64
N_FOURIER_GROUPS = 4
FOURIER_GROUP_DIM = 128
D_HYENA = 512
HYENA_ORDER = 2
POS_BANDS = 16
N_GROUPS = 4
EXPERTS_PER_GROUP = 8
N_EXPERTS = 32
TOP_K = 2
SHORT_DECAY_PCT = 0.3
LONG_DECAY_PCT = 1.5
DECAY_TARGET = 1e-2
LANES = 128
SUBLANES = 8
VMEM_LIMIT = 56 * 1024 * 1024

TOKEN_TILE = 512
MERGE_SUBTILE = 128
MOE_BLOCK = 512
FFT_COLS = LANES
ROW_CHUNK = 256
FFT_UNROLL = 64
DMA_GROUP = 8


def _dot(a, b):
    return jnp.dot(a, b, preferred_element_type=F32)


def _dot_hi(a, b):
    return jnp.dot(a, b, preferred_element_type=F32, precision=HI)


def _split_bf16(x):
    hi = x.astype(BF16)
    return hi, (x - hi.astype(F32)).astype(BF16)


def _dot3(a, b):
    a_hi, a_lo = _split_bf16(a)
    b_hi, b_lo = _split_bf16(b)
    return _dot(a_hi, b_hi) + (_dot(a_lo, b_hi) + _dot(a_hi, b_lo))


def _params(sem=None):
    return pltpu.CompilerParams(dimension_semantics=sem, vmem_limit_bytes=VMEM_LIMIT)


def _const_spec(shape):
    nd = len(shape)
    return pl.BlockSpec(shape, lambda *_: (0,) * nd, pipeline_mode=pl.Buffered(1))


def _pitch(rows):
    p = -(-rows // SUBLANES)
    return SUBLANES * (p if p % 2 else p + 1)


@functools.lru_cache(maxsize=None)
def _stage2_tables():
    i = np.arange(FFT_MINOR)
    ph = 2.0 * np.pi * np.outer(i, i) / FFT_MINOR
    c, s = np.cos(ph), np.sin(ph)
    fwd = np.block([[c, s], [-s, c]])
    inv = np.block([[c, -s], [s, c]])
    return fwd, inv, c, s


@functools.lru_cache(maxsize=None)
def _hyena_tables(L):
    n_fft = 2 * L
    m = FFT_MINOR
    na = n_fft // m
    nh = na // 2
    b = np.arange(m)[:, None, None]
    ka = np.arange(nh + 1)[None, :, None]

    def forward(a_count):
        a = np.arange(a_count)[None, None, :]
        th = 2.0 * np.pi * ((ka * (m * a + b)) % n_fft) / n_fft
        c, s = np.cos(th), -np.sin(th)
        pad = np.zeros((m, SUBLANES - 2, a_count))
        return np.concatenate([c[:, :nh], s[:, :nh], c[:, nh:], s[:, nh:], pad], axis=1)

    fwd_half = forward(nh)
    fwd_full = forward(na)
    weight = np.where(np.arange(nh) == 0, 1.0, 2.0)[None, :, None]
    inv_half = np.transpose(fwd_half[:, :2 * nh] * np.concatenate([weight, weight], axis=1), (0, 2, 1)) / n_fft
    sign = np.broadcast_to(((-1.0) ** np.arange(nh))[:, None] / n_fft, (nh, LANES))
    ph = np.pi * np.arange(m) / m
    cos_b = np.broadcast_to(np.cos(ph)[:, None], (m, LANES))
    sin_b = np.broadcast_to(np.sin(ph)[:, None], (m, LANES))
    return fwd_half, inv_half, fwd_full, sign, cos_b, sin_b


@functools.lru_cache(maxsize=None)
def _fourier_tables(L):
    na = L // FFT_MINOR
    b = np.arange(FFT_MINOR)[:, None, None]
    ka = np.arange(na)[None, :, None]
    a = np.arange(na)[None, None, :]
    th = 2.0 * np.pi * ((ka * (FFT_MINOR * a + b)) % L) / L
    c, s = np.cos(th), np.sin(th)
    st1 = np.concatenate([np.concatenate([c, s], axis=2),
                          np.concatenate([-s, c], axis=2)], axis=1)
    _, _, c2, s2 = _stage2_tables()
    st2 = np.concatenate([c2, s2], axis=1) / math.sqrt(L)
    return st1, st2


def _mxu_table(table):
    return jnp.asarray(table, F32).astype(BF16)


@functools.lru_cache(maxsize=None)
def _channel_dft():
    i = np.arange(FOURIER_GROUP_DIM)
    ph = 2.0 * np.pi * np.outer(i, i) / FOURIER_GROUP_DIM
    return np.concatenate([np.cos(ph), -np.sin(ph)], axis=1) / math.sqrt(FOURIER_GROUP_DIM)


def _ada_kernel(c_ref, w_ref, b_ref, o_ref):
    c = c_ref[...]
    o_ref[...] = _dot_hi(c * jax.nn.sigmoid(c), w_ref[...]) + b_ref[...]


def _ada(c, w_ada, b_ada):
    nb, d = c.shape
    n = w_ada.shape[1]
    tn = 1536
    return pl.pallas_call(
        _ada_kernel,
        grid=(n // tn,),
        in_specs=[pl.BlockSpec((nb, d), lambda j: (0, 0)),
                  pl.BlockSpec((d, tn), lambda j: (0, j)),
                  pl.BlockSpec((1, tn), lambda j: (0, j))],
        out_specs=pl.BlockSpec((nb, tn), lambda j: (0, j)),
        out_shape=jax.ShapeDtypeStruct((nb, n), F32),
        compiler_params=_params(("arbitrary",)),
        name="ada",
    )(c, w_ada, b_ada.reshape(1, n))


def _for_row_chunks(n_rows, body, unroll=1):
    def step(j, carry):
        body(j, pl.multiple_of(j * ROW_CHUNK, ROW_CHUNK))
        return carry

    lax.fori_loop(0, n_rows // ROW_CHUNK, step, 0, unroll=unroll)


def _store_padded(dst, j, val, run=FFT_MINOR):
    pz = _pitch(run)
    per = ROW_CHUNK // run
    for i in range(per):
        dst[pl.ds(pl.multiple_of((j * per + i) * pz, SUBLANES), run), :] = val[i * run:(i + 1) * run]


def _load_padded(src, j, run=FFT_MINOR):
    pz = _pitch(run)
    per = ROW_CHUNK // run
    return jnp.concatenate([src[pl.ds(pl.multiple_of((j * per + i) * pz, SUBLANES), run), :] for i in range(per)],
                           axis=0)


def _short_conv_chunk(src_ref, w_ref, b_ref, L, j, r0):
    R = ROW_CHUNK
    ct = src_ref.shape[-1]
    row = lax.broadcasted_iota(jnp.int32, (R, ct), 0)
    cur = src_ref[0, pl.ds(r0, R), :]
    before = src_ref[0, pl.ds(jnp.maximum(r0 - 1, 0), 1), :] * jnp.where(j > 0, 1.0, 0.0)
    after = src_ref[0, pl.ds(jnp.minimum(r0 + R, L - 1), 1), :] * jnp.where(j < L // R - 1, 1.0, 0.0)
    up = jnp.where(row == 0, before, pltpu.roll(cur, 1, 0))
    dn = jnp.where(row == R - 1, after, pltpu.roll(cur, R - 1, 0))
    return w_ref[0:1, :] * up + w_ref[1:2, :] * cur + w_ref[2:3, :] * dn + b_ref[...]


def _filt_time_kernel(L, rows, bands_ref, delt_ref, w1_ref, b1_ref, w2_ref, b2_ref, w3_ref, b3_ref,
                      fr_ref, wo_ref, k_ref, sum_ref):
    i = pl.program_id(0)
    n = i * rows + lax.broadcasted_iota(jnp.int32, (rows, 1), 0)
    fwd = n < L
    pos = jnp.where(fwd, n, 2 * L - n).astype(F32)
    t = pos * (1.0 / (L - 1))
    ang = (2.0 * math.pi / L) * pos * bands_ref[...]
    w1 = w1_ref[...]
    pre = (t * w1[0:1, :] + _dot3(jnp.cos(ang), w1[1:1 + POS_BANDS, :])
           + _dot3(-jnp.sin(ang), w1[1 + POS_BANDS:, :]) + b1_ref[...])
    fr = fr_ref[...]
    h = jnp.sin(fr[0:1, :] * pre)
    h = jnp.sin(fr[1:2, :] * (_dot3(h, w2_ref[...]) + b2_ref[...]))
    h = jnp.sin(fr[2:3, :] * (_dot3(h, w3_ref[...]) + b3_ref[...]))
    window = jnp.exp(-t * delt_ref[...])
    live = n != L
    tile_fwd = i < (L // rows)
    parts = []
    for o in range(HYENA_ORDER):
        base = o * 2 * D_HYENA
        w = jnp.where(tile_fwd, wo_ref[:, base:base + D_HYENA], wo_ref[:, base + D_HYENA:base + 2 * D_HYENA])
        parts.append(jnp.where(live, _dot3(h, w) * window, 0.0))
    k = jnp.concatenate(parts, axis=1)
    k_ref[...] = k

    @pl.when(i == 0)
    def _():
        sum_ref[...] = jnp.zeros_like(sum_ref)

    sum_ref[...] += jnp.sum(jnp.abs(k), axis=0, keepdims=True)


def _filt_time(L, w1, b1, w2, b2, w3, b3, freq, wout):
    rows = 512
    assert L % rows == 0, "each row tile must hold one filter direction only"
    n_fft = 2 * L
    cols = HYENA_ORDER * D_HYENA
    bands = jnp.linspace(1e-4, POS_BANDS - 1, POS_BANDS, dtype=F32).reshape(1, POS_BANDS)
    max_decay = math.log(DECAY_TARGET) / SHORT_DECAY_PCT
    min_decay = math.log(DECAY_TARGET) / LONG_DECAY_PCT
    deltas = jnp.abs(jnp.linspace(min_decay, max_decay, D_HYENA, dtype=F32)).reshape(1, D_HYENA)
    args = (bands, deltas, w1, b1.reshape(1, -1), w2, b2.reshape(1, -1), w3, b3.reshape(1, -1), freq, wout)
    return pl.pallas_call(
        functools.partial(_filt_time_kernel, L, rows),
        grid=(n_fft // rows,),
        in_specs=[pl.BlockSpec(a.shape, lambda i: (0, 0)) for a in args],
        out_specs=[pl.BlockSpec((rows, cols), lambda i: (i, 0)),
                   pl.BlockSpec((1, cols), lambda i: (0, 0))],
        out_shape=[jax.ShapeDtypeStruct((n_fft, cols), F32),
                   jax.ShapeDtypeStruct((1, cols), F32)],
        compiler_params=_params(("arbitrary",)),
        name="filt_time",
    )(*args)


def _filt_fft_kernel(na, k_ref, sum_ref, st1_ref, st2_ref, o_ref, kp, s1):
    m = FFT_MINOR
    nh = na // 2
    rows1 = 2 * nh + SUBLANES
    pz, p1 = _pitch(m), _pitch(rows1)
    _for_row_chunks(na * m, lambda j, r0: _store_padded(kp, j, k_ref[pl.ds(r0, ROW_CHUNK), :]))

    def stage1(b, carry):
        slab = kp[pl.ds(b, na, stride=pz), :]
        s1[pl.ds(pl.multiple_of(b * p1, SUBLANES), rows1), :] = _dot3(st1_ref[b], slab)
        return carry

    lax.fori_loop(0, m, stage1, 0, unroll=4)
    inv_norm = 1.0 / sum_ref[...]

    def spectrum_row(re_row, im_row):
        a = jnp.concatenate([s1[pl.ds(re_row, m, stride=p1), :], s1[pl.ds(im_row, m, stride=p1), :]], axis=0)
        return _dot3(st2_ref[...], a) * inv_norm

    def stage2(ka, carry):
        o_ref[ka] = spectrum_row(ka, nh + ka)
        return carry

    lax.fori_loop(0, nh, stage2, 0, unroll=4)
    o_ref[nh] = spectrum_row(2 * nh, 2 * nh + 1)


def _filt_fft(L, ktime, ksum):
    n_fft, cols = ktime.shape
    m = FFT_MINOR
    na = n_fft // m
    nh = na // 2
    rows1 = 2 * nh + SUBLANES
    ct = FFT_COLS
    fwd_full = _hyena_tables(L)[2]
    st2f = _stage2_tables()[0]
    return pl.pallas_call(
        functools.partial(_filt_fft_kernel, na),
        grid=(cols // ct,),
        in_specs=[pl.BlockSpec((n_fft, ct), lambda j: (0, j)),
                  pl.BlockSpec((1, ct), lambda j: (0, j)),
                  _const_spec((m, rows1, na)),
                  _const_spec((2 * m, 2 * m))],
        out_specs=pl.BlockSpec((nh + 1, 2 * m, ct), lambda j: (0, 0, j)),
        out_shape=jax.ShapeDtypeStruct((nh + 1, 2 * m, cols), F32),
        scratch_shapes=[pltpu.VMEM((na * _pitch(m), ct), F32),
                        pltpu.VMEM((m * _pitch(rows1), ct), F32)],
        compiler_params=_params(("arbitrary",)),
        name="filt_fft",
    )(ktime, ksum, jnp.asarray(fwd_full, F32), jnp.asarray(st2f, F32))


def _rms_mod(x, scale, shift):
    y = x * lax.rsqrt(jnp.mean(x * x, axis=-1, keepdims=True) + EPS)
    return y * (1.0 + scale) + shift


def _group_tile(n_first, a_ref, b_ref):
    return jnp.where(pl.program_id(0) < n_first, a_ref[...], b_ref[...])


def _group_specs(n_first, tm, D):
    return [pl.BlockSpec((tm, D), lambda i: (jnp.minimum(i, n_first - 1), 0)),
            pl.BlockSpec((tm, D), lambda i: (jnp.maximum(i - n_first, 0), 0))]


def _inproj_kernel(d_f, d_h, n_first, xa_ref, xb_ref, sc_ref, sh_ref, w_ref, b_ref, uf_ref, uh_ref, sg_ref):
    h = _rms_mod(_group_tile(n_first, xa_ref, xb_ref), sc_ref[0], sh_ref[0]).astype(BF16)

    def proj(c0, width):
        return _dot(h, w_ref[:, c0:c0 + width]) + b_ref[:, c0:c0 + width]

    uf_ref[...] = proj(0, d_f)
    chunk = 512
    for c in range(0, d_h, chunk):
        uh_ref[:, c:c + chunk] = proj(d_f + c, chunk)
    d_g = sg_ref.shape[1]
    for c in range(0, d_g, chunk):
        sg_ref[:, c:c + chunk] = jax.nn.sigmoid(proj(d_f + d_h + c, chunk)).astype(BF16)


def _inproj(xa, xb, sc, sh, w, b, L, d_f, d_h, d_g):
    D = xa.shape[1]
    T = xa.shape[0] + xb.shape[0]
    tm = TOKEN_TILE
    per_b = L // tm
    n_first = xa.shape[0] // tm
    n = w.shape[1]
    mod_spec = pl.BlockSpec((1, 1, D), lambda i: (i // per_b, 0, 0))
    return pl.pallas_call(
        functools.partial(_inproj_kernel, d_f, d_h, n_first),
        grid=(T // tm,),
        in_specs=_group_specs(n_first, tm, D) + [mod_spec, mod_spec, _const_spec((D, n)), _const_spec((1, n))],
        out_specs=[pl.BlockSpec((tm, d_f), lambda i: (i, 0)),
                   pl.BlockSpec((tm, d_h), lambda i: (i, 0)),
                   pl.BlockSpec((tm, d_g), lambda i: (i, 0))],
        out_shape=[jax.ShapeDtypeStruct((T, d_f), F32),
                   jax.ShapeDtypeStruct((T, d_h), F32), jax.ShapeDtypeStruct((T, d_g), BF16)],
        compiler_params=_params(("parallel",)),
        name="inproj",
    )(xa, xb, sc, sh, w, b)


def _fourier_kernel(na, u_ref, cdft_ref, st1_ref, st2_ref, o_ref, zpr, zpi, s1, op):
    m = FFT_MINOR
    pz, p1 = _pitch(m), _pitch(2 * na)

    def fill(j, r0):
        z = _dot(u_ref[0, pl.ds(r0, ROW_CHUNK), :].astype(BF16), cdft_ref[...])
        _store_padded(zpr, j, z[:, :FOURIER_GROUP_DIM])
        _store_padded(zpi, j, z[:, FOURIER_GROUP_DIM:])

    _for_row_chunks(na * m, fill, unroll=8)

    def stage1(b, carry):
        z = jnp.concatenate([zpr[pl.ds(b, na, stride=pz), :], zpi[pl.ds(b, na, stride=pz), :]], axis=0)
        s1[pl.ds(pl.multiple_of(b * p1, SUBLANES), 2 * na), :] = _dot(st1_ref[b], z.astype(BF16))
        return carry

    lax.fori_loop(0, m, stage1, 0, unroll=FFT_UNROLL)

    po = _pitch(na)

    def stage2(ka, carry):
        a = jnp.concatenate([s1[pl.ds(ka, m, stride=p1), :], s1[pl.ds(na + ka, m, stride=p1), :]], axis=0)
        op[pl.ds(ka, m, stride=po), :] = _dot(st2_ref[...], a.astype(BF16))
        return carry

    lax.fori_loop(0, na, stage2, 0, unroll=FFT_UNROLL)

    def unpad(j, r0):
        o_ref[0, pl.ds(r0, ROW_CHUNK), :] = _load_padded(op, j, na)

    _for_row_chunks(na * m, unpad)


def _fourier(u):
    B, L, C = u.shape
    m = FFT_MINOR
    na = L // m
    ct = FOURIER_GROUP_DIM
    assert ct == FFT_COLS
    st1, st2 = _fourier_tables(L)
    spec = pl.BlockSpec((1, L, ct), lambda j, b: (b, 0, j))
    pad = pltpu.VMEM((na * _pitch(m), ct), F32)
    return pl.pallas_call(
        functools.partial(_fourier_kernel, na),
        grid=(C // ct, B),
        in_specs=[spec, _const_spec((ct, 2 * ct)), _const_spec((m, 2 * na, 2 * na)), _const_spec((m, 2 * m))],
        out_specs=spec,
        out_shape=jax.ShapeDtypeStruct((B, L, C), F32),
        scratch_shapes=[pad, pad, pltpu.VMEM((m * _pitch(2 * na), ct), F32),
                        pltpu.VMEM((m * _pitch(na), ct), F32)],
        compiler_params=_params(("parallel", "parallel")),
        name="fourier",
    )(u, _mxu_table(_channel_dft()), _mxu_table(st1), _mxu_table(st2))


def _hyena_kernel(conv_z, L, zin_ref, gin_ref, cwz_ref, cbz_ref, cwg_ref, cbg_ref, skip_ref, kf_ref,
                  fwd1_ref, inv1_ref, st2f_ref, st2i_ref, sign_ref, cosb_ref, sinb_ref, out_ref, zp, yp, s1, g2, hh):
    m = FFT_MINOR
    nh = L // m
    rows1 = 2 * nh + SUBLANES
    pz, p1, p2 = _pitch(m), _pitch(rows1), _pitch(2 * m)

    def fill(j, r0):
        if conv_z:
            _store_padded(zp, j, _short_conv_chunk(zin_ref, cwz_ref, cbz_ref, L, j, r0))
        else:
            _store_padded(zp, j, zin_ref[0, pl.ds(r0, ROW_CHUNK), :])

    _for_row_chunks(L, fill)

    def stage1(b, carry):
        slab = zp[pl.ds(b, nh, stride=pz), :].astype(BF16)
        s1[pl.ds(pl.multiple_of(b * p1, SUBLANES), rows1), :] = _dot(fwd1_ref[b], slab)
        return carry

    lax.fori_loop(0, m, stage1, 0, unroll=FFT_UNROLL)

    def spectrum_row(ka, re_row, im_row):
        a = jnp.concatenate([s1[pl.ds(re_row, m, stride=p1), :], s1[pl.ds(im_row, m, stride=p1), :]], axis=0)
        x = _dot(st2f_ref[...], a.astype(BF16))
        xr, xi = x[:m], x[m:]
        k = kf_ref[ka]
        kr, ki = k[:m], k[m:]
        y = jnp.concatenate([xr * kr - xi * ki, xr * ki + xi * kr], axis=0).astype(BF16)
        return _dot(st2i_ref[...], y)

    def stage2(ka, carry):
        g2[pl.ds(pl.multiple_of(ka * p2, SUBLANES), 2 * m), :] = spectrum_row(ka, ka, nh + ka)
        return carry

    lax.fori_loop(0, nh, stage2, 0, unroll=FFT_UNROLL)
    g_mid = spectrum_row(nh, 2 * nh, 2 * nh + 1)
    hh[...] = cosb_ref[...] * g_mid[:m] - sinb_ref[...] * g_mid[m:]
    sign = sign_ref[...]

    def stage3(b, carry):
        g = jnp.concatenate([g2[pl.ds(b, nh, stride=p2), :], g2[pl.ds(m + b, nh, stride=p2), :]], axis=0)
        yp[pl.ds(b, nh, stride=pz), :] = _dot(inv1_ref[b], g.astype(BF16)) + sign * hh[pl.ds(b, 1), :]
        return carry

    lax.fori_loop(0, m, stage3, 0, unroll=FFT_UNROLL)
    skip = skip_ref[...]

    def finish(j, r0):
        gate = _short_conv_chunk(gin_ref, cwg_ref, cbg_ref, L, j, r0)
        out_ref[0, pl.ds(r0, ROW_CHUNK), :] = gate * (_load_padded(yp, j) + skip * _load_padded(zp, j))

    _for_row_chunks(L, finish)


def _hyena_order(order, zin, z_col0, uh, g_col0, conv_w, conv_b, skip, kf, L):
    B = uh.shape[0]
    m = FFT_MINOR
    ct = FFT_COLS
    ncol = D_HYENA // ct
    nh = L // m
    rows1 = 2 * nh + SUBLANES
    conv_z = order == 0
    fwd_half, inv_half, _, sign, cos_b, sin_b = _hyena_tables(L)
    st2f, st2i, _, _ = _stage2_tables()
    z_cols = (lambda j, b: (b, 0, z_col0 + j))
    g_cols = (lambda j, b: (b, 0, g_col0 + j))
    zw_col0 = z_col0 if conv_z else 0
    pad = pltpu.VMEM((nh * _pitch(m), ct), F32)
    return pl.pallas_call(
        functools.partial(_hyena_kernel, conv_z, L),
        grid=(ncol, B),
        in_specs=[pl.BlockSpec((1, L, ct), z_cols),
                  pl.BlockSpec((1, L, ct), g_cols),
                  pl.BlockSpec((3, ct), lambda j, b: (0, zw_col0 + j)),
                  pl.BlockSpec((1, ct), lambda j, b: (0, zw_col0 + j)),
                  pl.BlockSpec((3, ct), lambda j, b: (0, g_col0 + j)),
                  pl.BlockSpec((1, ct), lambda j, b: (0, g_col0 + j)),
                  pl.BlockSpec((1, ct), lambda j, b: (0, j)),
                  pl.BlockSpec((nh + 1, 2 * m, ct), lambda j, b: (0, 0, order * ncol + j),
                               pipeline_mode=pl.Buffered(1)),
                  _const_spec((m, rows1, nh)),
                  _const_spec((m, nh, 2 * nh)),
                  _const_spec((2 * m, 2 * m)),
                  _const_spec((2 * m, 2 * m)),
                  _const_spec((nh, LANES)), _const_spec((m, LANES)), _const_spec((m, LANES))],
        out_specs=pl.BlockSpec((1, L, ct), lambda j, b: (b, 0, j)),
        out_shape=jax.ShapeDtypeStruct((B, L, D_HYENA), F32),
        scratch_shapes=[pad, pad,
                        pltpu.VMEM((m * _pitch(rows1), ct), F32),
                        pltpu.VMEM((nh * _pitch(2 * m), ct), F32),
                        pltpu.VMEM((m, ct), F32)],
        compiler_params=_params(("parallel", "arbitrary")),
        name=f"hyena{order}",
    )(zin, uh, conv_w, conv_b, conv_w, conv_b, skip, kf,
      _mxu_table(fwd_half), _mxu_table(inv_half), _mxu_table(st2f), _mxu_table(st2i),
      jnp.asarray(sign, F32), jnp.asarray(cos_b, F32), jnp.asarray(sin_b, F32))


def _merge_kernel(n_first, f_ref, z_ref, sg_ref, xa_ref, xb_ref, gt_ref, sc_ref, sh_ref, wf_ref, wh_ref, wo_ref,
                  rw_ref, rb_ref, tri_ref, x1_ref, h2_ref, route_ref, rt_ref, cnt_ref):
    i = pl.program_id(0)
    tm, D = xa_ref.shape
    sub = MERGE_SUBTILE
    lane = lax.broadcasted_iota(jnp.int32, (sub, LANES), 1).astype(F32)
    neg = -1e30

    def first_max(v):
        mx = jnp.max(v, axis=1, keepdims=True)
        return mx, jnp.min(jnp.where(v == mx, lane, float(LANES)), axis=1, keepdims=True)

    @pl.when(i == 0)
    def _():
        cnt_ref[...] = jnp.zeros_like(cnt_ref)

    def sub_tile(r0, count):
        rows = pl.ds(r0, sub)
        ya = _dot(f_ref[rows, :].astype(BF16), wf_ref[...])
        yb = _dot(z_ref[rows, :].astype(BF16), wh_ref[...])
        merged = sg_ref[rows, :D].astype(F32) * ya + sg_ref[rows, D:].astype(F32) * yb
        x = jnp.where(i < n_first, xa_ref[rows, :], xb_ref[rows, :])
        x1 = x + gt_ref[0] * _dot(merged.astype(BF16), wo_ref[...])
        x1_ref[rows, :] = x1
        h2 = _rms_mod(x1, sc_ref[0], sh_ref[0])
        _to_token_tiles(h2_ref, h2, r0 * SUBLANES)

        h_hi, h_lo = _split_bf16(h2)
        logits = (_dot(h_hi, rw_hi) + (_dot(h_lo, rw_hi) + _dot(h_hi, rw_lo))) + rb_ref[...]
        gl = jnp.where(lane < N_GROUPS, logits, neg)
        gmax, g = first_max(gl)
        p_g = 1.0 / jnp.sum(jnp.exp(gl - gmax), axis=1, keepdims=True)
        lo = N_GROUPS + EXPERTS_PER_GROUP * g
        el = jnp.where((lane >= lo) & (lane < lo + EXPERTS_PER_GROUP), logits, neg)
        m1, i1 = first_max(el)
        m2, i2 = first_max(jnp.where(lane == i1, neg, el))
        r = jnp.exp(m2 - m1)
        wt1 = p_g / (1.0 + r)
        wt2 = p_g * r / (1.0 + r)
        e1 = i1 - N_GROUPS
        e2 = i2 - N_GROUPS

        onehot = ((lane == e1) | (lane == e2)).astype(BF16)
        before = _dot(tri_ref[...], onehot) + count
        r1 = jnp.sum(jnp.where(lane == e1, before, 0.0), axis=1, keepdims=True)
        r2 = jnp.sum(jnp.where(lane == e2, before, 0.0), axis=1, keepdims=True)

        packed = jnp.zeros((sub, LANES), F32)
        for slot, v in enumerate((e1, e2, wt1, wt2, r1, r2)):
            packed = jnp.where(lane == slot, v, packed)
        route_ref[rows, :] = packed
        rt_ref[:, rows] = packed.T[0:SUBLANES, :]
        return count + jnp.sum(onehot.astype(F32), axis=0, keepdims=True)

    rw_hi, rw_lo = _split_bf16(rw_ref[...])
    count = cnt_ref[...]
    for h in range(tm // sub):
        count = sub_tile(h * sub, count)
    cnt_ref[...] = count


def _merge(f2d, z2d, sg, xa, xb, gt1, sc2, sh2, w_four, w_hyena, w_out, rw, rb, L):
    D = xa.shape[1]
    T = xa.shape[0] + xb.shape[0]
    assert D == SUBLANES * LANES
    tm = TOKEN_TILE
    per_b = L // tm
    n_first = xa.shape[0] // tm
    d_f = f2d.shape[1]
    d_h = z2d.shape[1]
    sub = MERGE_SUBTILE
    tri = jnp.asarray(np.tril(np.ones((sub, sub)), -1), BF16)
    mod_spec = pl.BlockSpec((1, 1, D), lambda i: (i // per_b, 0, 0))
    row = lambda w: pl.BlockSpec((tm, w), lambda i: (i, 0))
    return pl.pallas_call(
        functools.partial(_merge_kernel, n_first),
        grid=(T // tm,),
        in_specs=[row(d_f), row(d_h), row(2 * D)] + _group_specs(n_first, tm, D) + [
            mod_spec, mod_spec, mod_spec,
            _const_spec((d_f, D)), _const_spec((d_h, D)), _const_spec((D, D)),
            _const_spec((D, LANES)), _const_spec((1, LANES)), _const_spec((sub, sub))],
        out_specs=[row(D), pl.BlockSpec((tm * SUBLANES, LANES), lambda i: (i, 0)), row(LANES),
                   pl.BlockSpec((SUBLANES, tm), lambda i: (0, i)), pl.BlockSpec((1, LANES), lambda i: (0, 0))],
        out_shape=[jax.ShapeDtypeStruct((T, D), F32), jax.ShapeDtypeStruct((T * SUBLANES, LANES), F32),
                   jax.ShapeDtypeStruct((T, LANES), F32), jax.ShapeDtypeStruct((SUBLANES, T), F32),
                   jax.ShapeDtypeStruct((1, LANES), F32)],
        compiler_params=_params(("arbitrary",)),
        name="merge",
    )(f2d, z2d, sg, xa, xb, gt1, sc2, sh2, w_four, w_hyena, w_out, rw, rb, tri)


def _to_token_tiles(ref, val, row0=0):
    n = val.shape[0]
    for s in range(SUBLANES):
        ref[pl.ds(row0 + s, n, stride=SUBLANES), :] = val[:, s * LANES:(s + 1) * LANES]


def _from_token_tiles(ref, row0, n):
    return jnp.concatenate([ref[pl.ds(row0 + s, n, stride=SUBLANES), :] for s in range(SUBLANES)], axis=1)


def _gather_start(idx_ref, src_hbm, dst, row0, n_tok, sem):
    def issue(g, carry):
        for u in range(DMA_GROUP):
            r = g * DMA_GROUP + u
            src = pl.multiple_of(idx_ref[0, 0, r] * SUBLANES, SUBLANES)
            row = pl.multiple_of(row0 + r * SUBLANES, SUBLANES)
            copy = pltpu.make_async_copy(src_hbm.at[pl.ds(src, SUBLANES)], dst.at[pl.ds(row, SUBLANES)], sem)
            copy.start(priority=u % 2)
        return carry

    lax.fori_loop(0, n_tok // DMA_GROUP, issue, 0)


def _gather_wait(src_hbm, dst, row0, n_tok, sem):
    rows = n_tok * SUBLANES
    pltpu.make_async_copy(src_hbm.at[pl.ds(0, rows)], dst.at[pl.ds(row0, rows)], sem).wait()


def _dispatch_kernel(pend_ref, dest_ref, h_ref, x_hbm, zero, sem, zsem):
    i = pl.program_id(0)
    tm = dest_ref.shape[2] // TOP_K
    bm = MOE_BLOCK

    def token_rows(ref, tok):
        return ref.at[pl.ds(pl.multiple_of(tok * SUBLANES, SUBLANES), SUBLANES)]

    def drain(n_tok, s):
        rows = n_tok * SUBLANES
        pltpu.make_async_copy(zero.at[pl.ds(0, rows)], x_hbm.at[pl.ds(0, rows)], s).wait()

    @pl.when(i == 0)
    def _():
        zero[...] = jnp.zeros_like(zero)
        n_blocks = x_hbm.shape[0] // (bm * SUBLANES)
        used = pend_ref[N_EXPERTS - 1] // bm

        def zero_block(first_tok):
            row = pl.multiple_of(first_tok * SUBLANES, bm * SUBLANES)
            pltpu.make_async_copy(zero, x_hbm.at[pl.ds(row, bm * SUBLANES)], zsem).start()

        def per_expert(e, carry):
            zero_block(jnp.maximum(pend_ref[e] - bm, 0))
            return carry

        def per_unused(blk, carry):
            zero_block(blk * bm)
            return carry

        def drain_block(blk, carry):
            drain(bm, zsem)
            return carry

        lax.fori_loop(0, N_EXPERTS, per_expert, 0)
        lax.fori_loop(used, n_blocks, per_unused, 0)
        lax.fori_loop(0, N_EXPERTS + n_blocks - used, drain_block, 0)

    for k in range(TOP_K):
        def issue(g, carry, k=k):
            for u in range(DMA_GROUP):
                r = g * DMA_GROUP + u
                dst = token_rows(x_hbm, dest_ref[0, 0, k * tm + r])
                pltpu.make_async_copy(token_rows(h_ref, r), dst, sem).start(priority=u % 2)
            return carry

        lax.fori_loop(0, tm // DMA_GROUP, issue, 0)

    for k in range(TOP_K):
        pltpu.make_async_copy(h_ref, x_hbm.at[pl.ds(0, tm * SUBLANES)], sem).wait()


def _dispatch(pend, dest_tiles, h2, n_rows):
    n, _, tm2 = dest_tiles.shape
    tm = tm2 // TOP_K
    grid_spec = pltpu.PrefetchScalarGridSpec(
        num_scalar_prefetch=1,
        grid=(n,),
        in_specs=[pl.BlockSpec((1, 1, tm2), lambda i, pe: (i, 0, 0), memory_space=pltpu.SMEM),
                  pl.BlockSpec((tm * SUBLANES, LANES), lambda i, pe: (i, 0))],
        out_specs=pl.BlockSpec(memory_space=pl.ANY),
        scratch_shapes=[pltpu.VMEM((MOE_BLOCK * SUBLANES, LANES), F32), pltpu.SemaphoreType.DMA(()),
                        pltpu.SemaphoreType.DMA(())],
    )
    return pl.pallas_call(
        _dispatch_kernel,
        grid_spec=grid_spec,
        out_shape=jax.ShapeDtypeStruct((n_rows * SUBLANES, LANES), F32),
        compiler_params=_params(("arbitrary",)),
        name="dispatch",
    )(pend, dest_tiles, h2)


def _expert_kernel(blk_e_ref, nused_ref, x_ref, wg_ref, wu_ref, wd_ref, y_ref, wg16, wu16, wd16):
    i = pl.program_id(0)
    nused = nused_ref[0]
    bm = MOE_BLOCK

    @pl.when((i == 0) | (blk_e_ref[i] != blk_e_ref[jnp.maximum(i - 1, 0)]))
    def _():
        wg16[...] = wg_ref[0].astype(BF16)
        wu16[...] = wu_ref[0].astype(BF16)
        wd16[...] = wd_ref[0].astype(BF16)

    @pl.when(i < nused)
    def _():
        x = _from_token_tiles(x_ref, 0, bm).astype(BF16)
        g = _dot(x, wg16[...])
        u = _dot(x, wu16[...])
        a = (g * jax.nn.sigmoid(g) * u).astype(BF16)
        _to_token_tiles(y_ref, _dot(a, wd16[...]))

    @pl.when(i >= nused)
    def _():
        y_ref[...] = jnp.zeros_like(y_ref)


def _experts(blk_e, nused, xb, wg, wu, wd):
    nb = blk_e.shape[0]
    bm = MOE_BLOCK
    D, de = wg.shape[1:]
    assert D == SUBLANES * LANES and xb.shape == (nb * bm * SUBLANES, LANES)
    grid_spec = pltpu.PrefetchScalarGridSpec(
        num_scalar_prefetch=2,
        grid=(nb,),
        in_specs=[pl.BlockSpec((bm * SUBLANES, LANES), lambda i, be, nu: (jnp.minimum(i, nu[0] - 1), 0)),
                  pl.BlockSpec((1, D, de), lambda i, be, nu: (be[i], 0, 0)),
                  pl.BlockSpec((1, D, de), lambda i, be, nu: (be[i], 0, 0)),
                  pl.BlockSpec((1, de, D), lambda i, be, nu: (be[i], 0, 0))],
        out_specs=pl.BlockSpec((bm * SUBLANES, LANES), lambda i, be, nu: (i, 0)),
        scratch_shapes=[pltpu.VMEM((D, de), BF16), pltpu.VMEM((D, de), BF16), pltpu.VMEM((de, D), BF16)],
    )
    return pl.pallas_call(
        _expert_kernel,
        grid_spec=grid_spec,
        out_shape=jax.ShapeDtypeStruct((nb * bm * SUBLANES, LANES), F32),
        compiler_params=_params(("arbitrary",)),
        name="experts",
    )(blk_e, nused, xb, wg, wu, wd)


def _combine_kernel(n_steps, cur_ref, nxt_ref, y_hbm, x1_ref, route_ref, gt_ref, gf_ref, o_ref, buf, sem):
    i = pl.program_id(0)
    tm = x1_ref.shape[0]
    slot = i % 2
    rows = 2 * tm * SUBLANES

    @pl.when(i == 0)
    def _():
        _gather_start(cur_ref, y_hbm, buf, 0, 2 * tm, sem.at[0])

    @pl.when(i + 1 < n_steps)
    def _():
        _gather_start(nxt_ref, y_hbm, buf, (1 - slot) * rows, 2 * tm, sem.at[1 - slot])

    _gather_wait(y_hbm, buf, slot * rows, 2 * tm, sem.at[slot])
    route = route_ref[...]
    y1 = _from_token_tiles(buf, slot * rows, tm)
    y2 = _from_token_tiles(buf, slot * rows + tm * SUBLANES, tm)
    x = x1_ref[...] + gt_ref[0] * (route[:, 2:3] * y1 + route[:, 3:4] * y2)
    o_ref[...] = x * lax.rsqrt(jnp.mean(x * x, axis=-1, keepdims=True) + EPS) * gf_ref[...]


def _combine(tile0, n_tok, dest_tiles, yb, x1, route, gt2, g_final, L):
    T, D = x1.shape
    tm = TOKEN_TILE
    per_b = L // tm
    n = n_tok // tm
    last = tile0 + n - 1
    smem = lambda f: pl.BlockSpec((1, 1, 2 * tm), lambda i: (f(i), 0, 0), memory_space=pltpu.SMEM)
    return pl.pallas_call(
        functools.partial(_combine_kernel, n),
        grid=(n,),
        in_specs=[smem(lambda i: tile0 + i), smem(lambda i: jnp.minimum(tile0 + i + 1, last)),
                  pl.BlockSpec(memory_space=pl.ANY),
                  pl.BlockSpec((tm, D), lambda i: (tile0 + i, 0)),
                  pl.BlockSpec((tm, LANES), lambda i: (tile0 + i, 0)),
                  pl.BlockSpec((1, 1, D), lambda i: ((tile0 + i) // per_b, 0, 0)),
                  pl.BlockSpec((1, D), lambda i: (0, 0))],
        out_specs=pl.BlockSpec((tm, D), lambda i: (i, 0)),
        out_shape=jax.ShapeDtypeStruct((n_tok, D), F32),
        scratch_shapes=[pltpu.VMEM((2 * 2 * tm * SUBLANES, LANES), F32), pltpu.SemaphoreType.DMA((2,))],
        compiler_params=_params(("arbitrary",)),
        name="combine",
    )(dest_tiles, dest_tiles, yb, x1, route, gt2, g_final.reshape(1, D))


def _encoder(xp, xs, c, p, g_final):
    n_prompt, L, D = xp.shape
    B = n_prompt + xs.shape[0]
    T = B * L
    xa, xb = xp.reshape(-1, D), xs.reshape(-1, D)
    d_f = N_FOURIER_GROUPS * FOURIER_GROUP_DIM
    d_h = (HYENA_ORDER + 1) * D_HYENA
    d_g = 2 * D

    mod = _ada(c, p["w_ada"], p["b_ada"])
    sh1, sc1, gt1, sh2, sc2, gt2 = [mod[:, k * D:(k + 1) * D].reshape(B, 1, D) for k in range(6)]

    uf, uh, sg = _inproj(xa, xb, sc1, sh1, p["w_in"].astype(BF16), p["b_in"][None, :], L, d_f, d_h, d_g)
    f = _fourier(uf.reshape(B, L, d_f))

    ktime, ksum = _filt_time(L, p["filt_w1"], p["filt_b1"], p["filt_w2"], p["filt_b2"],
                             p["filt_w3"], p["filt_b3"], p["filt_freq"], p["filt_wout"])
    kf = _filt_fft(L, ktime, ksum)
    uh3 = uh.reshape(B, L, d_h)
    conv_b = p["conv_b"][None, :]
    ncol = D_HYENA // FFT_COLS
    skip = p["hyena_skip"]
    z = _hyena_order(0, uh3, 0, uh3, ncol, p["conv_w"], conv_b, skip[0:1], kf, L)
    z = _hyena_order(1, z, 0, uh3, 2 * ncol, p["conv_w"], conv_b, skip[1:2], kf, L)

    rw = jnp.zeros((D, LANES), F32).at[:, :N_GROUPS].set(p["router_w1"])
    rw = rw.at[:, N_GROUPS:N_GROUPS + N_EXPERTS].set(p["router_w2"])
    rb = jnp.zeros((1, LANES), F32).at[0, :N_GROUPS].set(p["router_b1"])
    rb = rb.at[0, N_GROUPS:N_GROUPS + N_EXPERTS].set(p["router_b2"])
    x1, h2, route, route_t, counts = _merge(f.reshape(T, d_f), z.reshape(T, D_HYENA), sg, xa, xb,
                                            gt1, sc2, sh2, p["w_four"].astype(BF16), p["w_hyena"].astype(BF16),
                                            p["w_out"].astype(BF16), rw, rb, L)

    bm = MOE_BLOCK
    tm = TOKEN_TILE
    nb = (T * TOP_K) // bm + N_EXPERTS
    e = route_t[0:2].astype(jnp.int32)
    rank = route_t[4:6].astype(jnp.int32)
    cnt = counts[0, :N_EXPERTS].astype(jnp.int32)
    pcnt = (cnt + bm - 1) // bm * bm
    pend = jnp.cumsum(pcnt)
    experts = jnp.arange(N_EXPERTS, dtype=jnp.int32)
    dest = rank + jnp.sum(jnp.where(e[..., None] == experts, pend - pcnt, 0), axis=-1)
    blk_row0 = jnp.arange(nb, dtype=jnp.int32)[:, None] * bm
    blk_e = jnp.minimum(jnp.sum((pend[None, :] <= blk_row0).astype(jnp.int32), axis=1), N_EXPERTS - 1)
    nused = (pend[-1] // bm).astype(jnp.int32).reshape(1)
    dest_tiles = dest.reshape(TOP_K, T // tm, tm).transpose(1, 0, 2).reshape(T // tm, 1, TOP_K * tm)
    xb = _dispatch(pend.astype(jnp.int32), dest_tiles, h2, nb * bm)
    yb = _experts(blk_e, nused, xb, p["exp_w_gate"], p["exp_w_up"], p["exp_w_down"])

    t_prompt = n_prompt * L
    outs = []
    for tile0, n_tok in ((0, t_prompt), (t_prompt // tm, T - t_prompt)):
        outs.append(_combine(tile0, n_tok, dest_tiles, yb, x1, route, gt2, g_final, L))
    return outs[0].reshape(xp.shape), outs[1].reshape(xs.shape)


def kernel(x_prompt, x_sample, c_prompt, c_sample, w_ada, b_ada, w_in, b_in, conv_w, conv_b, filt_w1, filt_b1, filt_w2, filt_b2, filt_w3, filt_b3, filt_freq, filt_wout, hyena_skip, w_four, w_hyena, w_out, router_w1, router_b1, router_w2, router_b2, exp_w_gate, exp_w_up, exp_w_down, g_final):
    assert w_ada.shape[0] == 1, "single-layer block"
    assert x_prompt.shape[1:] == x_sample.shape[1:], "both request groups share sequence length and width"
    p = dict(w_ada=w_ada[0], b_ada=b_ada[0], w_in=w_in[0], b_in=b_in[0], conv_w=conv_w[0], conv_b=conv_b[0],
             filt_w1=filt_w1[0], filt_b1=filt_b1[0], filt_w2=filt_w2[0], filt_b2=filt_b2[0],
             filt_w3=filt_w3[0], filt_b3=filt_b3[0], filt_freq=filt_freq[0], filt_wout=filt_wout[0],
             hyena_skip=hyena_skip[0], w_four=w_four[0], w_hyena=w_hyena[0], w_out=w_out[0],
             router_w1=router_w1[0], router_b1=router_b1[0], router_w2=router_w2[0], router_b2=router_b2[0],
             exp_w_gate=exp_w_gate[0], exp_w_up=exp_w_up[0], exp_w_down=exp_w_down[0])
    c = jnp.concatenate([c_prompt, c_sample], axis=0)
    return _encoder(x_prompt, x_sample, c, p, g_final)
```

```python
import functools
import math

import numpy as np
import jax
import jax.numpy as jnp
from jax import lax
from jax.experimental import pallas as pl
from jax.experimental.pallas import tpu as pltpu

F32 = jnp.float32
BF16 = jnp.bfloat16
HI = lax.Precision.HIGHEST

EPS = 1e-6
FFT_MINOR = 64
N_FOURIER_GROUPS = 4
FOURIER_GROUP_DIM = 128
D_HYENA = 512
HYENA_ORDER = 2
POS_BANDS = 16
N_GROUPS = 4
EXPERTS_PER_GROUP = 8
N_EXPERTS = 32
TOP_K = 2
SHORT_DECAY_PCT = 0.3
LONG_DECAY_PCT = 1.5
DECAY_TARGET = 1e-2
LANES = 128
SUBLANES = 8
VMEM_LIMIT = 56 * 1024 * 1024

TOKEN_TILE = 512
MERGE_SUBTILE = 128
MOE_BLOCK = 512
FFT_COLS = LANES
ROW_CHUNK = 256
FFT_UNROLL = 64
DMA_GROUP = 8


def _dot(a, b):
    return jnp.dot(a, b, preferred_element_type=F32)


def _dot_hi(a, b):
    return jnp.dot(a, b, preferred_element_type=F32, precision=HI)


def _split_bf16(x):
    hi = x.astype(BF16)
    return hi, (x - hi.astype(F32)).astype(BF16)


def _dot3(a, b):
    a_hi, a_lo = _split_bf16(a)
    b_hi, b_lo = _split_bf16(b)
    return _dot(a_hi, b_hi) + (_dot(a_lo, b_hi) + _dot(a_hi, b_lo))


def _params(sem=None):
    return pltpu.CompilerParams(dimension_semantics=sem, vmem_limit_bytes=VMEM_LIMIT)


def _const_spec(shape):
    nd = len(shape)
    return pl.BlockSpec(shape, lambda *_: (0,) * nd, pipeline_mode=pl.Buffered(1))


def _pitch(rows):
    p = -(-rows // SUBLANES)
    return SUBLANES * (p if p % 2 else p + 1)


@functools.lru_cache(maxsize=None)
def _stage2_tables():
    i = np.arange(FFT_MINOR)
    ph = 2.0 * np.pi * np.outer(i, i) / FFT_MINOR
    c, s = np.cos(ph), np.sin(ph)
    fwd = np.block([[c, s], [-s, c]])
    inv = np.block([[c, -s], [s, c]])
    return fwd, inv, c, s


@functools.lru_cache(maxsize=None)
def _hyena_tables(L):
    n_fft = 2 * L
    m = FFT_MINOR
    na = n_fft // m
    nh = na // 2
    b = np.arange(m)[:, None, None]
    ka = np.arange(nh + 1)[None, :, None]

    def forward(a_count):
        a = np.arange(a_count)[None, None, :]
        th = 2.0 * np.pi * ((ka * (m * a + b)) % n_fft) / n_fft
        c, s = np.cos(th), -np.sin(th)
        pad = np.zeros((m, SUBLANES - 2, a_count))
        return np.concatenate([c[:, :nh], s[:, :nh], c[:, nh:], s[:, nh:], pad], axis=1)

    fwd_half = forward(nh)
    fwd_full = forward(na)
    weight = np.where(np.arange(nh) == 0, 1.0, 2.0)[None, :, None]
    inv_half = np.transpose(fwd_half[:, :2 * nh] * np.concatenate([weight, weight], axis=1), (0, 2, 1)) / n_fft
    sign = np.broadcast_to(((-1.0) ** np.arange(nh))[:, None] / n_fft, (nh, LANES))
    ph = np.pi * np.arange(m) / m
    cos_b = np.broadcast_to(np.cos(ph)[:, None], (m, LANES))
    sin_b = np.broadcast_to(np.sin(ph)[:, None], (m, LANES))
    return fwd_half, inv_half, fwd_full, sign, cos_b, sin_b


@functools.lru_cache(maxsize=None)
def _fourier_tables(L):
    na = L // FFT_MINOR
    b = np.arange(FFT_MINOR)[:, None, None]
    ka = np.arange(na)[None, :, None]
    a = np.arange(na)[None, None, :]
    th = 2.0 * np.pi * ((ka * (FFT_MINOR * a + b)) % L) / L
    c, s = np.cos(th), np.sin(th)
    st1 = np.concatenate([np.concatenate([c, s], axis=2),
                          np.concatenate([-s, c], axis=2)], axis=1)
    _, _, c2, s2 = _stage2_tables()
    st2 = np.concatenate([c2, s2], axis=1) / math.sqrt(L)
    return st1, st2


def _mxu_table(table):
    return jnp.asarray(table, F32).astype(BF16)


@functools.lru_cache(maxsize=None)
def _channel_dft():
    i = np.arange(FOURIER_GROUP_DIM)
    ph = 2.0 * np.pi * np.outer(i, i) / FOURIER_GROUP_DIM
    return np.concatenate([np.cos(ph), -np.sin(ph)], axis=1) / math.sqrt(FOURIER_GROUP_DIM)


def _ada_kernel(c_ref, w_ref, b_ref, o_ref):
    c = c_ref[...]
    o_ref[...] = _dot_hi(c * jax.nn.sigmoid(c), w_ref[...]) + b_ref[...]


def _ada(c, w_ada, b_ada):
    nb, d = c.shape
    n = w_ada.shape[1]
    tn = 1536
    return pl.pallas_call(
        _ada_kernel,
        grid=(n // tn,),
        in_specs=[pl.BlockSpec((nb, d), lambda j: (0, 0)),
                  pl.BlockSpec((d, tn), lambda j: (0, j)),
                  pl.BlockSpec((1, tn), lambda j: (0, j))],
        out_specs=pl.BlockSpec((nb, tn), lambda j: (0, j)),
        out_shape=jax.ShapeDtypeStruct((nb, n), F32),
        compiler_params=_params(("arbitrary",)),
        name="ada",
    )(c, w_ada, b_ada.reshape(1, n))


def _for_row_chunks(n_rows, body, unroll=1):
    def step(j, carry):
        body(j, pl.multiple_of(j * ROW_CHUNK, ROW_CHUNK))
        return carry

    lax.fori_loop(0, n_rows // ROW_CHUNK, step, 0, unroll=unroll)


def _store_padded(dst, j, val, run=FFT_MINOR):
    pz = _pitch(run)
    per = ROW_CHUNK // run
    for i in range(per):
        dst[pl.ds(pl.multiple_of((j * per + i) * pz, SUBLANES), run), :] = val[i * run:(i + 1) * run]


def _load_padded(src, j, run=FFT_MINOR):
    pz = _pitch(run)
    per = ROW_CHUNK // run
    return jnp.concatenate([src[pl.ds(pl.multiple_of((j * per + i) * pz, SUBLANES), run), :] for i in range(per)],
                           axis=0)


def _short_conv_chunk(src_ref, w_ref, b_ref, L, j, r0):
    R = ROW_CHUNK
    ct = src_ref.shape[-1]
    row = lax.broadcasted_iota(jnp.int32, (R, ct), 0)
    cur = src_ref[0, pl.ds(r0, R), :]
    before = src_ref[0, pl.ds(jnp.maximum(r0 - 1, 0), 1), :] * jnp.where(j > 0, 1.0, 0.0)
    after = src_ref[0, pl.ds(jnp.minimum(r0 + R, L - 1), 1), :] * jnp.where(j < L // R - 1, 1.0, 0.0)
    up = jnp.where(row == 0, before, pltpu.roll(cur, 1, 0))
    dn = jnp.where(row == R - 1, after, pltpu.roll(cur, R - 1, 0))
    return w_ref[0:1, :] * up + w_ref[1:2, :] * cur + w_ref[2:3, :] * dn + b_ref[...]


def _filt_time_kernel(L, rows, bands_ref, delt_ref, w1_ref, b1_ref, w2_ref, b2_ref, w3_ref, b3_ref,
                      fr_ref, wo_ref, k_ref, sum_ref):
    i = pl.program_id(0)
    n = i * rows + lax.broadcasted_iota(jnp.int32, (rows, 1), 0)
    fwd = n < L
    pos = jnp.where(fwd, n, 2 * L - n).astype(F32)
    t = pos * (1.0 / (L - 1))
    ang = (2.0 * math.pi / L) * pos * bands_ref[...]
    w1 = w1_ref[...]
    pre = (t * w1[0:1, :] + _dot3(jnp.cos(ang), w1[1:1 + POS_BANDS, :])
           + _dot3(-jnp.sin(ang), w1[1 + POS_BANDS:, :]) + b1_ref[...])
    fr = fr_ref[...]
    h = jnp.sin(fr[0:1, :] * pre)
    h = jnp.sin(fr[1:2, :] * (_dot3(h, w2_ref[...]) + b2_ref[...]))
    h = jnp.sin(fr[2:3, :] * (_dot3(h, w3_ref[...]) + b3_ref[...]))
    window = jnp.exp(-t * delt_ref[...])
    live = n != L
    tile_fwd = i < (L // rows)
    parts = []
    for o in range(HYENA_ORDER):
        base = o * 2 * D_HYENA
        w = jnp.where(tile_fwd, wo_ref[:, base:base + D_HYENA], wo_ref[:, base + D_HYENA:base + 2 * D_HYENA])
        parts.append(jnp.where(live, _dot3(h, w) * window, 0.0))
    k = jnp.concatenate(parts, axis=1)
    k_ref[...] = k

    @pl.when(i == 0)
    def _():
        sum_ref[...] = jnp.zeros_like(sum_ref)

    sum_ref[...] += jnp.sum(jnp.abs(k), axis=0, keepdims=True)


def _filt_time(L, w1, b1, w2, b2, w3, b3, freq, wout):
    rows = 512
    assert L % rows == 0, "each row tile must hold one filter direction only"
    n_fft = 2 * L
    cols = HYENA_ORDER * D_HYENA
    bands = jnp.linspace(1e-4, POS_BANDS - 1, POS_BANDS, dtype=F32).reshape(1, POS_BANDS)
    max_decay = math.log(DECAY_TARGET) / SHORT_DECAY_PCT
    min_decay = math.log(DECAY_TARGET) / LONG_DECAY_PCT
    deltas = jnp.abs(jnp.linspace(min_decay, max_decay, D_HYENA, dtype=F32)).reshape(1, D_HYENA)
    args = (bands, deltas, w1, b1.reshape(1, -1), w2, b2.reshape(1, -1), w3, b3.reshape(1, -1), freq, wout)
    return pl.pallas_call(
        functools.partial(_filt_time_kernel, L, rows),
        grid=(n_fft // rows,),
        in_specs=[pl.BlockSpec(a.shape, lambda i: (0, 0)) for a in args],
        out_specs=[pl.BlockSpec((rows, cols), lambda i: (i, 0)),
                   pl.BlockSpec((1, cols), lambda i: (0, 0))],
        out_shape=[jax.ShapeDtypeStruct((n_fft, cols), F32),
                   jax.ShapeDtypeStruct((1, cols), F32)],
        compiler_params=_params(("arbitrary",)),
        name="filt_time",
    )(*args)


def _filt_fft_kernel(na, k_ref, sum_ref, st1_ref, st2_ref, o_ref, kp, s1):
    m = FFT_MINOR
    nh = na // 2
    rows1 = 2 * nh + SUBLANES
    pz, p1 = _pitch(m), _pitch(rows1)
    _for_row_chunks(na * m, lambda j, r0: _store_padded(kp, j, k_ref[pl.ds(r0, ROW_CHUNK), :]))

    def stage1(b, carry):
        slab = kp[pl.ds(b, na, stride=pz), :]
        s1[pl.ds(pl.multiple_of(b * p1, SUBLANES), rows1), :] = _dot3(st1_ref[b], slab)
        return carry

    lax.fori_loop(0, m, stage1, 0, unroll=4)
    inv_norm = 1.0 / sum_ref[...]

    def spectrum_row(re_row, im_row):
        a = jnp.concatenate([s1[pl.ds(re_row, m, stride=p1), :], s1[pl.ds(im_row, m, stride=p1), :]], axis=0)
        return _dot3(st2_ref[...], a) * inv_norm

    def stage2(ka, carry):
        o_ref[ka] = spectrum_row(ka, nh + ka)
        return carry

    lax.fori_loop(0, nh, stage2, 0, unroll=4)
    o_ref[nh] = spectrum_row(2 * nh, 2 * nh + 1)


def _filt_fft(L, ktime, ksum):
    n_fft, cols = ktime.shape
    m = FFT_MINOR
    na = n_fft // m
    nh = na // 2
    rows1 = 2 * nh + SUBLANES
    ct = FFT_COLS
    fwd_full = _hyena_tables(L)[2]
    st2f = _stage2_tables()[0]
    return pl.pallas_call(
        functools.partial(_filt_fft_kernel, na),
        grid=(cols // ct,),
        in_specs=[pl.BlockSpec((n_fft, ct), lambda j: (0, j)),
                  pl.BlockSpec((1, ct), lambda j: (0, j)),
                  _const_spec((m, rows1, na)),
                  _const_spec((2 * m, 2 * m))],
        out_specs=pl.BlockSpec((nh + 1, 2 * m, ct), lambda j: (0, 0, j)),
        out_shape=jax.ShapeDtypeStruct((nh + 1, 2 * m, cols), F32),
        scratch_shapes=[pltpu.VMEM((na * _pitch(m), ct), F32),
                        pltpu.VMEM((m * _pitch(rows1), ct), F32)],
        compiler_params=_params(("arbitrary",)),
        name="filt_fft",
    )(ktime, ksum, jnp.asarray(fwd_full, F32), jnp.asarray(st2f, F32))


def _rms_mod(x, scale, shift):
    y = x * lax.rsqrt(jnp.mean(x * x, axis=-1, keepdims=True) + EPS)
    return y * (1.0 + scale) + shift


def _group_tile(n_first, a_ref, b_ref):
    return jnp.where(pl.program_id(0) < n_first, a_ref[...], b_ref[...])


def _group_specs(n_first, tm, D):
    return [pl.BlockSpec((tm, D), lambda i: (jnp.minimum(i, n_first - 1), 0)),
            pl.BlockSpec((tm, D), lambda i: (jnp.maximum(i - n_first, 0), 0))]


def _inproj_kernel(d_f, d_h, n_first, xa_ref, xb_ref, sc_ref, sh_ref, w_ref, b_ref, uf_ref, uh_ref, sg_ref):
    h = _rms_mod(_group_tile(n_first, xa_ref, xb_ref), sc_ref[0], sh_ref[0]).astype(BF16)

    def proj(c0, width):
        return _dot(h, w_ref[:, c0:c0 + width]) + b_ref[:, c0:c0 + width]

    uf_ref[...] = proj(0, d_f)
    chunk = 512
    for c in range(0, d_h, chunk):
        uh_ref[:, c:c + chunk] = proj(d_f + c, chunk)
    d_g = sg_ref.shape[1]
    for c in range(0, d_g, chunk):
        sg_ref[:, c:c + chunk] = jax.nn.sigmoid(proj(d_f + d_h + c, chunk)).astype(BF16)


def _inproj(xa, xb, sc, sh, w, b, L, d_f, d_h, d_g):
    D = xa.shape[1]
    T = xa.shape[0] + xb.shape[0]
    tm = TOKEN_TILE
    per_b = L // tm
    n_first = xa.shape[0] // tm
    n = w.shape[1]
    mod_spec = pl.BlockSpec((1, 1, D), lambda i: (i // per_b, 0, 0))
    return pl.pallas_call(
        functools.partial(_inproj_kernel, d_f, d_h, n_first),
        grid=(T // tm,),
        in_specs=_group_specs(n_first, tm, D) + [mod_spec, mod_spec, _const_spec((D, n)), _const_spec((1, n))],
        out_specs=[pl.BlockSpec((tm, d_f), lambda i: (i, 0)),
                   pl.BlockSpec((tm, d_h), lambda i: (i, 0)),
                   pl.BlockSpec((tm, d_g), lambda i: (i, 0))],
        out_shape=[jax.ShapeDtypeStruct((T, d_f), F32),
                   jax.ShapeDtypeStruct((T, d_h), F32), jax.ShapeDtypeStruct((T, d_g), BF16)],
        compiler_params=_params(("parallel",)),
        name="inproj",
    )(xa, xb, sc, sh, w, b)


def _fourier_kernel(na, u_ref, cdft_ref, st1_ref, st2_ref, o_ref, zpr, zpi, s1, op):
    m = FFT_MINOR
    pz, p1 = _pitch(m), _pitch(2 * na)

    def fill(j, r0):
        z = _dot(u_ref[0, pl.ds(r0, ROW_CHUNK), :].astype(BF16), cdft_ref[...])
        _store_padded(zpr, j, z[:, :FOURIER_GROUP_DIM])
        _store_padded(zpi, j, z[:, FOURIER_GROUP_DIM:])

    _for_row_chunks(na * m, fill, unroll=8)

    def stage1(b, carry):
        z = jnp.concatenate([zpr[pl.ds(b, na, stride=pz), :], zpi[pl.ds(b, na, stride=pz), :]], axis=0)
        s1[pl.ds(pl.multiple_of(b * p1, SUBLANES), 2 * na), :] = _dot(st1_ref[b], z.astype(BF16))
        return carry

    lax.fori_loop(0, m, stage1, 0, unroll=FFT_UNROLL)

    po = _pitch(na)

    def stage2(ka, carry):
        a = jnp.concatenate([s1[pl.ds(ka, m, stride=p1), :], s1[pl.ds(na + ka, m, stride=p1), :]], axis=0)
        op[pl.ds(ka, m, stride=po), :] = _dot(st2_ref[...], a.astype(BF16))
        return carry

    lax.fori_loop(0, na, stage2, 0, unroll=FFT_UNROLL)

    def unpad(j, r0):
        o_ref[0, pl.ds(r0, ROW_CHUNK), :] = _load_padded(op, j, na).astype(o_ref.dtype)

    _for_row_chunks(na * m, unpad)


def _fourier(u):
    B, L, C = u.shape
    m = FFT_MINOR
    na = L // m
    ct = FOURIER_GROUP_DIM
    assert ct == FFT_COLS
    st1, st2 = _fourier_tables(L)
    spec = pl.BlockSpec((1, L, ct), lambda j, b: (b, 0, j))
    pad = pltpu.VMEM((na * _pitch(m), ct), F32)
    return pl.pallas_call(
        functools.partial(_fourier_kernel, na),
        grid=(C // ct, B),
        in_specs=[spec, _const_spec((ct, 2 * ct)), _const_spec((m, 2 * na, 2 * na)), _const_spec((m, 2 * m))],
        out_specs=spec,
        out_shape=jax.ShapeDtypeStruct((B, L, C), BF16),
        scratch_shapes=[pad, pad, pltpu.VMEM((m * _pitch(2 * na), ct), F32),
                        pltpu.VMEM((m * _pitch(na), ct), F32)],
        compiler_params=_params(("parallel", "parallel")),
        name="fourier",
    )(u, _mxu_table(_channel_dft()), _mxu_table(st1), _mxu_table(st2))


def _hyena_kernel(conv_z, L, zin_ref, gin_ref, cwz_ref, cbz_ref, cwg_ref, cbg_ref, skip_ref, kf_ref,
                  fwd1_ref, inv1_ref, st2f_ref, st2i_ref, sign_ref, cosb_ref, sinb_ref, out_ref, zp, yp, s1, g2, hh):
    m = FFT_MINOR
    nh = L // m
    rows1 = 2 * nh + SUBLANES
    pz, p1, p2 = _pitch(m), _pitch(rows1), _pitch(2 * m)

    def fill(j, r0):
        if conv_z:
            _store_padded(zp, j, _short_conv_chunk(zin_ref, cwz_ref, cbz_ref, L, j, r0))
        else:
            _store_padded(zp, j, zin_ref[0, pl.ds(r0, ROW_CHUNK), :])

    _for_row_chunks(L, fill)

    def stage1(b, carry):
        slab = zp[pl.ds(b, nh, stride=pz), :].astype(BF16)
        s1[pl.ds(pl.multiple_of(b * p1, SUBLANES), rows1), :] = _dot(fwd1_ref[b], slab)
        return carry

    lax.fori_loop(0, m, stage1, 0, unroll=FFT_UNROLL)

    def spectrum_row(ka, re_row, im_row):
        a = jnp.concatenate([s1[pl.ds(re_row, m, stride=p1), :], s1[pl.ds(im_row, m, stride=p1), :]], axis=0)
        x = _dot(st2f_ref[...], a.astype(BF16))
        xr, xi = x[:m], x[m:]
        k = kf_ref[ka]
        kr, ki = k[:m], k[m:]
        y = jnp.concatenate([xr * kr - xi * ki, xr * ki + xi * kr], axis=0).astype(BF16)
        return _dot(st2i_ref[...], y)

    def stage2(ka, carry):
        g2[pl.ds(pl.multiple_of(ka * p2, SUBLANES), 2 * m), :] = spectrum_row(ka, ka, nh + ka)
        return carry

    lax.fori_loop(0, nh, stage2, 0, unroll=FFT_UNROLL)
    g_mid = spectrum_row(nh, 2 * nh, 2 * nh + 1)
    hh[...] = cosb_ref[...] * g_mid[:m] - sinb_ref[...] * g_mid[m:]
    sign = sign_ref[...]

    def stage3(b, carry):
        g = jnp.concatenate([g2[pl.ds(b, nh, stride=p2), :], g2[pl.ds(m + b, nh, stride=p2), :]], axis=0)
        yp[pl.ds(b, nh, stride=pz), :] = _dot(inv1_ref[b], g.astype(BF16)) + sign * hh[pl.ds(b, 1), :]
        return carry

    lax.fori_loop(0, m, stage3, 0, unroll=FFT_UNROLL)
    skip = skip_ref[...]

    def finish(j, r0):
        gate = _short_conv_chunk(gin_ref, cwg_ref, cbg_ref, L, j, r0)
        out = gate * (_load_padded(yp, j) + skip * _load_padded(zp, j))
        out_ref[0, pl.ds(r0, ROW_CHUNK), :] = out.astype(out_ref.dtype)

    _for_row_chunks(L, finish)


def _hyena_order(order, zin, z_col0, uh, g_col0, conv_w, conv_b, skip, kf, L):
    B = uh.shape[0]
    m = FFT_MINOR
    ct = FFT_COLS
    ncol = D_HYENA // ct
    nh = L // m
    rows1 = 2 * nh + SUBLANES
    conv_z = order == 0
    fwd_half, inv_half, _, sign, cos_b, sin_b = _hyena_tables(L)
    st2f, st2i, _, _ = _stage2_tables()
    z_cols = (lambda j, b: (b, 0, z_col0 + j))
    g_cols = (lambda j, b: (b, 0, g_col0 + j))
    zw_col0 = z_col0 if conv_z else 0
    pad = pltpu.VMEM((nh * _pitch(m), ct), F32)
    return pl.pallas_call(
        functools.partial(_hyena_kernel, conv_z, L),
        grid=(ncol, B),
        in_specs=[pl.BlockSpec((1, L, ct), z_cols),
                  pl.BlockSpec((1, L, ct), g_cols),
                  pl.BlockSpec((3, ct), lambda j, b: (0, zw_col0 + j)),
                  pl.BlockSpec((1, ct), lambda j, b: (0, zw_col0 + j)),
                  pl.BlockSpec((3, ct), lambda j, b: (0, g_col0 + j)),
                  pl.BlockSpec((1, ct), lambda j, b: (0, g_col0 + j)),
                  pl.BlockSpec((1, ct), lambda j, b: (0, j)),
                  pl.BlockSpec((nh + 1, 2 * m, ct), lambda j, b: (0, 0, order * ncol + j),
                               pipeline_mode=pl.Buffered(1)),
                  _const_spec((m, rows1, nh)),
                  _const_spec((m, nh, 2 * nh)),
                  _const_spec((2 * m, 2 * m)),
                  _const_spec((2 * m, 2 * m)),
                  _const_spec((nh, LANES)), _const_spec((m, LANES)), _const_spec((m, LANES))],
        out_specs=pl.BlockSpec((1, L, ct), lambda j, b: (b, 0, j)),
        out_shape=jax.ShapeDtypeStruct((B, L, D_HYENA), F32 if conv_z else BF16),
        scratch_shapes=[pad, pad,
                        pltpu.VMEM((m * _pitch(rows1), ct), F32),
                        pltpu.VMEM((nh * _pitch(2 * m), ct), F32),
                        pltpu.VMEM((m, ct), F32)],
        compiler_params=_params(("parallel", "arbitrary")),
        name=f"hyena{order}",
    )(zin, uh, conv_w, conv_b, conv_w, conv_b, skip, kf,
      _mxu_table(fwd_half), _mxu_table(inv_half), _mxu_table(st2f), _mxu_table(st2i),
      jnp.asarray(sign, F32), jnp.asarray(cos_b, F32), jnp.asarray(sin_b, F32))


def _merge_kernel(n_first, f_ref, z_ref, sg_ref, xa_ref, xb_ref, gt_ref, sc_ref, sh_ref, wf_ref, wh_ref, wo_ref,
                  rw_ref, rb_ref, tri_ref, x1_ref, h2_ref, route_ref, rt_ref, cnt_ref):
    i = pl.program_id(0)
    tm, D = xa_ref.shape
    sub = MERGE_SUBTILE
    lane = lax.broadcasted_iota(jnp.int32, (sub, LANES), 1).astype(F32)
    neg = -1e30

    def first_max(v):
        mx = jnp.max(v, axis=1, keepdims=True)
        return mx, jnp.min(jnp.where(v == mx, lane, float(LANES)), axis=1, keepdims=True)

    @pl.when(i == 0)
    def _():
        cnt_ref[...] = jnp.zeros_like(cnt_ref)

    def sub_tile(r0, count):
        rows = pl.ds(r0, sub)
        ya = _dot(f_ref[rows, :].astype(BF16), wf_ref[...])
        yb = _dot(z_ref[rows, :].astype(BF16), wh_ref[...])
        merged = sg_ref[rows, :D].astype(F32) * ya + sg_ref[rows, D:].astype(F32) * yb
        x = jnp.where(i < n_first, xa_ref[rows, :], xb_ref[rows, :])
        x1 = x + gt_ref[0] * _dot(merged.astype(BF16), wo_ref[...])
        x1_ref[rows, :] = x1
        h2 = _rms_mod(x1, sc_ref[0], sh_ref[0])
        _to_token_tiles(h2_ref, h2, r0 * SUBLANES)

        h_hi, h_lo = _split_bf16(h2)
        logits = (_dot(h_hi, rw_hi) + (_dot(h_lo, rw_hi) + _dot(h_hi, rw_lo))) + rb_ref[...]
        gl = jnp.where(lane < N_GROUPS, logits, neg)
        gmax, g = first_max(gl)
        p_g = 1.0 / jnp.sum(jnp.exp(gl - gmax), axis=1, keepdims=True)
        lo = N_GROUPS + EXPERTS_PER_GROUP * g
        el = jnp.where((lane >= lo) & (lane < lo + EXPERTS_PER_GROUP), logits, neg)
        m1, i1 = first_max(el)
        m2, i2 = first_max(jnp.where(lane == i1, neg, el))
        r = jnp.exp(m2 - m1)
        wt1 = p_g / (1.0 + r)
        wt2 = p_g * r / (1.0 + r)
        e1 = i1 - N_GROUPS
        e2 = i2 - N_GROUPS

        onehot = ((lane == e1) | (lane == e2)).astype(BF16)
        before = _dot(tri_ref[...], onehot) + count
        r1 = jnp.sum(jnp.where(lane == e1, before, 0.0), axis=1, keepdims=True)
        r2 = jnp.sum(jnp.where(lane == e2, before, 0.0), axis=1, keepdims=True)

        packed = jnp.zeros((sub, LANES), F32)
        for slot, v in enumerate((e1, e2, wt1, wt2, r1, r2)):
            packed = jnp.where(lane == slot, v, packed)
        route_ref[rows, :] = packed
        rt_ref[:, rows] = packed.T[0:SUBLANES, :]
        return count + jnp.sum(onehot.astype(F32), axis=0, keepdims=True)

    rw_hi, rw_lo = _split_bf16(rw_ref[...])
    count = cnt_ref[...]
    for h in range(tm // sub):
        count = sub_tile(h * sub, count)
    cnt_ref[...] = count


def _merge(f2d, z2d, sg, xa, xb, gt1, sc2, sh2, w_four, w_hyena, w_out, rw, rb, L):
    D = xa.shape[1]
    T = xa.shape[0] + xb.shape[0]
    assert D == SUBLANES * LANES
    tm = TOKEN_TILE
    per_b = L // tm
    n_first = xa.shape[0] // tm
    d_f = f2d.shape[1]
    d_h = z2d.shape[1]
    sub = MERGE_SUBTILE
    tri = jnp.asarray(np.tril(np.ones((sub, sub)), -1), BF16)
    mod_spec = pl.BlockSpec((1, 1, D), lambda i: (i // per_b, 0, 0))
    row = lambda w: pl.BlockSpec((tm, w), lambda i: (i, 0))
    return pl.pallas_call(
        functools.partial(_merge_kernel, n_first),
        grid=(T // tm,),
        in_specs=[row(d_f), row(d_h), row(2 * D)] + _group_specs(n_first, tm, D) + [
            mod_spec, mod_spec, mod_spec,
            _const_spec((d_f, D)), _const_spec((d_h, D)), _const_spec((D, D)),
            _const_spec((D, LANES)), _const_spec((1, LANES)), _const_spec((sub, sub))],
        out_specs=[row(D), pl.BlockSpec((tm * SUBLANES, LANES), lambda i: (i, 0)), row(LANES),
                   pl.BlockSpec((SUBLANES, tm), lambda i: (0, i)), pl.BlockSpec((1, LANES), lambda i: (0, 0))],
        out_shape=[jax.ShapeDtypeStruct((T, D), F32), jax.ShapeDtypeStruct((T * SUBLANES, LANES), F32),
                   jax.ShapeDtypeStruct((T, LANES), F32), jax.ShapeDtypeStruct((SUBLANES, T), F32),
                   jax.ShapeDtypeStruct((1, LANES), F32)],
        compiler_params=_params(("arbitrary",)),
        name="merge",
    )(f2d, z2d, sg, xa, xb, gt1, sc2, sh2, w_four, w_hyena, w_out, rw, rb, tri)


def _to_token_tiles(ref, val, row0=0):
    n = val.shape[0]
    for s in range(SUBLANES):
        ref[pl.ds(row0 + s, n, stride=SUBLANES), :] = val[:, s * LANES:(s + 1) * LANES]


def _from_token_tiles(ref, row0, n):
    return jnp.concatenate([ref[pl.ds(row0 + s, n, stride=SUBLANES), :] for s in range(SUBLANES)], axis=1)


def _gather_start(idx_ref, src_hbm, dst, row0, n_tok, sem):
    def issue(g, carry):
        for u in range(DMA_GROUP):
            r = g * DMA_GROUP + u
            src = pl.multiple_of(idx_ref[0, 0, r] * SUBLANES, SUBLANES)
            row = pl.multiple_of(row0 + r * SUBLANES, SUBLANES)
            copy = pltpu.make_async_copy(src_hbm.at[pl.ds(src, SUBLANES)], dst.at[pl.ds(row, SUBLANES)], sem)
            copy.start(priority=u % 2)
        return carry

    lax.fori_loop(0, n_tok // DMA_GROUP, issue, 0)


def _gather_wait(src_hbm, dst, row0, n_tok, sem):
    rows = n_tok * SUBLANES
    pltpu.make_async_copy(src_hbm.at[pl.ds(0, rows)], dst.at[pl.ds(row0, rows)], sem).wait()


def _dispatch_kernel(pend_ref, dest_ref, h_ref, x_hbm, zero, sem, zsem):
    i = pl.program_id(0)
    tm = dest_ref.shape[2] // TOP_K
    bm = MOE_BLOCK

    def token_rows(ref, tok):
        return ref.at[pl.ds(pl.multiple_of(tok * SUBLANES, SUBLANES), SUBLANES)]

    def drain(n_tok, s):
        rows = n_tok * SUBLANES
        pltpu.make_async_copy(zero.at[pl.ds(0, rows)], x_hbm.at[pl.ds(0, rows)], s).wait()

    @pl.when(i == 0)
    def _():
        zero[...] = jnp.zeros_like(zero)
        n_blocks = x_hbm.shape[0] // (bm * SUBLANES)
        used = pend_ref[N_EXPERTS - 1] // bm

        def zero_block(first_tok):
            row = pl.multiple_of(first_tok * SUBLANES, bm * SUBLANES)
            pltpu.make_async_copy(zero, x_hbm.at[pl.ds(row, bm * SUBLANES)], zsem).start()

        def per_expert(e, carry):
            zero_block(jnp.maximum(pend_ref[e] - bm, 0))
            return carry

        def per_unused(blk, carry):
            zero_block(blk * bm)
            return carry

        def drain_block(blk, carry):
            drain(bm, zsem)
            return carry

        lax.fori_loop(0, N_EXPERTS, per_expert, 0)
        lax.fori_loop(used, n_blocks, per_unused, 0)
        lax.fori_loop(0, N_EXPERTS + n_blocks - used, drain_block, 0)

    for k in range(TOP_K):
        def issue(g, carry, k=k):
            for u in range(DMA_GROUP):
                r = g * DMA_GROUP + u
                dst = token_rows(x_hbm, dest_ref[0, 0, k * tm + r])
                pltpu.make_async_copy(token_rows(h_ref, r), dst, sem).start(priority=u % 2)
            return carry

        lax.fori_loop(0, tm // DMA_GROUP, issue, 0)

    for k in range(TOP_K):
        pltpu.make_async_copy(h_ref, x_hbm.at[pl.ds(0, tm * SUBLANES)], sem).wait()


def _dispatch(pend, dest_tiles, h2, n_rows):
    n, _, tm2 = dest_tiles.shape
    tm = tm2 // TOP_K
    grid_spec = pltpu.PrefetchScalarGridSpec(
        num_scalar_prefetch=1,
        grid=(n,),
        in_specs=[pl.BlockSpec((1, 1, tm2), lambda i, pe: (i, 0, 0), memory_space=pltpu.SMEM),
                  pl.BlockSpec((tm * SUBLANES, LANES), lambda i, pe: (i, 0))],
        out_specs=pl.BlockSpec(memory_space=pl.ANY),
        scratch_shapes=[pltpu.VMEM((MOE_BLOCK * SUBLANES, LANES), F32), pltpu.SemaphoreType.DMA(()),
                        pltpu.SemaphoreType.DMA(())],
    )
    return pl.pallas_call(
        _dispatch_kernel,
        grid_spec=grid_spec,
        out_shape=jax.ShapeDtypeStruct((n_rows * SUBLANES, LANES), F32),
        compiler_params=_params(("arbitrary",)),
        name="dispatch",
    )(pend, dest_tiles, h2)


def _expert_kernel(blk_e_ref, nused_ref, x_ref, wg_ref, wu_ref, wd_ref, y_ref, wg16, wu16, wd16):
    i = pl.program_id(0)
    nused = nused_ref[0]
    bm = MOE_BLOCK

    @pl.when((i == 0) | (blk_e_ref[i] != blk_e_ref[jnp.maximum(i - 1, 0)]))
    def _():
        wg16[...] = wg_ref[0].astype(BF16)
        wu16[...] = wu_ref[0].astype(BF16)
        wd16[...] = wd_ref[0].astype(BF16)

    @pl.when(i < nused)
    def _():
        x = _from_token_tiles(x_ref, 0, bm).astype(BF16)
        g = _dot(x, wg16[...])
        u = _dot(x, wu16[...])
        a = (g * jax.nn.sigmoid(g) * u).astype(BF16)
        _to_token_tiles(y_ref, _dot(a, wd16[...]))

    @pl.when(i >= nused)
    def _():
        y_ref[...] = jnp.zeros_like(y_ref)


def _experts(blk_e, nused, xb, wg, wu, wd):
    nb = blk_e.shape[0]
    bm = MOE_BLOCK
    D, de = wg.shape[1:]
    assert D == SUBLANES * LANES and xb.shape == (nb * bm * SUBLANES, LANES)
    grid_spec = pltpu.PrefetchScalarGridSpec(
        num_scalar_prefetch=2,
        grid=(nb,),
        in_specs=[pl.BlockSpec((bm * SUBLANES, LANES), lambda i, be, nu: (jnp.minimum(i, nu[0] - 1), 0)),
                  pl.BlockSpec((1, D, de), lambda i, be, nu: (be[i], 0, 0)),
                  pl.BlockSpec((1, D, de), lambda i, be, nu: (be[i], 0, 0)),
                  pl.BlockSpec((1, de, D), lambda i, be, nu: (be[i], 0, 0))],
        out_specs=pl.BlockSpec((bm * SUBLANES, LANES), lambda i, be, nu: (i, 0)),
        scratch_shapes=[pltpu.VMEM((D, de), BF16), pltpu.VMEM((D, de), BF16), pltpu.VMEM((de, D), BF16)],
    )
    return pl.pallas_call(
        _expert_kernel,
        grid_spec=grid_spec,
        out_shape=jax.ShapeDtypeStruct((nb * bm * SUBLANES, LANES), F32),
        compiler_params=_params(("arbitrary",)),
        name="experts",
    )(blk_e, nused, xb, wg, wu, wd)


def _combine_kernel(n_steps, cur_ref, nxt_ref, y_hbm, x1_ref, route_ref, gt_ref, gf_ref, o_ref, buf, sem):
    i = pl.program_id(0)
    tm = x1_ref.shape[0]
    slot = i % 2
    rows = 2 * tm * SUBLANES

    @pl.when(i == 0)
    def _():
        _gather_start(cur_ref, y_hbm, buf, 0, 2 * tm, sem.at[0])

    @pl.when(i + 1 < n_steps)
    def _():
        _gather_start(nxt_ref, y_hbm, buf, (1 - slot) * rows, 2 * tm, sem.at[1 - slot])

    _gather_wait(y_hbm, buf, slot * rows, 2 * tm, sem.at[slot])
    route = route_ref[...]
    y1 = _from_token_tiles(buf, slot * rows, tm)
    y2 = _from_token_tiles(buf, slot * rows + tm * SUBLANES, tm)
    x = x1_ref[...] + gt_ref[0] * (route[:, 2:3] * y1 + route[:, 3:4] * y2)
    o_ref[...] = x * lax.rsqrt(jnp.mean(x * x, axis=-1, keepdims=True) + EPS) * gf_ref[...]


def _combine(tile0, n_tok, dest_tiles, yb, x1, route, gt2, g_final, L):
    T, D = x1.shape
    tm = TOKEN_TILE
    per_b = L // tm
    n = n_tok // tm
    last = tile0 + n - 1
    smem = lambda f: pl.BlockSpec((1, 1, 2 * tm), lambda i: (f(i), 0, 0), memory_space=pltpu.SMEM)
    return pl.pallas_call(
        functools.partial(_combine_kernel, n),
        grid=(n,),
        in_specs=[smem(lambda i: tile0 + i), smem(lambda i: jnp.minimum(tile0 + i + 1, last)),
                  pl.BlockSpec(memory_space=pl.ANY),
                  pl.BlockSpec((tm, D), lambda i: (tile0 + i, 0)),
                  pl.BlockSpec((tm, LANES), lambda i: (tile0 + i, 0)),
                  pl.BlockSpec((1, 1, D), lambda i: ((tile0 + i) // per_b, 0, 0)),
                  pl.BlockSpec((1, D), lambda i: (0, 0))],
        out_specs=pl.BlockSpec((tm, D), lambda i: (i, 0)),
        out_shape=jax.ShapeDtypeStruct((n_tok, D), F32),
        scratch_shapes=[pltpu.VMEM((2 * 2 * tm * SUBLANES, LANES), F32), pltpu.SemaphoreType.DMA((2,))],
        compiler_params=_params(("arbitrary",)),
        name="combine",
    )(dest_tiles, dest_tiles, yb, x1, route, gt2, g_final.reshape(1, D))


def _encoder(xp, xs, c, p, g_final):
    n_prompt, L, D = xp.shape
    B = n_prompt + xs.shape[0]
    T = B * L
    xa, xb = xp.reshape(-1, D), xs.reshape(-1, D)
    d_f = N_FOURIER_GROUPS * FOURIER_GROUP_DIM
    d_h = (HYENA_ORDER + 1) * D_HYENA
    d_g = 2 * D

    mod = _ada(c, p["w_ada"], p["b_ada"])
    sh1, sc1, gt1, sh2, sc2, gt2 = [mod[:, k * D:(k + 1) * D].reshape(B, 1, D) for k in range(6)]

    uf, uh, sg = _inproj(xa, xb, sc1, sh1, p["w_in"].astype(BF16), p["b_in"][None, :], L, d_f, d_h, d_g)
    f = _fourier(uf.reshape(B, L, d_f))

    ktime, ksum = _filt_time(L, p["filt_w1"], p["filt_b1"], p["filt_w2"], p["filt_b2"],
                             p["filt_w3"], p["filt_b3"], p["filt_freq"], p["filt_wout"])
    kf = _filt_fft(L, ktime, ksum)
    uh3 = uh.reshape(B, L, d_h)
    conv_b = p["conv_b"][None, :]
    ncol = D_HYENA // FFT_COLS
    skip = p["hyena_skip"]
    z = _hyena_order(0, uh3, 0, uh3, ncol, p["conv_w"], conv_b, skip[0:1], kf, L)
    z = _hyena_order(1, z, 0, uh3, 2 * ncol, p["conv_w"], conv_b, skip[1:2], kf, L)

    rw = jnp.zeros((D, LANES), F32).at[:, :N_GROUPS].set(p["router_w1"])
    rw = rw.at[:, N_GROUPS:N_GROUPS + N_EXPERTS].set(p["router_w2"])
    rb = jnp.zeros((1, LANES), F32).at[0, :N_GROUPS].set(p["router_b1"])
    rb = rb.at[0, N_GROUPS:N_GROUPS + N_EXPERTS].set(p["router_b2"])
    x1, h2, route, route_t, counts = _merge(f.reshape(T, d_f), z.reshape(T, D_HYENA), sg, xa, xb,
                                            gt1, sc2, sh2, p["w_four"].astype(BF16), p["w_hyena"].astype(BF16),
                                            p["w_out"].astype(BF16), rw, rb, L)

    bm = MOE_BLOCK
    tm = TOKEN_TILE
    nb = (T * TOP_K) // bm + N_EXPERTS
    e = route_t[0:2].astype(jnp.int32)
    rank = route_t[4:6].astype(jnp.int32)
    cnt = counts[0, :N_EXPERTS].astype(jnp.int32)
    pcnt = (cnt + bm - 1) // bm * bm
    pend = jnp.cumsum(pcnt)
    experts = jnp.arange(N_EXPERTS, dtype=jnp.int32)
    dest = rank + jnp.sum(jnp.where(e[..., None] == experts, pend - pcnt, 0), axis=-1)
    blk_row0 = jnp.arange(nb, dtype=jnp.int32)[:, None] * bm
    blk_e = jnp.minimum(jnp.sum((pend[None, :] <= blk_row0).astype(jnp.int32), axis=1), N_EXPERTS - 1)
    nused = (pend[-1] // bm).astype(jnp.int32).reshape(1)
    dest_tiles = dest.reshape(TOP_K, T // tm, tm).transpose(1, 0, 2).reshape(T // tm, 1, TOP_K * tm)
    xb = _dispatch(pend.astype(jnp.int32), dest_tiles, h2, nb * bm)
    yb = _experts(blk_e, nused, xb, p["exp_w_gate"], p["exp_w_up"], p["exp_w_down"])

    t_prompt = n_prompt * L
    outs = []
    for tile0, n_tok in ((0, t_prompt), (t_prompt // tm, T - t_prompt)):
        outs.append(_combine(tile0, n_tok, dest_tiles, yb, x1, route, gt2, g_final, L))
    return outs[0].reshape(xp.shape), outs[1].reshape(xs.shape)


def kernel(x_prompt, x_sample, c_prompt, c_sample, w_ada, b_ada, w_in, b_in, conv_w, conv_b, filt_w1, filt_b1, filt_w2, filt_b2, filt_w3, filt_b3, filt_freq, filt_wout, hyena_skip, w_four, w_hyena, w_out, router_w1, router_b1, router_w2, router_b2, exp_w_gate, exp_w_up, exp_w_down, g_final):
    assert w_ada.shape[0] == 1, "single-layer block"
    assert x_prompt.shape[1:] == x_sample.shape[1:], "both request groups share sequence length and width"
    p = dict(w_ada=w_ada[0], b_ada=b_ada[0], w_in=w_in[0], b_in=b_in[0], conv_w=conv_w[0], conv_b=conv_b[0],
             filt_w1=filt_w1[0], filt_b1=filt_b1[0], filt_w2=filt_w2[0], filt_b2=filt_b2[0],
             filt_w3=filt_w3[0], filt_b3=filt_b3[0], filt_freq=filt_freq[0], filt_wout=filt_wout[0],
             hyena_skip=hyena_skip[0], w_four=w_four[0], w_hyena=w_hyena[0], w_out=w_out[0],
             router_w1=router_w1[0], router_b1=router_b1[0], router_w2=router_w2[0], router_b2=router_b2[0],
             exp_w_gate=exp_w_gate[0], exp_w_up=exp_w_up[0], exp_w_down=exp_w_down[0])
    c = jnp.concatenate([c_prompt, c_sample], axis=0)
    return _encoder(x_prompt, x_sample, c, p, g_final)
```
